```python
import math
import jax, jax.numpy as jnp
from jax import lax
import numpy as np

D_MODEL = 1024
BATCH = 2
SEQ = 8192
DEPTH = 2

N_A_LAYERS = (DEPTH + 1) // 2
N_B_LAYERS = DEPTH // 2
DEEPNORM_ALPHA = (2.0 * DEPTH) ** 0.25
DEEPNORM_BETA = (8.0 * DEPTH) ** -0.25
LN_EPS = 1e-5
RMS_EPS = 1e-5

GLA_HEADS = 4
GLA_DK_TOTAL = D_MODEL // 2
GLA_DV_TOTAL = D_MODEL
GLA_DK = GLA_DK_TOTAL // GLA_HEADS
GLA_DV = GLA_DV_TOTAL // GLA_HEADS
GLA_GATE_RANK = 16
GLA_GATE_TAU = 16.0
GLA_CHUNK = 64
GLA_IN = 2 * GLA_DK_TOTAL + 2 * GLA_DV_TOTAL + GLA_GATE_RANK

SWA_GROUPS = ((128, 1), (512, 4), (2048, 16))
SWA_N_GROUPS = len(SWA_GROUPS)
SWA_HEAD_DIM = 128
SWA_Q_HEADS = 8
SWA_KV_HEADS = 2
SWA_REP = SWA_Q_HEADS // SWA_KV_HEADS
SWA_OUT = SWA_Q_HEADS * SWA_HEAD_DIM
SWA_Q_TOTAL = SWA_N_GROUPS * SWA_OUT
SWA_IN = SWA_Q_TOTAL + SWA_OUT
SWA_KV_HALF = SWA_N_GROUPS * SWA_KV_HEADS * SWA_HEAD_DIM
SWA_KV_TOTAL = 2 * SWA_KV_HALF
SWA_BLOCK = 128
ROPE_THETA = 10000.0

kernel_name = "yoco_gla_dilated_window_hybrid"


def layer_norm(x, g, b):
    x32 = x.astype(jnp.float32)
    mu = jnp.mean(x32, axis=-1, keepdims=True)
    xc = x32 - mu
    var = jnp.mean(xc * xc, axis=-1, keepdims=True)
    return (xc * lax.rsqrt(var + LN_EPS) * g.astype(jnp.float32) + b.astype(jnp.float32)).astype(x.dtype)


def rope(x, pos):
    e = x.shape[-1]
    half = e // 2
    inv = ROPE_THETA ** (-(jnp.arange(half, dtype=jnp.float32) * 2.0) / e)
    ang = pos.astype(jnp.float32)[:, None] * inv[None, :]
    cos = jnp.cos(ang)[None, :, None, :]
    sin = jnp.sin(ang)[None, :, None, :]
    x32 = x.astype(jnp.float32)
    x1, x2 = x32[..., :half], x32[..., half:]
    return jnp.concatenate([x1 * cos - x2 * sin, x2 * cos + x1 * sin], axis=-1)


def gla_chunked(q, k, v, log_a):
    bsz, s, h, dk = q.shape
    dv = v.shape[-1]
    c = GLA_CHUNK
    n = s // c

    def to_chunks(t):
        return t.reshape(bsz, n, c, h, t.shape[-1]).transpose(1, 0, 3, 2, 4)

    qc_all, kc_all, vc_all = to_chunks(q), to_chunks(k), to_chunks(v)
    b_all = jnp.cumsum(to_chunks(log_a), axis=3)
    causal = jnp.tril(jnp.ones((c, c), dtype=bool))[None, None, :, :, None]

    def step(state, inp):
        qc, kc, vc, bc = inp
        o_inter = jnp.einsum('bhcd,bhde->bhce', qc * jnp.exp(bc), state)
        diff = bc[:, :, :, None, :] - bc[:, :, None, :, :]
        decay = jnp.where(causal, jnp.exp(jnp.where(causal, diff, 0.0)), 0.0)
        att = jnp.einsum('bhid,bhjd,bhijd->bhij', qc, kc, decay)
        o_intra = jnp.einsum('bhij,bhje->bhie', att, vc)
        b_last = bc[:, :, -1:, :]
        new_state = jnp.exp(b_last[:, :, 0, :])[..., None] * state + \
            jnp.einsum('bhcd,bhce->bhde', kc * jnp.exp(b_last - bc), vc)
        return new_state, o_inter + o_intra

    state0 = jnp.zeros((bsz, h, dk, dv), dtype=jnp.float32)
    _, o = lax.scan(step, state0, (qc_all, kc_all, vc_all, b_all))
    return o.transpose(1, 0, 3, 2, 4).reshape(bsz, s, h, dv)


def gla_mixer(x, w_in, w_a2, b_a2, norm_g, w_out):
    bsz, s, _ = x.shape
    proj = x @ w_in
    q, k, v, g, a_lr = jnp.split(
        proj, [GLA_DK_TOTAL, 2 * GLA_DK_TOTAL, 2 * GLA_DK_TOTAL + GLA_DV_TOTAL,
               2 * GLA_DK_TOTAL + 2 * GLA_DV_TOTAL], axis=-1)
    q = q.astype(jnp.float32).reshape(bsz, s, GLA_HEADS, GLA_DK) * (GLA_DK ** -0.5)
    k = k.astype(jnp.float32).reshape(bsz, s, GLA_HEADS, GLA_DK)
    v = v.astype(jnp.float32).reshape(bsz, s, GLA_HEADS, GLA_DV)
    log_a = jax.nn.log_sigmoid((a_lr @ w_a2 + b_a2).astype(jnp.float32)) / GLA_GATE_TAU
    log_a = log_a.reshape(bsz, s, GLA_HEADS, GLA_DK)
    o = gla_chunked(q, k, v, log_a)
    o = o * lax.rsqrt(jnp.mean(o * o, axis=-1, keepdims=True) + RMS_EPS)
    o = o.reshape(bsz, s, GLA_DV_TOTAL) * norm_g.astype(jnp.float32)
    return (o.astype(x.dtype) * jax.nn.silu(g)) @ w_out


def shared_kv(x, w_kv):
    bsz, s, _ = x.shape
    pos = jnp.arange(s)
    kv = x @ w_kv
    k, v = jnp.split(kv, [SWA_KV_HALF], axis=-1)
    k = rope(k.reshape(bsz, s, SWA_N_GROUPS * SWA_KV_HEADS, SWA_HEAD_DIM), pos)
    k = k.reshape(bsz, s, SWA_N_GROUPS, SWA_KV_HEADS, SWA_HEAD_DIM)
    v = v.astype(jnp.float32).reshape(bsz, s, SWA_N_GROUPS, SWA_KV_HEADS, SWA_HEAD_DIM)
    return k, v


def dilated_window_attention(q, k, v, window, dilation):
    bsz, s, hq, e = q.shape
    hkv = k.shape[2]
    rep = hq // hkv
    blk = SWA_BLOCK
    w_sub = window // dilation
    span = dilation * blk
    seq_pad = -(-s // span) * span
    n_sub = seq_pad // dilation
    nb = n_sub // blk

    def to_strided(t):
        t = jnp.pad(t, ((0, 0), (0, seq_pad - s), (0, 0), (0, 0)))
        t = t.reshape(bsz, n_sub, dilation, t.shape[2], e).transpose(0, 2, 1, 3, 4)
        return t.reshape(bsz, dilation, nb, blk, t.shape[3], e)

    qb = to_strided(q).reshape(bsz, dilation, nb, blk, hkv, rep, e)
    kb, vb = to_strided(k), to_strided(v)

    def with_prev(t):
        prev = jnp.pad(t, ((0, 0), (0, 0), (1, 0), (0, 0), (0, 0), (0, 0)))[:, :, :-1]
        return jnp.concatenate([prev, t], axis=3)

    kcat, vcat = with_prev(kb), with_prev(vb)
    scores = jnp.einsum('brnqgpe,brnkge->brngpqk', qb, kcat)
    qi = jnp.arange(blk)[:, None]
    kj = jnp.arange(2 * blk)[None, :]
    rel = blk + qi - kj
    band = (rel >= 0) & (rel <= w_sub)
    mask = band[None] & ((jnp.arange(nb)[:, None, None] > 0) | (kj[None] >= blk))
    scores = jnp.where(mask[None, None, :, None, None], scores, -jnp.inf)
    m = jnp.max(scores, axis=-1, keepdims=True)
    p = jnp.exp(scores - m)
    den = jnp.sum(p, axis=-1)
    o = jnp.einsum('brngpqk,brnkge->brnqgpe', p, vcat) / jnp.moveaxis(den, -1, 3)[..., None]
    lse = jnp.moveaxis(m[..., 0] + jnp.log(den), -1, 3)

    def from_strided(t):
        t = t.reshape((bsz, dilation, n_sub) + t.shape[6 - 2 + 1:] if False else (bsz, dilation, n_sub, -1))
        return t

    o = o.reshape(bsz, dilation, n_sub, hq, e).transpose(0, 2, 1, 3, 4).reshape(bsz, seq_pad, hq, e)[:, :s]
    lse = lse.reshape(bsz, dilation, n_sub, hq).transpose(0, 2, 1, 3).reshape(bsz, seq_pad, hq)[:, :s]
    return o, lse


def dilated_mixer(x, w_in, w_out, k_sh, v_sh):
    bsz, s, _ = x.shape
    pos = jnp.arange(s)
    proj = x @ w_in
    q, g = jnp.split(proj, [SWA_Q_TOTAL], axis=-1)
    q = rope(q.reshape(bsz, s, SWA_N_GROUPS * SWA_Q_HEADS, SWA_HEAD_DIM), pos) * (SWA_HEAD_DIM ** -0.5)
    q = q.reshape(bsz, s, SWA_N_GROUPS, SWA_Q_HEADS, SWA_HEAD_DIM)
    outs, lses = [], []
    for gi, (window, dilation) in enumerate(SWA_GROUPS):
        o_g, lse_g = dilated_window_attention(q[:, :, gi], k_sh[:, :, gi], v_sh[:, :, gi], window, dilation)
        outs.append(o_g)
        lses.append(lse_g)
    wts = jax.nn.softmax(jnp.stack(lses, axis=0), axis=0)
    o = jnp.sum(wts[..., None] * jnp.stack(outs, axis=0), axis=0)
    o = o.reshape(bsz, s, SWA_OUT)
    return (o.astype(x.dtype) * jax.nn.silu(g)) @ w_out


def setup_inputs(seed: int = 0) -> dict:
    key = jax.random.key(seed)
    ks = jax.random.split(key, 12)
    f32 = jnp.float32
    x = jax.random.normal(ks[0], (BATCH, SEQ, D_MODEL), f32)
    gla_w_in = jax.random.normal(ks[1], (N_A_LAYERS, D_MODEL, GLA_IN), f32) * D_MODEL ** -0.5
    gla_w_a2 = jax.random.normal(ks[2], (N_A_LAYERS, GLA_GATE_RANK, GLA_DK_TOTAL), f32) * GLA_GATE_RANK ** -0.5
    gla_b_a2 = jax.random.normal(ks[3], (N_A_LAYERS, GLA_DK_TOTAL), f32) * 0.1
    gla_norm_g = 1.0 + 0.02 * jax.random.normal(ks[4], (N_A_LAYERS, GLA_DV_TOTAL), f32)
    gla_w_out = jax.random.normal(ks[5], (N_A_LAYERS, GLA_DV_TOTAL, D_MODEL), f32) * (GLA_DV_TOTAL ** -0.5 * DEEPNORM_BETA)
    w_kv = jax.random.normal(ks[6], (D_MODEL, SWA_KV_TOTAL), f32) * D_MODEL ** -0.5
    swa_w_in = jax.random.normal(ks[7], (N_B_LAYERS, D_MODEL, SWA_IN), f32) * D_MODEL ** -0.5
    swa_w_out = jax.random.normal(ks[8], (N_B_LAYERS, SWA_OUT, D_MODEL), f32) * (SWA_OUT ** -0.5 * DEEPNORM_BETA)
    ln_g = 1.0 + 0.02 * jax.random.normal(ks[9], (DEPTH, D_MODEL), f32)
    ln_b = 0.02 * jax.random.normal(ks[10], (DEPTH, D_MODEL), f32)
    return {"x": x, "gla_w_in": gla_w_in, "gla_w_a2": gla_w_a2, "gla_b_a2": gla_b_a2,
            "gla_norm_g": gla_norm_g, "gla_w_out": gla_w_out, "w_kv": w_kv,
            "swa_w_in": swa_w_in, "swa_w_out": swa_w_out, "ln_g": ln_g, "ln_b": ln_b}


def reference(x, gla_w_in, gla_w_a2, gla_b_a2, gla_norm_g, gla_w_out, w_kv,
              swa_w_in, swa_w_out, ln_g, ln_b):
    k_sh = None
    v_sh = None
    for layer in range(DEPTH):
        if layer < N_A_LAYERS:
            y = gla_mixer(x, gla_w_in[layer], gla_w_a2[layer], gla_b_a2[layer],
                          gla_norm_g[layer], gla_w_out[layer])
        else:
            if layer == N_A_LAYERS:
                k_sh, v_sh = shared_kv(x, w_kv)
            i = layer - N_A_LAYERS
            y = dilated_mixer(x, swa_w_in[i], swa_w_out[i], k_sh, v_sh)
        x = layer_norm(DEEPNORM_ALPHA * x + y, ln_g[layer], ln_b[layer])
    return x
```

```python
import functools

import jax
import jax.numpy as jnp
from jax import lax
from jax.experimental import pallas as pl
from jax.experimental.pallas import tpu as pltpu

BF16 = jnp.bfloat16
F32 = jnp.float32

D_MODEL = 1024
DEPTH = 2
DEEPNORM_ALPHA = (2.0 * DEPTH) ** 0.25
LN_EPS = 1e-5
RMS_EPS = 1e-5

GLA_HEADS = 4
GLA_DK = 128
GLA_DV = 256
GLA_DK_TOTAL = GLA_HEADS * GLA_DK
GLA_DV_TOTAL = GLA_HEADS * GLA_DV
GLA_GATE_RANK = 16
GLA_GATE_TAU = 16.0
GLA_CHUNK = 64
GLA_BLOCK = 512

SWA_GROUPS = ((128, 1), (512, 4), (2048, 16))
SWA_N_GROUPS = len(SWA_GROUPS)
SWA_HEAD_DIM = 128
SWA_Q_HEADS = 8
SWA_KV_HEADS = 2
SWA_REP = SWA_Q_HEADS // SWA_KV_HEADS
SWA_OUT = SWA_Q_HEADS * SWA_HEAD_DIM
SWA_KV_W = SWA_KV_HEADS * SWA_HEAD_DIM
SWA_BLOCK = 128
SWA_SPAN = 2048
SWA_LSE_LANES = SWA_HEAD_DIM // SWA_Q_HEADS
ROPE_THETA = 10000.0

LANES = 128
ROW_TILE = 512
VMEM_LIMIT = 56 * 1024 * 1024

_NT = (((1,), (1,)), ((), ()))
_TN = (((0,), (0,)), ((), ()))


def _params(*sem):
    return pltpu.CompilerParams(dimension_semantics=sem, vmem_limit_bytes=VMEM_LIMIT)


def _full(shape):
    return pl.BlockSpec(shape, lambda *_: (0,) * len(shape))


def _layer_norm(h, g, b):
    mu = jnp.mean(h, axis=-1, keepdims=True)
    hc = h - mu
    var = jnp.mean(hc * hc, axis=-1, keepdims=True)
    return hc * lax.rsqrt(var + LN_EPS) * g + b


def _silu(g):
    return g * jax.nn.sigmoid(g)


def _gla_in_kernel(x_ref, w_ref, walr_ref, wa2_ref, ba2_ref, q_ref, k_ref, v_ref, g_ref, la_ref):
    xb = x_ref[...].astype(BF16)

    def proj(lo, hi):
        return jnp.dot(xb, w_ref[:, lo:hi], preferred_element_type=F32)

    q_ref[...] = (proj(0, GLA_DK_TOTAL) * (GLA_DK ** -0.5)).astype(BF16)
    k_ref[...] = proj(GLA_DK_TOTAL, 2 * GLA_DK_TOTAL).astype(BF16)
    v_ref[...] = proj(2 * GLA_DK_TOTAL, 2 * GLA_DK_TOTAL + GLA_DV_TOTAL).astype(BF16)
    g_ref[...] = proj(2 * GLA_DK_TOTAL + GLA_DV_TOTAL, 2 * GLA_DK_TOTAL + 2 * GLA_DV_TOTAL).astype(BF16)
    a_lr = jnp.dot(xb, walr_ref[...], preferred_element_type=F32)
    z = jnp.dot(a_lr.astype(BF16), wa2_ref[...], preferred_element_type=F32) + ba2_ref[...]
    log_sig = jnp.minimum(z, 0.0) - jnp.log1p(jnp.exp(-jnp.abs(z)))
    la_ref[...] = log_sig * (1.0 / GLA_GATE_TAU)


def _gla_in_proj(x2, w_main, w_alr, w_a2, b_a2):
    m = x2.shape[0]
    tm = ROW_TILE
    row = lambda n: pl.BlockSpec((tm, n), lambda i: (i, 0))
    return pl.pallas_call(
        _gla_in_kernel,
        grid=(m // tm,),
        in_specs=[row(D_MODEL), _full(w_main.shape), _full(w_alr.shape), _full(w_a2.shape), _full(b_a2.shape)],
        out_specs=[row(GLA_DK_TOTAL), row(GLA_DK_TOTAL), row(GLA_DV_TOTAL), row(GLA_DV_TOTAL), row(GLA_DK_TOTAL)],
        out_shape=[jax.ShapeDtypeStruct((m, GLA_DK_TOTAL), BF16), jax.ShapeDtypeStruct((m, GLA_DK_TOTAL), BF16),
                   jax.ShapeDtypeStruct((m, GLA_DV_TOTAL), BF16), jax.ShapeDtypeStruct((m, GLA_DV_TOTAL), BF16),
                   jax.ShapeDtypeStruct((m, GLA_DK_TOTAL), F32)],
        compiler_params=_params("parallel"),
        name="gla_in_proj",
    )(x2, w_main, w_alr, w_a2, b_a2)


def _gla_rec_kernel(q_ref, k_ref, v_ref, g_ref, la_ref, ng_ref, o_ref,
                    st_ref, b_s, qe_s, ke_s, kl_s, o_s):
    t = la_ref.shape[1]
    c = GLA_CHUNK
    n_chunks = t // c

    @pl.when(pl.program_id(1) == 0)
    def _():
        st_ref[...] = jnp.zeros_like(st_ref)

    la = la_ref[0]
    row = lax.broadcasted_iota(jnp.int32, (t, t), 0)
    col = lax.broadcasted_iota(jnp.int32, (t, t), 1)
    tri = jnp.where((col <= row) & (row // c == col // c), 1.0, 0.0).astype(BF16)
    la_hi = la.astype(BF16)
    la_lo = (la - la_hi.astype(F32)).astype(BF16)
    b = jnp.dot(tri, la_hi, preferred_element_type=F32) + jnp.dot(tri, la_lo, preferred_element_type=F32)
    b3 = b.reshape(n_chunks, c, GLA_DK_TOTAL)
    b_last = b3[:, c - 1:c, :]
    kf = k_ref[0].astype(F32)
    b_s[...] = b
    qe_s[...] = (q_ref[0].astype(F32) * jnp.exp(b)).astype(BF16)
    ke_s[...] = (kf * jnp.exp(-b)).astype(BF16)
    kl_s[...] = (kf.reshape(n_chunks, c, GLA_DK_TOTAL) * jnp.exp(b_last - b3)).reshape(t, GLA_DK_TOTAL).astype(BF16)

    ci = lax.broadcasted_iota(jnp.int32, (c, c), 0)
    cj = lax.broadcasted_iota(jnp.int32, (c, c), 1)
    causal = cj <= ci

    def chunk(ic, carry):
        r0 = pl.multiple_of(ic * c, c)
        rows = pl.ds(r0, c)
        decay = jnp.exp(b_s[pl.ds(r0 + c - 1, 1), :])
        for h in range(GLA_HEADS):
            kcols = slice(h * GLA_DK, (h + 1) * GLA_DK)
            vcols = slice(h * GLA_DV, (h + 1) * GLA_DV)
            qe = qe_s[rows, kcols]
            st = st_ref[h]
            vv = v_ref[0, rows, vcols]
            o_inter = lax.dot_general(qe, st.astype(BF16), _NT, preferred_element_type=F32)
            att = lax.dot_general(qe, ke_s[rows, kcols], _NT, preferred_element_type=F32)
            att = jnp.where(causal, att, 0.0).astype(BF16)
            o_s[rows, vcols] = o_inter + jnp.dot(att, vv, preferred_element_type=F32)
            upd = lax.dot_general(vv, kl_s[rows, kcols], _TN, preferred_element_type=F32)
            st_ref[h] = st * decay[:, kcols] + upd
        return carry

    lax.fori_loop(0, n_chunks, chunk, 0)

    for h in range(GLA_HEADS):
        vcols = slice(h * GLA_DV, (h + 1) * GLA_DV)
        o = o_s[:, vcols]
        o = o * lax.rsqrt(jnp.mean(o * o, axis=-1, keepdims=True) + RMS_EPS) * ng_ref[:, vcols]
        o_ref[0, :, vcols] = (o * _silu(g_ref[0, :, vcols].astype(F32))).astype(BF16)


def _gla_recurrence(q, k, v, g, la, norm_g):
    bsz, s, _ = q.shape
    t = GLA_BLOCK
    blk = lambda n: pl.BlockSpec((1, t, n), lambda b, i: (b, i, 0))
    return pl.pallas_call(
        _gla_rec_kernel,
        grid=(bsz, s // t),
        in_specs=[blk(GLA_DK_TOTAL), blk(GLA_DK_TOTAL), blk(GLA_DV_TOTAL), blk(GLA_DV_TOTAL), blk(GLA_DK_TOTAL),
                  _full(norm_g.shape)],
        out_specs=blk(GLA_DV_TOTAL),
        out_shape=jax.ShapeDtypeStruct((bsz, s, GLA_DV_TOTAL), BF16),
        scratch_shapes=[pltpu.VMEM((GLA_HEADS, GLA_DV, GLA_DK), F32),
                        pltpu.VMEM((t, GLA_DK_TOTAL), F32),
                        pltpu.VMEM((t, GLA_DK_TOTAL), BF16),
                        pltpu.VMEM((t, GLA_DK_TOTAL), BF16),
                        pltpu.VMEM((t, GLA_DK_TOTAL), BF16),
                        pltpu.VMEM((t, GLA_DV_TOTAL), F32)],
        compiler_params=_params("parallel", "arbitrary"),
        name="gla_recurrence",
    )(q, k, v, g, la, norm_g)


def _out_ln_kernel(a_ref, x_ref, w_ref, lg_ref, lb_ref, o_ref):
    y = jnp.dot(a_ref[...], w_ref[...], preferred_element_type=F32)
    o_ref[...] = _layer_norm(DEEPNORM_ALPHA * x_ref[...] + y, lg_ref[...], lb_ref[...])


def _out_ln(a, x2, w_out, ln_g, ln_b):
    m = x2.shape[0]
    tm = ROW_TILE
    row = lambda n: pl.BlockSpec((tm, n), lambda i: (i, 0))
    return pl.pallas_call(
        _out_ln_kernel,
        grid=(m // tm,),
        in_specs=[row(a.shape[1]), row(D_MODEL), _full(w_out.shape), _full(ln_g.shape), _full(ln_b.shape)],
        out_specs=row(D_MODEL),
        out_shape=jax.ShapeDtypeStruct((m, D_MODEL), F32),
        compiler_params=_params("parallel"),
        name="out_ln",
    )(a, x2, w_out, ln_g, ln_b)


def _rope(h, cos, sin_signed):
    return h * cos + pltpu.roll(h, SWA_HEAD_DIM // 2, 1) * sin_signed


def _swa_in_kernel(x_ref, w_ref, cos_ref, sin_ref, *out_refs):
    k_refs = out_refs[0:SWA_N_GROUPS]
    v_refs = out_refs[SWA_N_GROUPS:2 * SWA_N_GROUPS]
    q_refs = out_refs[2 * SWA_N_GROUPS:3 * SWA_N_GROUPS]
    g_ref = out_refs[3 * SWA_N_GROUPS]
    xb = x_ref[...].astype(BF16)
    cos = cos_ref[...]
    sin = sin_ref[...]
    e = SWA_HEAD_DIM
    kv_half = SWA_N_GROUPS * SWA_KV_W
    q0 = 2 * kv_half

    def proj(lo, n):
        return jnp.dot(xb, w_ref[:, lo:lo + n], preferred_element_type=F32)

    for gi in range(SWA_N_GROUPS):
        kk = proj(gi * SWA_KV_W, SWA_KV_W)
        for h in range(SWA_KV_HEADS):
            k_refs[gi][:, h * e:(h + 1) * e] = _rope(kk[:, h * e:(h + 1) * e], cos, sin).astype(BF16)
        v_refs[gi][...] = proj(kv_half + gi * SWA_KV_W, SWA_KV_W).astype(BF16)
        for h in range(SWA_Q_HEADS):
            qh = proj(q0 + gi * SWA_OUT + h * e, e)
            q_refs[gi][:, h * e:(h + 1) * e] = (_rope(qh, cos, sin) * (e ** -0.5)).astype(BF16)
    g_ref[...] = proj(q0 + SWA_N_GROUPS * SWA_OUT, SWA_OUT).astype(BF16)


def _swa_in_proj(x2, w_all, cos, sin_signed, seq):
    m = x2.shape[0]
    tm = ROW_TILE
    row = lambda n: pl.BlockSpec((tm, n), lambda i: (i, 0))
    pos = pl.BlockSpec((tm, SWA_HEAD_DIM), lambda i: (i % (seq // tm), 0))
    widths = [SWA_KV_W] * (2 * SWA_N_GROUPS) + [SWA_OUT] * (SWA_N_GROUPS + 1)
    outs = pl.pallas_call(
        _swa_in_kernel,
        grid=(m // tm,),
        in_specs=[row(D_MODEL), _full(w_all.shape), pos, pos],
        out_specs=[row(n) for n in widths],
        out_shape=[jax.ShapeDtypeStruct((m, n), BF16) for n in widths],
        compiler_params=_params("parallel"),
        name="swa_in_proj",
    )(x2, w_all, cos, sin_signed)
    return outs[0:3], outs[3:6], outs[6:9], outs[9]


def _swa_kernel(q_ref, kc_ref, kp_ref, vc_ref, vp_ref, o_ref, lse_ref, k_s, v_s, *, dilation):
    blk = SWA_BLOCK
    e = SWA_HEAD_DIM
    span_blocks = q_ref.shape[1] // blk
    n = pl.program_id(1)

    k_s[0:blk, :] = kp_ref[0]
    k_s[blk:, :] = kc_ref[0]
    v_s[0:blk, :] = vp_ref[0]
    v_s[blk:, :] = vc_ref[0]

    rows = SWA_REP * blk
    qi = lax.broadcasted_iota(jnp.int32, (rows, 2 * blk), 0) % blk
    kj = lax.broadcasted_iota(jnp.int32, (rows, 2 * blk), 1)
    rel = blk + qi - kj
    band = (rel >= 0) & (rel <= blk)
    lane_head = lax.broadcasted_iota(jnp.int32, (blk, LANES), 1) // SWA_LSE_LANES

    def unit(idx, carry):
        j = idx // dilation
        r = idx % dilation
        row0 = pl.multiple_of(j * blk, blk)
        first_key = jnp.where((j > 0) | (n > 0), 0, blk)
        mask = band & (kj >= first_key)
        lse_tile = jnp.zeros((blk, LANES), F32)
        for kvh in range(SWA_KV_HEADS):
            qcol = pl.multiple_of(r * SWA_OUT + kvh * SWA_REP * e, LANES)
            q4 = q_ref[0, pl.ds(row0, blk), pl.ds(qcol, SWA_REP * e)]
            qs = jnp.concatenate([q4[:, h * e:(h + 1) * e] for h in range(SWA_REP)], axis=0)
            kcol = pl.multiple_of(r * SWA_KV_W + kvh * e, LANES)
            kk = k_s[pl.ds(row0, 2 * blk), pl.ds(kcol, e)]
            vv = v_s[pl.ds(row0, 2 * blk), pl.ds(kcol, e)]
            s = lax.dot_general(qs, kk, _NT, preferred_element_type=F32)
            s = jnp.where(mask, s, -jnp.inf)
            m = jnp.max(s, axis=-1, keepdims=True)
            p = jnp.exp(s - m)
            den = jnp.sum(p, axis=-1, keepdims=True)
            o = jnp.dot(p.astype(BF16), vv, preferred_element_type=F32) / den
            lse = m + jnp.log(den)
            for h in range(SWA_REP):
                ocol = pl.multiple_of(r * SWA_OUT + (kvh * SWA_REP + h) * e, LANES)
                o_ref[0, pl.ds(row0, blk), pl.ds(ocol, e)] = o[h * blk:(h + 1) * blk].astype(BF16)
                lse_tile = jnp.where(lane_head == kvh * SWA_REP + h, lse[h * blk:(h + 1) * blk], lse_tile)
        lse_ref[0, pl.ds(row0, blk), pl.ds(pl.multiple_of(r * LANES, LANES), LANES)] = lse_tile
        return carry

    lax.fori_loop(0, span_blocks * dilation, unit, 0)


def _swa_attention(q, k, v, dilation):
    bsz, s, _ = q.shape
    d = dilation
    n_sub = s // d
    sub = SWA_SPAN // d
    qv = q.reshape(bsz, n_sub, d * SWA_OUT)
    kv = k.reshape(bsz, n_sub, d * SWA_KV_W)
    vv = v.reshape(bsz, n_sub, d * SWA_KV_W)
    cur = lambda w: pl.BlockSpec((1, sub, w), lambda b, n: (b, n, 0))
    prev = pl.BlockSpec((1, SWA_BLOCK, d * SWA_KV_W),
                        lambda b, n: (b, jnp.maximum(n * (sub // SWA_BLOCK) - 1, 0), 0))
    o, lse = pl.pallas_call(
        functools.partial(_swa_kernel, dilation=d),
        grid=(bsz, s // SWA_SPAN),
        in_specs=[cur(d * SWA_OUT), cur(d * SWA_KV_W), prev, cur(d * SWA_KV_W), prev],
        out_specs=[cur(d * SWA_OUT), cur(d * LANES)],
        out_shape=[jax.ShapeDtypeStruct((bsz, n_sub, d * SWA_OUT), BF16),
                   jax.ShapeDtypeStruct((bsz, n_sub, d * LANES), F32)],
        scratch_shapes=[pltpu.VMEM((SWA_BLOCK + sub, d * SWA_KV_W), BF16),
                        pltpu.VMEM((SWA_BLOCK + sub, d * SWA_KV_W), BF16)],
        compiler_params=_params("parallel", "parallel"),
        name=f"swa_attention_d{d}",
    )(qv, kv, kv, vv, vv)
    return o.reshape(bsz * s, SWA_OUT), lse.reshape(bsz * s, LANES)


def _merge_out_kernel(o0_ref, o1_ref, o2_ref, l0_ref, l1_ref, l2_ref, g_ref, x_ref, w_ref, lg_ref, lb_ref, out_ref):
    l0, l1, l2 = l0_ref[...], l1_ref[...], l2_ref[...]
    mx = jnp.maximum(jnp.maximum(l0, l1), l2)
    e0, e1, e2 = jnp.exp(l0 - mx), jnp.exp(l1 - mx), jnp.exp(l2 - mx)
    tot = e0 + e1 + e2
    src = lax.broadcasted_iota(jnp.int32, (LANES, SWA_OUT), 0)
    dst = lax.broadcasted_iota(jnp.int32, (LANES, SWA_OUT), 1)
    expand = jnp.where(src == (dst // SWA_HEAD_DIM) * SWA_LSE_LANES, 1.0, 0.0).astype(BF16)
    acc = None
    for e, o_ref in ((e0, o0_ref), (e1, o1_ref), (e2, o2_ref)):
        w = jnp.dot((e / tot).astype(BF16), expand, preferred_element_type=F32)
        term = w * o_ref[...].astype(F32)
        acc = term if acc is None else acc + term
    a = (acc * _silu(g_ref[...].astype(F32))).astype(BF16)
    y = jnp.dot(a, w_ref[...], preferred_element_type=F32)
    out_ref[...] = _layer_norm(DEEPNORM_ALPHA * x_ref[...] + y, lg_ref[...], lb_ref[...])


def _merge_out_ln(os, lses, g, x2, w_out, ln_g, ln_b):
    m = x2.shape[0]
    tm = ROW_TILE
    row = lambda n: pl.BlockSpec((tm, n), lambda i: (i, 0))
    return pl.pallas_call(
        _merge_out_kernel,
        grid=(m // tm,),
        in_specs=[row(SWA_OUT)] * 3 + [row(LANES)] * 3 + [row(SWA_OUT), row(D_MODEL), _full(w_out.shape),
                                                         _full(ln_g.shape), _full(ln_b.shape)],
        out_specs=row(D_MODEL),
        out_shape=jax.ShapeDtypeStruct((m, D_MODEL), F32),
        compiler_params=_params("parallel"),
        name="merge_out_ln",
    )(*os, *lses, g, x2, w_out, ln_g, ln_b)


def _rope_tables(seq):
    half = SWA_HEAD_DIM // 2
    inv = ROPE_THETA ** (-(jnp.arange(half, dtype=F32) * 2.0) / SWA_HEAD_DIM)
    ang = jnp.arange(seq).astype(F32)[:, None] * inv[None, :]
    cos, sin = jnp.cos(ang), jnp.sin(ang)
    return jnp.concatenate([cos, cos], axis=-1), jnp.concatenate([-sin, sin], axis=-1)


def kernel(x, gla_w_in, gla_w_a2, gla_b_a2, gla_norm_g, gla_w_out, w_kv, swa_w_in, swa_w_out, ln_g, ln_b):
    bsz, seq, _ = x.shape
    assert seq % SWA_SPAN == 0 and seq % GLA_BLOCK == 0 and (bsz * seq) % ROW_TILE == 0
    m = bsz * seq
    x2 = x.reshape(m, D_MODEL)

    n_main = 2 * GLA_DK_TOTAL + 2 * GLA_DV_TOTAL
    w_main = gla_w_in[0, :, :n_main].astype(BF16)
    w_alr = jnp.pad(gla_w_in[0, :, n_main:], ((0, 0), (0, LANES - GLA_GATE_RANK))).astype(BF16)
    w_a2 = jnp.pad(gla_w_a2[0], ((0, LANES - GLA_GATE_RANK), (0, 0))).astype(BF16)
    q, k, v, g, la = _gla_in_proj(x2, w_main, w_alr, w_a2, gla_b_a2[0][None, :])
    to3 = lambda a: a.reshape(bsz, seq, a.shape[-1])
    og = _gla_recurrence(to3(q), to3(k), to3(v), to3(g), to3(la), gla_norm_g[0][None, :])
    x2 = _out_ln(og.reshape(m, GLA_DV_TOTAL), x2, gla_w_out[0].astype(BF16), ln_g[0][None, :], ln_b[0][None, :])

    w_all = jnp.concatenate([w_kv, swa_w_in[0]], axis=1).astype(BF16)
    cos, sin_signed = _rope_tables(seq)
    ks, vs, qs, g = _swa_in_proj(x2, w_all, cos, sin_signed, seq)
    os, lses = [], []
    for gi, (_, dilation) in enumerate(SWA_GROUPS):
        o_g, lse_g = _swa_attention(to3(qs[gi]), to3(ks[gi]), to3(vs[gi]), dilation)
        os.append(o_g)
        lses.append(lse_g)
    out = _merge_out_ln(os, lses, g, x2, swa_w_out[0].astype(BF16), ln_g[1][None, :], ln_b[1][None, :])
    return out.reshape(bsz, seq, D_MODEL)
```

```python
import functools

import jax
import jax.numpy as jnp
from jax import lax
from jax.experimental import pallas as pl
from jax.experimental.pallas import tpu as pltpu

BF16 = jnp.bfloat16
F32 = jnp.float32

D_MODEL = 1024
DEPTH = 2
DEEPNORM_ALPHA = (2.0 * DEPTH) ** 0.25
LN_EPS = 1e-5
RMS_EPS = 1e-5

GLA_HEADS = 4
GLA_DK = 128
GLA_DV = 256
GLA_DK_TOTAL = GLA_HEADS * GLA_DK
GLA_DV_TOTAL = GLA_HEADS * GLA_DV
GLA_GATE_RANK = 16
GLA_GATE_TAU = 16.0
GLA_CHUNK = 64
GLA_BLOCK = 512

SWA_GROUPS = ((128, 1), (512, 4), (2048, 16))
SWA_N_GROUPS = len(SWA_GROUPS)
SWA_HEAD_DIM = 128
SWA_Q_HEADS = 8
SWA_KV_HEADS = 2
SWA_REP = SWA_Q_HEADS // SWA_KV_HEADS
SWA_OUT = SWA_Q_HEADS * SWA_HEAD_DIM
SWA_KV_W = SWA_KV_HEADS * SWA_HEAD_DIM
SWA_BLOCK = 128
SWA_SPAN = 2048
SWA_LSE_LANES = SWA_HEAD_DIM // SWA_Q_HEADS
ROPE_THETA = 10000.0

LANES = 128
ROW_TILE = 512
TILES_PER_SPAN = SWA_SPAN // ROW_TILE
VMEM_LIMIT = 56 * 1024 * 1024

_NT = (((1,), (1,)), ((), ()))
_TN = (((0,), (0,)), ((), ()))


def _params(*sem):
    return pltpu.CompilerParams(dimension_semantics=sem, vmem_limit_bytes=VMEM_LIMIT)


def _full(shape):
    return pl.BlockSpec(shape, lambda *_: (0,) * len(shape))


def _layer_norm(h, g, b):
    mu = jnp.mean(h, axis=-1, keepdims=True)
    hc = h - mu
    var = jnp.mean(hc * hc, axis=-1, keepdims=True)
    return hc * lax.rsqrt(var + LN_EPS) * g + b


def _silu(g):
    return g * jax.nn.sigmoid(g)


def _gla_in_kernel(x_ref, w_ref, walr_ref, wa2_ref, ba2_ref, q_ref, k_ref, v_ref, g_ref, la_ref):
    xb = x_ref[...].astype(BF16)

    def proj(lo, hi):
        return jnp.dot(xb, w_ref[:, lo:hi], preferred_element_type=F32)

    q_ref[...] = (proj(0, GLA_DK_TOTAL) * (GLA_DK ** -0.5)).astype(BF16)
    k_ref[...] = proj(GLA_DK_TOTAL, 2 * GLA_DK_TOTAL).astype(BF16)
    v_ref[...] = proj(2 * GLA_DK_TOTAL, 2 * GLA_DK_TOTAL + GLA_DV_TOTAL).astype(BF16)
    g_ref[...] = proj(2 * GLA_DK_TOTAL + GLA_DV_TOTAL, 2 * GLA_DK_TOTAL + 2 * GLA_DV_TOTAL).astype(BF16)
    a_lr = jnp.dot(xb, walr_ref[...], preferred_element_type=F32)
    z = jnp.dot(a_lr.astype(BF16), wa2_ref[...], preferred_element_type=F32) + ba2_ref[...]
    log_sig = jnp.minimum(z, 0.0) - jnp.log1p(jnp.exp(-jnp.abs(z)))
    la_ref[...] = log_sig * (1.0 / GLA_GATE_TAU)


def _gla_in_proj(x2, w_main, w_alr, w_a2, b_a2):
    m = x2.shape[0]
    tm = ROW_TILE
    row = lambda n: pl.BlockSpec((tm, n), lambda i: (i, 0))
    return pl.pallas_call(
        _gla_in_kernel,
        grid=(m // tm,),
        in_specs=[row(D_MODEL), _full(w_main.shape), _full(w_alr.shape), _full(w_a2.shape), _full(b_a2.shape)],
        out_specs=[row(GLA_DK_TOTAL), row(GLA_DK_TOTAL), row(GLA_DV_TOTAL), row(GLA_DV_TOTAL), row(GLA_DK_TOTAL)],
        out_shape=[jax.ShapeDtypeStruct((m, GLA_DK_TOTAL), BF16), jax.ShapeDtypeStruct((m, GLA_DK_TOTAL), BF16),
                   jax.ShapeDtypeStruct((m, GLA_DV_TOTAL), BF16), jax.ShapeDtypeStruct((m, GLA_DV_TOTAL), BF16),
                   jax.ShapeDtypeStruct((m, GLA_DK_TOTAL), F32)],
        compiler_params=_params("parallel"),
        name="gla_in_proj",
    )(x2, w_main, w_alr, w_a2, b_a2)


def _gla_rec_kernel(q_ref, k_ref, v_ref, g_ref, la_ref, ng_ref, o_ref,
                    st_ref, b_s, qe_s, ke_s, kl_s, o_s):
    t = la_ref.shape[1]
    c = GLA_CHUNK
    n_chunks = t // c

    @pl.when(pl.program_id(1) == 0)
    def _():
        st_ref[...] = jnp.zeros_like(st_ref)

    la = la_ref[0]
    row = lax.broadcasted_iota(jnp.int32, (t, t), 0)
    col = lax.broadcasted_iota(jnp.int32, (t, t), 1)
    tri = jnp.where((col <= row) & (row // c == col // c), 1.0, 0.0).astype(BF16)
    la_hi = la.astype(BF16)
    la_lo = (la - la_hi.astype(F32)).astype(BF16)
    b = jnp.dot(tri, la_hi, preferred_element_type=F32) + jnp.dot(tri, la_lo, preferred_element_type=F32)
    b3 = b.reshape(n_chunks, c, GLA_DK_TOTAL)
    b_last = b3[:, c - 1:c, :]
    kf = k_ref[0].astype(F32)
    b_s[...] = b
    qe_s[...] = (q_ref[0].astype(F32) * jnp.exp(b)).astype(BF16)
    ke_s[...] = (kf * jnp.exp(-b)).astype(BF16)
    kl_s[...] = (kf.reshape(n_chunks, c, GLA_DK_TOTAL) * jnp.exp(b_last - b3)).reshape(t, GLA_DK_TOTAL).astype(BF16)

    ci = lax.broadcasted_iota(jnp.int32, (c, c), 0)
    cj = lax.broadcasted_iota(jnp.int32, (c, c), 1)
    causal = cj <= ci

    def chunk(ic, carry):
        r0 = pl.multiple_of(ic * c, c)
        rows = pl.ds(r0, c)
        decay = jnp.exp(b_s[pl.ds(r0 + c - 1, 1), :])
        for h in range(GLA_HEADS):
            kcols = slice(h * GLA_DK, (h + 1) * GLA_DK)
            vcols = slice(h * GLA_DV, (h + 1) * GLA_DV)
            qe = qe_s[rows, kcols]
            st = st_ref[h]
            vv = v_ref[0, rows, vcols]
            o_inter = lax.dot_general(qe, st.astype(BF16), _NT, preferred_element_type=F32)
            att = lax.dot_general(qe, ke_s[rows, kcols], _NT, preferred_element_type=F32)
            att = jnp.where(causal, att, 0.0).astype(BF16)
            o_s[rows, vcols] = o_inter + jnp.dot(att, vv, preferred_element_type=F32)
            upd = lax.dot_general(vv, kl_s[rows, kcols], _TN, preferred_element_type=F32)
            st_ref[h] = st * decay[:, kcols] + upd
        return carry

    lax.fori_loop(0, n_chunks, chunk, 0, unroll=2)

    for h in range(GLA_HEADS):
        vcols = slice(h * GLA_DV, (h + 1) * GLA_DV)
        o = o_s[:, vcols]
        o = o * lax.rsqrt(jnp.mean(o * o, axis=-1, keepdims=True) + RMS_EPS) * ng_ref[:, vcols]
        o_ref[0, :, vcols] = (o * _silu(g_ref[0, :, vcols].astype(F32))).astype(BF16)


def _gla_recurrence(q, k, v, g, la, norm_g):
    bsz, s, _ = q.shape
    t = GLA_BLOCK
    blk = lambda n: pl.BlockSpec((1, t, n), lambda b, i: (b, i, 0))
    return pl.pallas_call(
        _gla_rec_kernel,
        grid=(bsz, s // t),
        in_specs=[blk(GLA_DK_TOTAL), blk(GLA_DK_TOTAL), blk(GLA_DV_TOTAL), blk(GLA_DV_TOTAL), blk(GLA_DK_TOTAL),
                  _full(norm_g.shape)],
        out_specs=blk(GLA_DV_TOTAL),
        out_shape=jax.ShapeDtypeStruct((bsz, s, GLA_DV_TOTAL), BF16),
        scratch_shapes=[pltpu.VMEM((GLA_HEADS, GLA_DV, GLA_DK), F32),
                        pltpu.VMEM((t, GLA_DK_TOTAL), F32),
                        pltpu.VMEM((t, GLA_DK_TOTAL), BF16),
                        pltpu.VMEM((t, GLA_DK_TOTAL), BF16),
                        pltpu.VMEM((t, GLA_DK_TOTAL), BF16),
                        pltpu.VMEM((t, GLA_DV_TOTAL), F32)],
        compiler_params=_params("parallel", "arbitrary"),
        name="gla_recurrence",
    )(q, k, v, g, la, norm_g)


def _out_ln_kernel(a_ref, x_ref, w_ref, lg_ref, lb_ref, o_ref):
    y = jnp.dot(a_ref[...], w_ref[...], preferred_element_type=F32)
    o_ref[...] = _layer_norm(DEEPNORM_ALPHA * x_ref[...] + y, lg_ref[...], lb_ref[...])


def _out_ln(a, x2, w_out, ln_g, ln_b):
    m = x2.shape[0]
    tm = ROW_TILE
    row = lambda n: pl.BlockSpec((tm, n), lambda i: (i, 0))
    return pl.pallas_call(
        _out_ln_kernel,
        grid=(m // tm,),
        in_specs=[row(a.shape[1]), row(D_MODEL), _full(w_out.shape), _full(ln_g.shape), _full(ln_b.shape)],
        out_specs=row(D_MODEL),
        out_shape=jax.ShapeDtypeStruct((m, D_MODEL), F32),
        compiler_params=_params("parallel"),
        name="out_ln",
    )(a, x2, w_out, ln_g, ln_b)


def _unit_shape(bsz, seq, d, width):
    return (bsz, seq // SWA_SPAN, SWA_SPAN // (SWA_BLOCK * d), d, SWA_BLOCK, width)


def _tile_unit_spec(d, width):
    rows_per = ROW_TILE // d
    if rows_per >= SWA_BLOCK:
        jb = rows_per // SWA_BLOCK
        return pl.BlockSpec((1, 1, jb, d, SWA_BLOCK, width), lambda b, n, t: (b, n, t, 0, 0, 0))
    per_j = SWA_BLOCK // rows_per
    return pl.BlockSpec((1, 1, 1, d, rows_per, width), lambda b, n, t: (b, n, t // per_j, 0, t % per_j, 0))


def _tile_unit_rows(d):
    rows_per = ROW_TILE // d
    return max(rows_per // SWA_BLOCK, 1), min(rows_per, SWA_BLOCK)


def _token_rows(d, jj, r, rows):
    return pl.ds(jj * SWA_BLOCK * d + r, rows, stride=d) if d > 1 else pl.ds(jj * SWA_BLOCK, rows)


def _rope(h, cos, sin_signed):
    return h * cos + pltpu.roll(h, SWA_HEAD_DIM // 2, 1) * sin_signed


def _swa_in_kernel(x_ref, w_ref, cos_ref, sin_ref, *refs):
    k_refs = refs[0:SWA_N_GROUPS]
    v_refs = refs[SWA_N_GROUPS:2 * SWA_N_GROUPS]
    q_refs = refs[2 * SWA_N_GROUPS:3 * SWA_N_GROUPS]
    g_ref = refs[3 * SWA_N_GROUPS]
    tok_s = refs[3 * SWA_N_GROUPS + 1]
    xb = x_ref[...].astype(BF16)
    cos = cos_ref[...]
    sin = sin_ref[...]
    e = SWA_HEAD_DIM
    kv_half = SWA_N_GROUPS * SWA_KV_W
    q0 = 2 * kv_half

    def proj(lo, n):
        return jnp.dot(xb, w_ref[:, lo:lo + n], preferred_element_type=F32)

    def scatter(val, out_ref, d):
        jb, rows = _tile_unit_rows(d)
        if d == 1:
            out_ref[0, 0] = val.astype(BF16).reshape(out_ref.shape[2:])
            return
        for c in range(val.shape[1] // LANES):
            tok_s[c] = val[:, c * LANES:(c + 1) * LANES]
        for jj in range(jb):
            for r in range(d):
                for c in range(val.shape[1] // LANES):
                    out_ref[0, 0, jj, r, :, c * LANES:(c + 1) * LANES] = (
                        tok_s[c, _token_rows(d, jj, r, rows), :].astype(BF16))

    for gi, (_, d) in enumerate(SWA_GROUPS):
        kk = proj(gi * SWA_KV_W, SWA_KV_W)
        kk = jnp.concatenate([_rope(kk[:, h * e:(h + 1) * e], cos, sin) for h in range(SWA_KV_HEADS)], axis=1)
        scatter(kk, k_refs[gi], d)
        scatter(proj(kv_half + gi * SWA_KV_W, SWA_KV_W), v_refs[gi], d)
        qq = proj(q0 + gi * SWA_OUT, SWA_OUT)
        qq = jnp.concatenate([_rope(qq[:, h * e:(h + 1) * e], cos, sin) * (e ** -0.5)
                              for h in range(SWA_Q_HEADS)], axis=1)
        scatter(qq, q_refs[gi], d)
    g_ref[...] = proj(q0 + SWA_N_GROUPS * SWA_OUT, SWA_OUT).astype(BF16)


def _swa_in_proj(x2, w_all, cos, sin_signed, bsz, seq):
    tm = ROW_TILE
    n_span = seq // SWA_SPAN
    tile = lambda b, n, t: (b * n_span + n) * TILES_PER_SPAN + t
    row = lambda w: pl.BlockSpec((tm, w), lambda b, n, t: (tile(b, n, t), 0))
    pos = pl.BlockSpec((tm, SWA_HEAD_DIM), lambda b, n, t: (n * TILES_PER_SPAN + t, 0))
    dils = [d for _, d in SWA_GROUPS]
    unit_out = [(d, SWA_KV_W) for d in dils] * 2 + [(d, SWA_OUT) for d in dils]
    outs = pl.pallas_call(
        _swa_in_kernel,
        grid=(bsz, n_span, TILES_PER_SPAN),
        in_specs=[row(D_MODEL), _full(w_all.shape), pos, pos],
        out_specs=[_tile_unit_spec(d, w) for d, w in unit_out] + [row(SWA_OUT)],
        out_shape=[jax.ShapeDtypeStruct(_unit_shape(bsz, seq, d, w), BF16) for d, w in unit_out]
        + [jax.ShapeDtypeStruct((bsz * seq, SWA_OUT), BF16)],
        scratch_shapes=[pltpu.VMEM((SWA_OUT // LANES, tm, LANES), F32)],
        compiler_params=_params("parallel", "parallel", "parallel"),
        name="swa_in_proj",
    )(x2, w_all, cos, sin_signed)
    return outs[0:3], outs[3:6], outs[6:9], outs[9]


def _swa_kernel(q_ref, kc_ref, kp_ref, vc_ref, vp_ref, o_ref, m_ref, l_ref, k_s, v_s, bias_s):
    blk = SWA_BLOCK
    e = SWA_HEAD_DIM
    n_j, d = q_ref.shape[2], q_ref.shape[3]
    n = pl.program_id(1)

    k_s[0] = kp_ref[0, 0, 0]
    k_s[1:] = kc_ref[0, 0]
    v_s[0] = vp_ref[0, 0, 0]
    v_s[1:] = vc_ref[0, 0]

    rows = SWA_REP * blk
    qi = lax.broadcasted_iota(jnp.int32, (rows, 2 * blk), 0) % blk
    kj = lax.broadcasted_iota(jnp.int32, (rows, 2 * blk), 1)
    rel = blk + qi - kj
    band = (rel >= 0) & (rel <= blk)
    bias_s[0] = jnp.where(band, 0.0, -jnp.inf)
    bias_s[1] = jnp.where(band & (kj >= blk), 0.0, -jnp.inf)
    lane_head = lax.broadcasted_iota(jnp.int32, (blk, LANES), 1) // SWA_LSE_LANES

    def unit(idx, carry):
        j = idx // d
        r = idx % d
        bias = bias_s[jnp.where((j == 0) & (n == 0), 1, 0)]
        m_tile = jnp.zeros((blk, LANES), F32)
        l_tile = jnp.zeros((blk, LANES), F32)
        for kvh in range(SWA_KV_HEADS):
            q4 = q_ref[0, 0, j, r, :, kvh * SWA_REP * e:(kvh + 1) * SWA_REP * e]
            qs = jnp.concatenate([q4[:, h * e:(h + 1) * e] for h in range(SWA_REP)], axis=0)
            kcols = slice(kvh * e, (kvh + 1) * e)
            kk = jnp.concatenate([k_s[j, r, :, kcols], k_s[j + 1, r, :, kcols]], axis=0)
            vv = jnp.concatenate([v_s[j, r, :, kcols], v_s[j + 1, r, :, kcols]], axis=0)
            s = lax.dot_general(qs, kk, _NT, preferred_element_type=F32) + bias
            m = jnp.max(s, axis=-1, keepdims=True)
            p = jnp.exp(s - m)
            den = jnp.sum(p, axis=-1, keepdims=True)
            o = jnp.dot(p.astype(BF16), vv, preferred_element_type=F32)
            for h in range(SWA_REP):
                head = kvh * SWA_REP + h
                o_ref[0, 0, j, r, :, head * e:(head + 1) * e] = o[h * blk:(h + 1) * blk].astype(BF16)
                m_tile = jnp.where(lane_head == head, m[h * blk:(h + 1) * blk], m_tile)
                l_tile = jnp.where(lane_head == head, den[h * blk:(h + 1) * blk], l_tile)
        m_ref[0, 0, j, r] = m_tile
        l_ref[0, 0, j, r] = l_tile
        return carry

    lax.fori_loop(0, n_j * d, unit, 0)


def _swa_attention(q, k, v):
    bsz, n_span, n_j, d, blk, _ = q.shape
    cur = lambda w: pl.BlockSpec((1, 1, n_j, d, blk, w), lambda b, n: (b, n, 0, 0, 0, 0))
    prev = pl.BlockSpec((1, 1, 1, d, blk, SWA_KV_W), lambda b, n: (b, jnp.maximum(n - 1, 0), n_j - 1, 0, 0, 0))
    stat_shape = q.shape[:-1] + (LANES,)
    return pl.pallas_call(
        _swa_kernel,
        grid=(bsz, n_span),
        in_specs=[cur(SWA_OUT), cur(SWA_KV_W), prev, cur(SWA_KV_W), prev],
        out_specs=[cur(SWA_OUT), cur(LANES), cur(LANES)],
        out_shape=[jax.ShapeDtypeStruct(q.shape, BF16), jax.ShapeDtypeStruct(stat_shape, F32),
                   jax.ShapeDtypeStruct(stat_shape, F32)],
        scratch_shapes=[pltpu.VMEM((n_j + 1, d, blk, SWA_KV_W), BF16),
                        pltpu.VMEM((n_j + 1, d, blk, SWA_KV_W), BF16),
                        pltpu.VMEM((2, SWA_REP * blk, 2 * blk), F32)],
        compiler_params=_params("parallel", "parallel"),
        name=f"swa_attention_d{d}",
    )(q, k, k, v, v)


def _merge_out_kernel(*refs):
    n_g = SWA_N_GROUPS
    o_refs, m_refs, l_refs = refs[0:n_g], refs[n_g:2 * n_g], refs[2 * n_g:3 * n_g]
    g_ref, x_ref, w_ref, lg_ref, lb_ref, out_ref, stat_s, o_s = refs[3 * n_g:]
    tm = ROW_TILE

    def to_tokens(ref, d, dst):
        jb, rows = _tile_unit_rows(d)
        n_slabs = ref.shape[-1] // LANES
        lanes = lambda c: slice(c * LANES, (c + 1) * LANES)
        if d == 1:
            tok = ref[0, 0].reshape(tm, ref.shape[-1]).astype(F32)
            return [tok[:, lanes(c)] for c in range(n_slabs)]
        for jj in range(jb):
            for r in range(d):
                for c in range(n_slabs):
                    dst[c, _token_rows(d, jj, r, rows), :] = ref[0, 0, jj, r, :, lanes(c)].astype(F32)
        return [dst[c] for c in range(n_slabs)]

    dils = [d for _, d in SWA_GROUPS]
    ms = [to_tokens(m_refs[gi], d, stat_s.at[2 * gi:2 * gi + 1])[0] for gi, d in enumerate(dils)]
    ls = [to_tokens(l_refs[gi], d, stat_s.at[2 * gi + 1:2 * gi + 2])[0] for gi, d in enumerate(dils)]
    mx = functools.reduce(jnp.maximum, ms)
    es = [jnp.exp(mg - mx) for mg in ms]
    tot = functools.reduce(jnp.add, [eg * lg for eg, lg in zip(es, ls)])
    src = lax.broadcasted_iota(jnp.int32, (LANES, SWA_OUT), 0)
    dst = lax.broadcasted_iota(jnp.int32, (LANES, SWA_OUT), 1)
    expand = jnp.where(src == (dst // SWA_HEAD_DIM) * SWA_LSE_LANES, 1.0, 0.0).astype(BF16)
    acc = None
    for gi, d in enumerate(dils):
        w = jnp.dot((es[gi] / tot).astype(BF16), expand, preferred_element_type=F32)
        term = w * jnp.concatenate(to_tokens(o_refs[gi], d, o_s), axis=1)
        acc = term if acc is None else acc + term
    a = (acc * _silu(g_ref[...].astype(F32))).astype(BF16)
    y = jnp.dot(a, w_ref[...], preferred_element_type=F32)
    out_ref[...] = _layer_norm(DEEPNORM_ALPHA * x_ref[...] + y, lg_ref[...], lb_ref[...])


def _merge_out_ln(os, ms, ls, g, x2, w_out, ln_g, ln_b, bsz, seq):
    tm = ROW_TILE
    n_span = seq // SWA_SPAN
    row = lambda w: pl.BlockSpec((tm, w), lambda b, n, t: ((b * n_span + n) * TILES_PER_SPAN + t, 0))
    dils = [d for _, d in SWA_GROUPS]
    return pl.pallas_call(
        _merge_out_kernel,
        grid=(bsz, n_span, TILES_PER_SPAN),
        in_specs=[_tile_unit_spec(d, SWA_OUT) for d in dils] + [_tile_unit_spec(d, LANES) for d in dils] * 2
        + [row(SWA_OUT), row(D_MODEL), _full(w_out.shape), _full(ln_g.shape), _full(ln_b.shape)],
        out_specs=row(D_MODEL),
        out_shape=jax.ShapeDtypeStruct((bsz * seq, D_MODEL), F32),
        scratch_shapes=[pltpu.VMEM((2 * SWA_N_GROUPS, tm, LANES), F32),
                        pltpu.VMEM((SWA_OUT // LANES, tm, LANES), F32)],
        compiler_params=_params("parallel", "parallel", "parallel"),
        name="merge_out_ln",
    )(*os, *ms, *ls, g, x2, w_out, ln_g, ln_b)


def _rope_tables(seq):
    half = SWA_HEAD_DIM // 2
    inv = ROPE_THETA ** (-(jnp.arange(half, dtype=F32) * 2.0) / SWA_HEAD_DIM)
    ang = jnp.arange(seq).astype(F32)[:, None] * inv[None, :]
    cos, sin = jnp.cos(ang), jnp.sin(ang)
    return jnp.concatenate([cos, cos], axis=-1), jnp.concatenate([-sin, sin], axis=-1)


def kernel(x, gla_w_in, gla_w_a2, gla_b_a2, gla_norm_g, gla_w_out, w_kv, swa_w_in, swa_w_out, ln_g, ln_b):
    bsz, seq, _ = x.shape
    assert seq % SWA_SPAN == 0 and seq % GLA_BLOCK == 0
    m = bsz * seq
    x2 = x.reshape(m, D_MODEL)

    n_main = 2 * GLA_DK_TOTAL + 2 * GLA_DV_TOTAL
    w_main = gla_w_in[0, :, :n_main].astype(BF16)
    w_alr = jnp.pad(gla_w_in[0, :, n_main:], ((0, 0), (0, LANES - GLA_GATE_RANK))).astype(BF16)
    w_a2 = jnp.pad(gla_w_a2[0], ((0, LANES - GLA_GATE_RANK), (0, 0))).astype(BF16)
    q, k, v, g, la = _gla_in_proj(x2, w_main, w_alr, w_a2, gla_b_a2[0][None, :])
    to3 = lambda a: a.reshape(bsz, seq, a.shape[-1])
    og = _gla_recurrence(to3(q), to3(k), to3(v), to3(g), to3(la), gla_norm_g[0][None, :])
    x2 = _out_ln(og.reshape(m, GLA_DV_TOTAL), x2, gla_w_out[0].astype(BF16), ln_g[0][None, :], ln_b[0][None, :])

    w_all = jnp.concatenate([w_kv, swa_w_in[0]], axis=1).astype(BF16)
    cos, sin_signed = _rope_tables(seq)
    ks, vs, qs, g = _swa_in_proj(x2, w_all, cos, sin_signed, bsz, seq)
    os, ms, ls = zip(*[_swa_attention(qs[gi], ks[gi], vs[gi]) for gi in range(SWA_N_GROUPS)])
    out = _merge_out_ln(os, ms, ls, g, x2, swa_w_out[0].astype(BF16), ln_g[1][None, :], ln_b[1][None, :], bsz, seq)
    return out.reshape(bsz, seq, D_MODEL)
```

```python
import functools

import jax
import jax.numpy as jnp
from jax import lax
from jax.experimental import pallas as pl
from jax.experimental.pallas import tpu as pltpu

BF16 = jnp.bfloat16
F32 = jnp.float32

D_MODEL = 1024
DEPTH = 2
DEEPNORM_ALPHA = (2.0 * DEPTH) ** 0.25
LN_EPS = 1e-5
RMS_EPS = 1e-5

GLA_HEADS = 4
GLA_DK = 128
GLA_DV = 256
GLA_DK_TOTAL = GLA_HEADS * GLA_DK
GLA_DV_TOTAL = GLA_HEADS * GLA_DV
GLA_GATE_RANK = 16
GLA_GATE_TAU = 16.0
GLA_SUB = 64
GLA_CHUNK = 2 * GLA_SUB
GLA_BLOCK = 512

SWA_GROUPS = ((128, 1), (512, 4), (2048, 16))
SWA_N_GROUPS = len(SWA_GROUPS)
SWA_HEAD_DIM = 128
SWA_Q_HEADS = 8
SWA_KV_HEADS = 2
SWA_REP = SWA_Q_HEADS // SWA_KV_HEADS
SWA_OUT = SWA_Q_HEADS * SWA_HEAD_DIM
SWA_KV_W = SWA_KV_HEADS * SWA_HEAD_DIM
SWA_BLOCK = 128
SWA_SPAN = 2048
SWA_LSE_LANES = SWA_HEAD_DIM // SWA_Q_HEADS
ROPE_THETA = 10000.0

LANES = 128
ROW_TILE = 512
TILES_PER_SPAN = SWA_SPAN // ROW_TILE
VMEM_LIMIT = 56 * 1024 * 1024

_NT = (((1,), (1,)), ((), ()))
_TN = (((0,), (0,)), ((), ()))


def _params(*sem):
    return pltpu.CompilerParams(dimension_semantics=sem, vmem_limit_bytes=VMEM_LIMIT)


def _full(shape):
    return pl.BlockSpec(shape, lambda *_: (0,) * len(shape))


def _layer_norm(h, g, b):
    mu = jnp.mean(h, axis=-1, keepdims=True)
    hc = h - mu
    var = jnp.mean(hc * hc, axis=-1, keepdims=True)
    return hc * lax.rsqrt(var + LN_EPS) * g + b


def _silu(g):
    return g * (0.5 + 0.5 * jnp.tanh(0.5 * g))


def _gla_in_kernel(x_ref, w_ref, walr_ref, wa2_ref, ba2_ref, q_ref, k_ref, v_ref, g_ref, la_ref):
    xb = x_ref[...].astype(BF16)

    def proj(lo, hi):
        return jnp.dot(xb, w_ref[:, lo:hi], preferred_element_type=F32)

    q_ref[...] = (proj(0, GLA_DK_TOTAL) * (GLA_DK ** -0.5)).astype(BF16)
    k_ref[...] = proj(GLA_DK_TOTAL, 2 * GLA_DK_TOTAL).astype(BF16)
    v_ref[...] = proj(2 * GLA_DK_TOTAL, 2 * GLA_DK_TOTAL + GLA_DV_TOTAL).astype(BF16)
    g_ref[...] = proj(2 * GLA_DK_TOTAL + GLA_DV_TOTAL, 2 * GLA_DK_TOTAL + 2 * GLA_DV_TOTAL).astype(BF16)
    a_lr = jnp.dot(xb, walr_ref[...], preferred_element_type=F32)
    z = jnp.dot(a_lr.astype(BF16), wa2_ref[...], preferred_element_type=F32) + ba2_ref[...]
    log_sig = jnp.minimum(z, 0.0) - jnp.log1p(jnp.exp(-jnp.abs(z)))
    la_ref[...] = log_sig * (1.0 / GLA_GATE_TAU)


def _gla_in_proj(x2, w_main, w_alr, w_a2, b_a2):
    m = x2.shape[0]
    tm = ROW_TILE
    row = lambda n: pl.BlockSpec((tm, n), lambda i: (i, 0))
    return pl.pallas_call(
        _gla_in_kernel,
        grid=(m // tm,),
        in_specs=[row(D_MODEL), _full(w_main.shape), _full(w_alr.shape), _full(w_a2.shape), _full(b_a2.shape)],
        out_specs=[row(GLA_DK_TOTAL), row(GLA_DK_TOTAL), row(GLA_DV_TOTAL), row(GLA_DV_TOTAL), row(GLA_DK_TOTAL)],
        out_shape=[jax.ShapeDtypeStruct((m, GLA_DK_TOTAL), BF16), jax.ShapeDtypeStruct((m, GLA_DK_TOTAL), BF16),
                   jax.ShapeDtypeStruct((m, GLA_DV_TOTAL), BF16), jax.ShapeDtypeStruct((m, GLA_DV_TOTAL), BF16),
                   jax.ShapeDtypeStruct((m, GLA_DK_TOTAL), F32)],
        compiler_params=_params("parallel"),
        name="gla_in_proj",
    )(x2, w_main, w_alr, w_a2, b_a2)


def _gla_rec_kernel(q_ref, k_ref, v_ref, g_ref, la_ref, ng_ref, o_ref,
                    st_ref, qe_s, qi_s, kd_s, kx_s, klt_s, gam_s, o_s):
    t = la_ref.shape[1]
    c, sub = GLA_CHUNK, GLA_SUB
    n_chunks = t // c
    dk_all = GLA_DK_TOTAL

    @pl.when(pl.program_id(1) == 0)
    def _():
        st_ref[...] = jnp.zeros_like(st_ref)

    la = la_ref[0]
    row = lax.broadcasted_iota(jnp.int32, (t, t), 0)
    col = lax.broadcasted_iota(jnp.int32, (t, t), 1)
    tri = jnp.where((col <= row) & (row // sub == col // sub), 1.0, 0.0).astype(BF16)
    la_hi = la.astype(BF16)
    la_lo = (la - la_hi.astype(F32)).astype(BF16)
    bs = jnp.dot(tri, la_hi, preferred_element_type=F32) + jnp.dot(tri, la_lo, preferred_element_type=F32)

    pair = lambda a: a.reshape(n_chunks, 2, sub, dk_all)
    flat = lambda a: a.reshape(t, dk_all)
    e_tot = jnp.exp(pair(bs)[:, :, sub - 1:sub, :])
    e_a, e_b = e_tot[:, 0:1], e_tot[:, 1:2]
    qi = pair(q_ref[0].astype(F32) * jnp.exp(bs))
    kd = pair(k_ref[0].astype(F32) * jnp.exp(-bs))
    kl = kd * e_tot
    qi_s[...] = flat(qi).astype(BF16)
    kd_s[...] = flat(kd).astype(BF16)
    qe_s[...] = flat(jnp.concatenate([qi[:, 0:1], qi[:, 1:2] * e_a], axis=1)).astype(BF16)
    kx_s[...] = flat(jnp.concatenate([kl[:, 0:1], kd[:, 1:2]], axis=1)).astype(BF16)
    klc = flat(jnp.concatenate([kl[:, 0:1] * e_b, kl[:, 1:2]], axis=1))
    gam = (e_a * e_b).reshape(n_chunks, 1, dk_all)
    for ic in range(n_chunks):
        for h in range(GLA_HEADS):
            kcols = slice(h * GLA_DK, (h + 1) * GLA_DK)
            klt_s[ic, h] = klc[ic * c:(ic + 1) * c, kcols].T.astype(BF16)
            gam_s[ic, h] = jnp.broadcast_to(gam[ic, :, kcols], (c, GLA_DK)).T

    ci = lax.broadcasted_iota(jnp.int32, (c, c), 0)
    cj = lax.broadcasted_iota(jnp.int32, (c, c), 1)
    causal = cj <= ci

    for ic in range(n_chunks):
        rows = slice(ic * c, (ic + 1) * c)
        first = slice(ic * c, ic * c + sub)
        second = slice(ic * c + sub, (ic + 1) * c)
        for h in range(GLA_HEADS):
            kcols = slice(h * GLA_DK, (h + 1) * GLA_DK)
            vcols = slice(h * GLA_DV, (h + 1) * GLA_DV)
            st = st_ref[h]
            vv = v_ref[0, rows, vcols]
            o_inter = jnp.dot(qe_s[rows, kcols], st.astype(BF16), preferred_element_type=F32)
            att = jnp.concatenate(
                [lax.dot_general(qi_s[first, kcols], kd_s[rows, kcols], _NT, preferred_element_type=F32),
                 lax.dot_general(qi_s[second, kcols], kx_s[rows, kcols], _NT, preferred_element_type=F32)], axis=0)
            att = jnp.where(causal, att, 0.0).astype(BF16)
            both = jnp.dot(jnp.concatenate([att, klt_s[ic, h]], axis=0), vv, preferred_element_type=F32)
            o_s[rows, vcols] = o_inter + both[0:c]
            gam_t = gam_s[ic, h]
            st_ref[h] = st * jnp.concatenate([gam_t] * (GLA_DV // GLA_DK), axis=1) + both[c:]

    for h in range(GLA_HEADS):
        vcols = slice(h * GLA_DV, (h + 1) * GLA_DV)
        o = o_s[:, vcols]
        o = o * lax.rsqrt(jnp.mean(o * o, axis=-1, keepdims=True) + RMS_EPS) * ng_ref[:, vcols]
        o_ref[0, :, vcols] = (o * _silu(g_ref[0, :, vcols].astype(F32))).astype(BF16)


def _gla_recurrence(q, k, v, g, la, norm_g):
    bsz, s, _ = q.shape
    t = GLA_BLOCK
    n_chunks = t // GLA_CHUNK
    blk = lambda n: pl.BlockSpec((1, t, n), lambda b, i: (b, i, 0))
    rows_dk = lambda dt: pltpu.VMEM((t, GLA_DK_TOTAL), dt)
    return pl.pallas_call(
        _gla_rec_kernel,
        grid=(bsz, s // t),
        in_specs=[blk(GLA_DK_TOTAL), blk(GLA_DK_TOTAL), blk(GLA_DV_TOTAL), blk(GLA_DV_TOTAL), blk(GLA_DK_TOTAL),
                  _full(norm_g.shape)],
        out_specs=blk(GLA_DV_TOTAL),
        out_shape=jax.ShapeDtypeStruct((bsz, s, GLA_DV_TOTAL), BF16),
        scratch_shapes=[pltpu.VMEM((GLA_HEADS, GLA_DK, GLA_DV), F32),
                        rows_dk(BF16), rows_dk(BF16), rows_dk(BF16), rows_dk(BF16),
                        pltpu.VMEM((n_chunks, GLA_HEADS, GLA_DK, GLA_CHUNK), BF16),
                        pltpu.VMEM((n_chunks, GLA_HEADS, GLA_DK, GLA_CHUNK), F32),
                        pltpu.VMEM((t, GLA_DV_TOTAL), F32)],
        compiler_params=_params("parallel", "arbitrary"),
        name="gla_recurrence",
    )(q, k, v, g, la, norm_g)


def _out_ln_kernel(a_ref, x_ref, w_ref, lg_ref, lb_ref, o_ref):
    y = jnp.dot(a_ref[...], w_ref[...], preferred_element_type=F32)
    o_ref[...] = _layer_norm(DEEPNORM_ALPHA * x_ref[...] + y, lg_ref[...], lb_ref[...])


def _out_ln(a, x2, w_out, ln_g, ln_b):
    m = x2.shape[0]
    tm = ROW_TILE
    row = lambda n: pl.BlockSpec((tm, n), lambda i: (i, 0))
    return pl.pallas_call(
        _out_ln_kernel,
        grid=(m // tm,),
        in_specs=[row(a.shape[1]), row(D_MODEL), _full(w_out.shape), _full(ln_g.shape), _full(ln_b.shape)],
        out_specs=row(D_MODEL),
        out_shape=jax.ShapeDtypeStruct((m, D_MODEL), F32),
        compiler_params=_params("parallel"),
        name="out_ln",
    )(a, x2, w_out, ln_g, ln_b)


def _unit_shape(bsz, seq, d, width):
    return (bsz, seq // SWA_SPAN, SWA_SPAN // (SWA_BLOCK * d), d, SWA_BLOCK, width)


def _tile_unit_spec(d, width):
    rows_per = ROW_TILE // d
    if rows_per >= SWA_BLOCK:
        jb = rows_per // SWA_BLOCK
        return pl.BlockSpec((1, 1, jb, d, SWA_BLOCK, width), lambda b, n, t: (b, n, t, 0, 0, 0))
    per_j = SWA_BLOCK // rows_per
    return pl.BlockSpec((1, 1, 1, d, rows_per, width), lambda b, n, t: (b, n, t // per_j, 0, t % per_j, 0))


def _tile_unit_rows(d):
    rows_per = ROW_TILE // d
    return max(rows_per // SWA_BLOCK, 1), min(rows_per, SWA_BLOCK)


def _token_rows(d, jj, r, rows):
    return pl.ds(jj * SWA_BLOCK * d + r, rows, stride=d) if d > 1 else pl.ds(jj * SWA_BLOCK, rows)


def _rope(h, cos, sin_signed):
    return h * cos + pltpu.roll(h, SWA_HEAD_DIM // 2, 1) * sin_signed


def _swa_in_kernel(x_ref, w_ref, cos_ref, sin_ref, *refs):
    k_refs = refs[0:SWA_N_GROUPS]
    v_refs = refs[SWA_N_GROUPS:2 * SWA_N_GROUPS]
    q_refs = refs[2 * SWA_N_GROUPS:3 * SWA_N_GROUPS]
    g_ref = refs[3 * SWA_N_GROUPS]
    tok_s = refs[3 * SWA_N_GROUPS + 1]
    xb = x_ref[...].astype(BF16)
    cos = cos_ref[...]
    sin = sin_ref[...]
    e = SWA_HEAD_DIM
    kv_half = SWA_N_GROUPS * SWA_KV_W
    q0 = 2 * kv_half

    def proj(lo, n):
        return jnp.dot(xb, w_ref[:, lo:lo + n], preferred_element_type=F32)

    def scatter(val, out_ref, d):
        jb, rows = _tile_unit_rows(d)
        if d == 1:
            out_ref[0, 0] = val.astype(BF16).reshape(out_ref.shape[2:])
            return
        for c in range(val.shape[1] // LANES):
            tok_s[c] = val[:, c * LANES:(c + 1) * LANES]
        for jj in range(jb):
            for r in range(d):
                for c in range(val.shape[1] // LANES):
                    out_ref[0, 0, jj, r, :, c * LANES:(c + 1) * LANES] = (
                        tok_s[c, _token_rows(d, jj, r, rows), :].astype(BF16))

    for gi, (_, d) in enumerate(SWA_GROUPS):
        kk = proj(gi * SWA_KV_W, SWA_KV_W)
        kk = jnp.concatenate([_rope(kk[:, h * e:(h + 1) * e], cos, sin) for h in range(SWA_KV_HEADS)], axis=1)
        scatter(kk, k_refs[gi], d)
        scatter(proj(kv_half + gi * SWA_KV_W, SWA_KV_W), v_refs[gi], d)
        qq = proj(q0 + gi * SWA_OUT, SWA_OUT)
        qq = jnp.concatenate([_rope(qq[:, h * e:(h + 1) * e], cos, sin) * (e ** -0.5)
                              for h in range(SWA_Q_HEADS)], axis=1)
        scatter(qq, q_refs[gi], d)
    g_ref[...] = proj(q0 + SWA_N_GROUPS * SWA_OUT, SWA_OUT).astype(BF16)


def _swa_in_proj(x2, w_all, cos, sin_signed, bsz, seq):
    tm = ROW_TILE
    n_span = seq // SWA_SPAN
    tile = lambda b, n, t: (b * n_span + n) * TILES_PER_SPAN + t
    row = lambda w: pl.BlockSpec((tm, w), lambda b, n, t: (tile(b, n, t), 0))
    pos = pl.BlockSpec((tm, SWA_HEAD_DIM), lambda b, n, t: (n * TILES_PER_SPAN + t, 0))
    dils = [d for _, d in SWA_GROUPS]
    unit_out = [(d, SWA_KV_W) for d in dils] * 2 + [(d, SWA_OUT) for d in dils]
    outs = pl.pallas_call(
        _swa_in_kernel,
        grid=(bsz, n_span, TILES_PER_SPAN),
        in_specs=[row(D_MODEL), _full(w_all.shape), pos, pos],
        out_specs=[_tile_unit_spec(d, w) for d, w in unit_out] + [row(SWA_OUT)],
        out_shape=[jax.ShapeDtypeStruct(_unit_shape(bsz, seq, d, w), BF16) for d, w in unit_out]
        + [jax.ShapeDtypeStruct((bsz * seq, SWA_OUT), BF16)],
        scratch_shapes=[pltpu.VMEM((SWA_OUT // LANES, tm, LANES), F32)],
        compiler_params=_params("parallel", "parallel", "parallel"),
        name="swa_in_proj",
    )(x2, w_all, cos, sin_signed)
    return outs[0:3], outs[3:6], outs[6:9], outs[9]


def _swa_kernel(q_ref, kc_ref, kp_ref, vc_ref, vp_ref, o_ref, m_ref, l_ref, k_s, v_s, bias_s):
    blk = SWA_BLOCK
    e = SWA_HEAD_DIM
    n_j, d = q_ref.shape[2], q_ref.shape[3]
    n = pl.program_id(1)

    k_s[0] = kp_ref[0, 0, 0]
    k_s[1:] = kc_ref[0, 0]
    v_s[0] = vp_ref[0, 0, 0]
    v_s[1:] = vc_ref[0, 0]

    rows = SWA_REP * blk
    qi = lax.broadcasted_iota(jnp.int32, (rows, 2 * blk), 0) % blk
    kj = lax.broadcasted_iota(jnp.int32, (rows, 2 * blk), 1)
    rel = blk + qi - kj
    band = (rel >= 0) & (rel <= blk)
    bias_s[0] = jnp.where(band, 0.0, -jnp.inf)
    bias_s[1] = jnp.where(band & (kj >= blk), 0.0, -jnp.inf)
    lane_head = lax.broadcasted_iota(jnp.int32, (blk, LANES), 1) // SWA_LSE_LANES

    def unit(idx, carry):
        j = idx // d
        r = idx % d
        bias = bias_s[jnp.where((j == 0) & (n == 0), 1, 0)]
        m_tile = jnp.zeros((blk, LANES), F32)
        l_tile = jnp.zeros((blk, LANES), F32)
        for kvh in range(SWA_KV_HEADS):
            q4 = q_ref[0, 0, j, r, :, kvh * SWA_REP * e:(kvh + 1) * SWA_REP * e]
            qs = jnp.concatenate([q4[:, h * e:(h + 1) * e] for h in range(SWA_REP)], axis=0)
            kcols = slice(kvh * e, (kvh + 1) * e)
            kk = jnp.concatenate([k_s[j, r, :, kcols], k_s[j + 1, r, :, kcols]], axis=0)
            vv = jnp.concatenate([v_s[j, r, :, kcols], v_s[j + 1, r, :, kcols]], axis=0)
            s = lax.dot_general(qs, kk, _NT, preferred_element_type=F32) + bias
            m = jnp.max(s, axis=-1, keepdims=True)
            p = jnp.exp(s - m)
            den = jnp.sum(p, axis=-1, keepdims=True)
            o = jnp.dot(p.astype(BF16), vv, preferred_element_type=F32)
            for h in range(SWA_REP):
                head = kvh * SWA_REP + h
                o_ref[0, 0, j, r, :, head * e:(head + 1) * e] = o[h * blk:(h + 1) * blk].astype(BF16)
                m_tile = jnp.where(lane_head == head, m[h * blk:(h + 1) * blk], m_tile)
                l_tile = jnp.where(lane_head == head, den[h * blk:(h + 1) * blk], l_tile)
        m_ref[0, 0, j, r] = m_tile
        l_ref[0, 0, j, r] = l_tile
        return carry

    lax.fori_loop(0, n_j * d, unit, 0, unroll=4)


def _swa_attention(q, k, v):
    bsz, n_span, n_j, d, blk, _ = q.shape
    cur = lambda w: pl.BlockSpec((1, 1, n_j, d, blk, w), lambda b, n: (b, n, 0, 0, 0, 0))
    prev = pl.BlockSpec((1, 1, 1, d, blk, SWA_KV_W), lambda b, n: (b, jnp.maximum(n - 1, 0), n_j - 1, 0, 0, 0))
    stat_shape = q.shape[:-1] + (LANES,)
    return pl.pallas_call(
        _swa_kernel,
        grid=(bsz, n_span),
        in_specs=[cur(SWA_OUT), cur(SWA_KV_W), prev, cur(SWA_KV_W), prev],
        out_specs=[cur(SWA_OUT), cur(LANES), cur(LANES)],
        out_shape=[jax.ShapeDtypeStruct(q.shape, BF16), jax.ShapeDtypeStruct(stat_shape, F32),
                   jax.ShapeDtypeStruct(stat_shape, F32)],
        scratch_shapes=[pltpu.VMEM((n_j + 1, d, blk, SWA_KV_W), BF16),
                        pltpu.VMEM((n_j + 1, d, blk, SWA_KV_W), BF16),
                        pltpu.VMEM((2, SWA_REP * blk, 2 * blk), F32)],
        compiler_params=_params("parallel", "parallel"),
        name=f"swa_attention_d{d}",
    )(q, k, k, v, v)


def _merge_out_kernel(*refs):
    n_g = SWA_N_GROUPS
    o_refs, m_refs, l_refs = refs[0:n_g], refs[n_g:2 * n_g], refs[2 * n_g:3 * n_g]
    g_ref, x_ref, w_ref, lg_ref, lb_ref, out_ref, stat_s, o_s = refs[3 * n_g:]
    tm = ROW_TILE

    def to_tokens(ref, d, dst):
        jb, rows = _tile_unit_rows(d)
        n_slabs = ref.shape[-1] // LANES
        lanes = lambda c: slice(c * LANES, (c + 1) * LANES)
        if d == 1:
            tok = ref[0, 0].reshape(tm, ref.shape[-1]).astype(F32)
            return [tok[:, lanes(c)] for c in range(n_slabs)]
        for jj in range(jb):
            for r in range(d):
                for c in range(n_slabs):
                    dst[c, _token_rows(d, jj, r, rows), :] = ref[0, 0, jj, r, :, lanes(c)].astype(F32)
        return [dst[c] for c in range(n_slabs)]

    dils = [d for _, d in SWA_GROUPS]
    ms = [to_tokens(m_refs[gi], d, stat_s.at[2 * gi:2 * gi + 1])[0] for gi, d in enumerate(dils)]
    ls = [to_tokens(l_refs[gi], d, stat_s.at[2 * gi + 1:2 * gi + 2])[0] for gi, d in enumerate(dils)]
    mx = functools.reduce(jnp.maximum, ms)
    es = [jnp.exp(mg - mx) for mg in ms]
    tot = functools.reduce(jnp.add, [eg * lg for eg, lg in zip(es, ls)])
    src = lax.broadcasted_iota(jnp.int32, (LANES, SWA_OUT), 0)
    dst = lax.broadcasted_iota(jnp.int32, (LANES, SWA_OUT), 1)
    expand = jnp.where(src == (dst // SWA_HEAD_DIM) * SWA_LSE_LANES, 1.0, 0.0).astype(BF16)
    acc = None
    for gi, d in enumerate(dils):
        w = jnp.dot((es[gi] / tot).astype(BF16), expand, preferred_element_type=F32)
        term = w * jnp.concatenate(to_tokens(o_refs[gi], d, o_s), axis=1)
        acc = term if acc is None else acc + term
    a = (acc * _silu(g_ref[...].astype(F32))).astype(BF16)
    y = jnp.dot(a, w_ref[...], preferred_element_type=F32)
    out_ref[...] = _layer_norm(DEEPNORM_ALPHA * x_ref[...] + y, lg_ref[...], lb_ref[...])


def _merge_out_ln(os, ms, ls, g, x2, w_out, ln_g, ln_b, bsz, seq):
    tm = ROW_TILE
    n_span = seq // SWA_SPAN
    row = lambda w: pl.BlockSpec((tm, w), lambda b, n, t: ((b * n_span + n) * TILES_PER_SPAN + t, 0))
    dils = [d for _, d in SWA_GROUPS]
    return pl.pallas_call(
        _merge_out_kernel,
        grid=(bsz, n_span, TILES_PER_SPAN),
        in_specs=[_tile_unit_spec(d, SWA_OUT) for d in dils] + [_tile_unit_spec(d, LANES) for d in dils] * 2
        + [row(SWA_OUT), row(D_MODEL), _full(w_out.shape), _full(ln_g.shape), _full(ln_b.shape)],
        out_specs=row(D_MODEL),
        out_shape=jax.ShapeDtypeStruct((bsz * seq, D_MODEL), F32),
        scratch_shapes=[pltpu.VMEM((2 * SWA_N_GROUPS, tm, LANES), F32),
                        pltpu.VMEM((SWA_OUT // LANES, tm, LANES), F32)],
        compiler_params=_params("parallel", "parallel", "parallel"),
        name="merge_out_ln",
    )(*os, *ms, *ls, g, x2, w_out, ln_g, ln_b)


def _rope_tables(seq):
    half = SWA_HEAD_DIM // 2
    inv = ROPE_THETA ** (-(jnp.arange(half, dtype=F32) * 2.0) / SWA_HEAD_DIM)
    ang = jnp.arange(seq).astype(F32)[:, None] * inv[None, :]
    cos, sin = jnp.cos(ang), jnp.sin(ang)
    return jnp.concatenate([cos, cos], axis=-1), jnp.concatenate([-sin, sin], axis=-1)


def kernel(x, gla_w_in, gla_w_a2, gla_b_a2, gla_norm_g, gla_w_out, w_kv, swa_w_in, swa_w_out, ln_g, ln_b):
    bsz, seq, _ = x.shape
    assert seq % SWA_SPAN == 0 and seq % GLA_BLOCK == 0
    m = bsz * seq
    x2 = x.reshape(m, D_MODEL)

    n_main = 2 * GLA_DK_TOTAL + 2 * GLA_DV_TOTAL
    w_main = gla_w_in[0, :, :n_main].astype(BF16)
    w_alr = jnp.pad(gla_w_in[0, :, n_main:], ((0, 0), (0, LANES - GLA_GATE_RANK))).astype(BF16)
    w_a2 = jnp.pad(gla_w_a2[0], ((0, LANES - GLA_GATE_RANK), (0, 0))).astype(BF16)
    q, k, v, g, la = _gla_in_proj(x2, w_main, w_alr, w_a2, gla_b_a2[0][None, :])
    to3 = lambda a: a.reshape(bsz, seq, a.shape[-1])
    og = _gla_recurrence(to3(q), to3(k), to3(v), to3(g), to3(la), gla_norm_g[0][None, :])
    x2 = _out_ln(og.reshape(m, GLA_DV_TOTAL), x2, gla_w_out[0].astype(BF16), ln_g[0][None, :], ln_b[0][None, :])

    w_all = jnp.concatenate([w_kv, swa_w_in[0]], axis=1).astype(BF16)
    cos, sin_signed = _rope_tables(seq)
    ks, vs, qs, g = _swa_in_proj(x2, w_all, cos, sin_signed, bsz, seq)
    os, ms, ls = zip(*[_swa_attention(qs[gi], ks[gi], vs[gi]) for gi in range(SWA_N_GROUPS)])
    out = _merge_out_ln(os, ms, ls, g, x2, swa_w_out[0].astype(BF16), ln_g[1][None, :], ln_b[1][None, :], bsz, seq)
    return out.reshape(bsz, seq, D_MODEL)
```

```python
import functools

import jax
import jax.numpy as jnp
from jax import lax
from jax.experimental import pallas as pl
from jax.experimental.pallas import tpu as pltpu

BF16 = jnp.bfloat16
F32 = jnp.float32

D_MODEL = 1024
DEPTH = 2
DEEPNORM_ALPHA = (2.0 * DEPTH) ** 0.25
LN_EPS = 1e-5
RMS_EPS = 1e-5

GLA_HEADS = 4
GLA_DK = 128
GLA_DV = 256
GLA_DK_TOTAL = GLA_HEADS * GLA_DK
GLA_DV_TOTAL = GLA_HEADS * GLA_DV
GLA_GATE_RANK = 16
GLA_GATE_TAU = 16.0
GLA_SUB = 64
GLA_CHUNK = 2 * GLA_SUB
GLA_BLOCK = 512
GLA_HEAD_GROUP = 2

SWA_GROUPS = ((128, 1), (512, 4), (2048, 16))
SWA_N_GROUPS = len(SWA_GROUPS)
SWA_HEAD_DIM = 128
SWA_Q_HEADS = 8
SWA_KV_HEADS = 2
SWA_REP = SWA_Q_HEADS // SWA_KV_HEADS
SWA_OUT = SWA_Q_HEADS * SWA_HEAD_DIM
SWA_KV_W = SWA_KV_HEADS * SWA_HEAD_DIM
SWA_BLOCK = 128
SWA_SPAN = 2048
SWA_LSE_LANES = SWA_HEAD_DIM // SWA_Q_HEADS
ROPE_THETA = 10000.0

LANES = 128
MXU_COLS = 256
ROW_TILE = 512
TILES_PER_SPAN = SWA_SPAN // ROW_TILE
VMEM_LIMIT = 56 * 1024 * 1024

_NT = (((1,), (1,)), ((), ()))
_TN = (((0,), (0,)), ((), ()))


def _params(*sem, flags=None):
    return pltpu.CompilerParams(dimension_semantics=sem, vmem_limit_bytes=VMEM_LIMIT, flags=flags)


def _full(shape):
    return pl.BlockSpec(shape, lambda *_: (0,) * len(shape))


def _layer_norm(h, g, b):
    mu = jnp.mean(h, axis=-1, keepdims=True)
    hc = h - mu
    var = jnp.mean(hc * hc, axis=-1, keepdims=True)
    return hc * lax.rsqrt(var + LN_EPS) * g + b


def _silu(g):
    return g * (0.5 + 0.5 * jnp.tanh(0.5 * g))


def _gla_project_pieces(x_ref, w_ref, walr_ref, wa2_ref, ba2_ref, xb_s, q_ref, k_ref, v_ref, g_ref, la_ref):
    xb_s[...] = x_ref[0].astype(BF16)

    def tile(dst, col, lo, scale):
        def piece():
            y = jnp.dot(xb_s[...], w_ref[:, col + lo:col + lo + MXU_COLS], preferred_element_type=F32)
            dst[:, lo:lo + MXU_COLS] = (y if scale is None else y * scale).astype(BF16)
        return piece

    def gate_piece():
        a_lr = jnp.dot(xb_s[...], walr_ref[...], preferred_element_type=F32)
        z = jnp.dot(a_lr.astype(BF16), wa2_ref[...], preferred_element_type=F32) + ba2_ref[...]
        log_sig = jnp.minimum(z, 0.0) - jnp.log1p(jnp.exp(-jnp.abs(z)))
        la_ref[...] = log_sig * (1.0 / GLA_GATE_TAU)

    pieces, col = [gate_piece], 0
    for dst, scale in ((q_ref, GLA_DK ** -0.5), (k_ref, None), (v_ref, None), (g_ref, None)):
        pieces += [tile(dst, col, lo, scale) for lo in range(0, dst.shape[1], MXU_COLS)]
        col += dst.shape[1]
    return pieces


def _gla_recur_pieces(q_ref, k_ref, v_ref, g_ref, la_ref, ng_ref, o_ref,
                      st_ref, qe_s, qi_s, kd_s, kx_s, klt_s, gam_s):
    t = la_ref.shape[0]
    c, sub = GLA_CHUNK, GLA_SUB

    row = lax.broadcasted_iota(jnp.int32, (c, c), 0)
    col = lax.broadcasted_iota(jnp.int32, (c, c), 1)
    causal = col <= row
    tri = jnp.where(causal & (row // sub == col // sub), 1.0, 0.0).astype(BF16)
    halves = lambda first, second: jnp.concatenate([first, second], axis=0)

    for ic in range(t // c):
        rows = slice(ic * c, (ic + 1) * c)
        first = slice(ic * c, ic * c + sub)
        second = slice(ic * c + sub, (ic + 1) * c)

        la = la_ref[rows, :]
        la_hi = la.astype(BF16)
        la_lo = (la - la_hi.astype(F32)).astype(BF16)
        bs = jnp.dot(tri, la_hi, preferred_element_type=F32) + jnp.dot(tri, la_lo, preferred_element_type=F32)
        e_a = jnp.exp(bs[sub - 1:sub, :])
        e_b = jnp.exp(bs[c - 1:c, :])
        qi = q_ref[rows, :].astype(F32) * jnp.exp(bs)
        kd = k_ref[rows, :].astype(F32) * jnp.exp(-bs)
        kl_a, kl_b = kd[:sub] * e_a, kd[sub:] * e_b
        qi_s[rows, :] = qi.astype(BF16)
        kd_s[rows, :] = kd.astype(BF16)
        qe_s[rows, :] = halves(qi[:sub], qi[sub:] * e_a).astype(BF16)
        kx_s[rows, :] = halves(kl_a, kd[sub:]).astype(BF16)
        klc = halves(kl_a * e_b, kl_b)
        gam = e_a * e_b
        for h in range(GLA_HEADS):
            kcols = slice(h * GLA_DK, (h + 1) * GLA_DK)
            klt_s[ic, h] = klc[:, kcols].T.astype(BF16)
            gam_s[ic, h] = jnp.broadcast_to(gam[:, kcols], (c, GLA_DK)).T
        yield

        for h0 in range(0, GLA_HEADS, GLA_HEAD_GROUP):
            stage1 = []
            for h in range(h0, h0 + GLA_HEAD_GROUP):
                kcols = slice(h * GLA_DK, (h + 1) * GLA_DK)
                st = st_ref[h]
                o_inter = jnp.dot(qe_s[rows, kcols], st.astype(BF16), preferred_element_type=F32)
                att = jnp.concatenate(
                    [lax.dot_general(qi_s[first, kcols], kd_s[rows, kcols], _NT, preferred_element_type=F32),
                     lax.dot_general(qi_s[second, kcols], kx_s[rows, kcols], _NT, preferred_element_type=F32)],
                    axis=0)
                stage1.append((st, o_inter, att))
            yield
            for h, (st, o_inter, att) in zip(range(h0, h0 + GLA_HEAD_GROUP), stage1):
                vcols = slice(h * GLA_DV, (h + 1) * GLA_DV)
                att = jnp.where(causal, att, 0.0).astype(BF16)
                both = jnp.dot(jnp.concatenate([att, klt_s[ic, h]], axis=0), v_ref[rows, vcols],
                               preferred_element_type=F32)
                gam_t = gam_s[ic, h]
                st_ref[h] = st * jnp.concatenate([gam_t] * (GLA_DV // GLA_DK), axis=1) + both[c:]
                o = o_inter + both[0:c]
                o = o * lax.rsqrt(jnp.mean(o * o, axis=-1, keepdims=True) + RMS_EPS) * ng_ref[:, vcols]
                o_ref[rows, vcols] = (o * _silu(g_ref[rows, vcols].astype(F32))).astype(BF16)
            yield


N_PROJ = 5


def _gla_layer_kernel(xp_ref, xr_ref, w_ref, walr_ref, wa2_ref, ba2_ref, ng_ref, wout_ref, lg_ref, lb_ref,
                      out_ref, st_ref, *scratch):
    bufs = (scratch[0:N_PROJ], scratch[N_PROJ:2 * N_PROJ])
    xb_s, rec_s, og_s = scratch[2 * N_PROJ], scratch[2 * N_PROJ + 1:-1], scratch[-1]
    j = pl.program_id(1)

    @pl.when(j == 0)
    def _():
        st_ref[...] = jnp.zeros_like(st_ref)
        for ref in bufs[1]:
            ref[...] = jnp.zeros_like(ref)

    def step(write, read):
        pieces = _gla_project_pieces(xp_ref, w_ref, walr_ref, wa2_ref, ba2_ref, xb_s, *write)
        n_pieces = len(pieces)
        n_stages = (xb_s.shape[0] // GLA_CHUNK) * (1 + 2 * (GLA_HEADS // GLA_HEAD_GROUP))
        for i, _ in enumerate(_gla_recur_pieces(*read, ng_ref, og_s, st_ref, *rec_s)):
            issued_after = ((i + 1) * n_pieces + n_stages - 1) // n_stages
            while n_pieces - len(pieces) < issued_after:
                pieces.pop(0)()
        y = jnp.dot(og_s[...], wout_ref[...], preferred_element_type=F32)
        out_ref[0] = _layer_norm(DEEPNORM_ALPHA * xr_ref[0] + y, lg_ref[...], lb_ref[...])

    @pl.when(j % 2 == 0)
    def _():
        step(bufs[0], bufs[1])

    @pl.when(j % 2 == 1)
    def _():
        step(bufs[1], bufs[0])


def _gla_layer(x, w_main, w_alr, w_a2, b_a2, norm_g, w_out, ln_g, ln_b):
    bsz, s, _ = x.shape
    t = GLA_BLOCK
    n_blk = s // t
    n_chunks = t // GLA_CHUNK
    rows = lambda n, dt: pltpu.VMEM((t, n), dt)
    proj_bufs = [rows(GLA_DK_TOTAL, BF16), rows(GLA_DK_TOTAL, BF16), rows(GLA_DV_TOTAL, BF16),
                 rows(GLA_DV_TOTAL, BF16), rows(GLA_DK_TOTAL, F32)]
    assert len(proj_bufs) == N_PROJ
    cur = pl.BlockSpec((1, t, D_MODEL), lambda b, j: (b, jnp.minimum(j, n_blk - 1), 0))
    prev = pl.BlockSpec((1, t, D_MODEL), lambda b, j: (b, jnp.maximum(j - 1, 0), 0))
    consts = (w_main, w_alr, w_a2, b_a2, norm_g, w_out, ln_g, ln_b)
    return pl.pallas_call(
        _gla_layer_kernel,
        grid=(bsz, n_blk + 1),
        in_specs=[cur, prev] + [_full(a.shape) for a in consts],
        out_specs=prev,
        out_shape=jax.ShapeDtypeStruct((bsz, s, D_MODEL), F32),
        scratch_shapes=[pltpu.VMEM((GLA_HEADS, GLA_DK, GLA_DV), F32)] + proj_bufs + proj_bufs
        + [rows(D_MODEL, BF16)] + [rows(GLA_DK_TOTAL, BF16)] * 4
        + [pltpu.VMEM((n_chunks, GLA_HEADS, GLA_DK, GLA_CHUNK), BF16),
           pltpu.VMEM((n_chunks, GLA_HEADS, GLA_DK, GLA_CHUNK), F32),
           rows(GLA_DV_TOTAL, BF16)],
        compiler_params=_params("arbitrary", "arbitrary", ),
        name="gla_layer",
    )(x, x, *consts)


def _unit_shape(bsz, seq, d, width):
    return (bsz, seq // SWA_SPAN, SWA_SPAN // (SWA_BLOCK * d), d, SWA_BLOCK, width)


def _tile_unit_spec(d, width):
    rows_per = ROW_TILE // d
    if rows_per >= SWA_BLOCK:
        jb = rows_per // SWA_BLOCK
        return pl.BlockSpec((1, 1, jb, d, SWA_BLOCK, width), lambda b, n, t: (b, n, t, 0, 0, 0))
    per_j = SWA_BLOCK // rows_per
    return pl.BlockSpec((1, 1, 1, d, rows_per, width), lambda b, n, t: (b, n, t // per_j, 0, t % per_j, 0))


def _tile_unit_rows(d):
    rows_per = ROW_TILE // d
    return max(rows_per // SWA_BLOCK, 1), min(rows_per, SWA_BLOCK)


def _token_rows(d, jj, r, rows):
    return pl.ds(jj * SWA_BLOCK * d + r, rows, stride=d) if d > 1 else pl.ds(jj * SWA_BLOCK, rows)


def _rope(h, cos, sin_signed):
    return h * cos + pltpu.roll(h, SWA_HEAD_DIM // 2, 1) * sin_signed


def _swa_in_kernel(x_ref, w_ref, cos_ref, sin_ref, *refs):
    k_refs = refs[0:SWA_N_GROUPS]
    v_refs = refs[SWA_N_GROUPS:2 * SWA_N_GROUPS]
    q_refs = refs[2 * SWA_N_GROUPS:3 * SWA_N_GROUPS]
    g_ref = refs[3 * SWA_N_GROUPS]
    tok_s = refs[3 * SWA_N_GROUPS + 1]
    xb = x_ref[...].astype(BF16)
    cos = cos_ref[...]
    sin = sin_ref[...]
    e = SWA_HEAD_DIM
    kv_half = SWA_N_GROUPS * SWA_KV_W
    q0 = 2 * kv_half

    def proj(lo, n):
        return jnp.dot(xb, w_ref[:, lo:lo + n], preferred_element_type=F32)

    def scatter(val, out_ref, d):
        jb, rows = _tile_unit_rows(d)
        if d == 1:
            out_ref[0, 0] = val.astype(BF16).reshape(out_ref.shape[2:])
            return
        for c in range(val.shape[1] // LANES):
            tok_s[c] = val[:, c * LANES:(c + 1) * LANES]
        for jj in range(jb):
            for r in range(d):
                for c in range(val.shape[1] // LANES):
                    out_ref[0, 0, jj, r, :, c * LANES:(c + 1) * LANES] = (
                        tok_s[c, _token_rows(d, jj, r, rows), :].astype(BF16))

    for gi, (_, d) in enumerate(SWA_GROUPS):
        kk = proj(gi * SWA_KV_W, SWA_KV_W)
        kk = jnp.concatenate([_rope(kk[:, h * e:(h + 1) * e], cos, sin) for h in range(SWA_KV_HEADS)], axis=1)
        scatter(kk, k_refs[gi], d)
        scatter(proj(kv_half + gi * SWA_KV_W, SWA_KV_W), v_refs[gi], d)
        qq = proj(q0 + gi * SWA_OUT, SWA_OUT)
        qq = jnp.concatenate([_rope(qq[:, h * e:(h + 1) * e], cos, sin) * (e ** -0.5)
                              for h in range(SWA_Q_HEADS)], axis=1)
        scatter(qq, q_refs[gi], d)
    g_ref[...] = proj(q0 + SWA_N_GROUPS * SWA_OUT, SWA_OUT).astype(BF16)


def _swa_in_proj(x2, w_all, cos, sin_signed, bsz, seq):
    tm = ROW_TILE
    n_span = seq // SWA_SPAN
    tile = lambda b, n, t: (b * n_span + n) * TILES_PER_SPAN + t
    row = lambda w: pl.BlockSpec((tm, w), lambda b, n, t: (tile(b, n, t), 0))
    pos = pl.BlockSpec((tm, SWA_HEAD_DIM), lambda b, n, t: (n * TILES_PER_SPAN + t, 0))
    dils = [d for _, d in SWA_GROUPS]
    unit_out = [(d, SWA_KV_W) for d in dils] * 2 + [(d, SWA_OUT) for d in dils]
    outs = pl.pallas_call(
        _swa_in_kernel,
        grid=(bsz, n_span, TILES_PER_SPAN),
        in_specs=[row(D_MODEL), _full(w_all.shape), pos, pos],
        out_specs=[_tile_unit_spec(d, w) for d, w in unit_out] + [row(SWA_OUT)],
        out_shape=[jax.ShapeDtypeStruct(_unit_shape(bsz, seq, d, w), BF16) for d, w in unit_out]
        + [jax.ShapeDtypeStruct((bsz * seq, SWA_OUT), BF16)],
        scratch_shapes=[pltpu.VMEM((SWA_OUT // LANES, tm, LANES), F32)],
        compiler_params=_params("parallel", "parallel", "parallel"),
        name="swa_in_proj",
    )(x2, w_all, cos, sin_signed)
    return outs[0:3], outs[3:6], outs[6:9], outs[9]


def _swa_kernel(q_ref, kc_ref, kp_ref, vc_ref, vp_ref, o_ref, m_ref, l_ref, k_s, v_s, bias_s):
    blk = SWA_BLOCK
    e = SWA_HEAD_DIM
    n_j, d = q_ref.shape[2], q_ref.shape[3]
    n = pl.program_id(1)

    k_s[0] = kp_ref[0, 0, 0]
    k_s[1:] = kc_ref[0, 0]
    v_s[0] = vp_ref[0, 0, 0]
    v_s[1:] = vc_ref[0, 0]

    rows = SWA_REP * blk
    qi = lax.broadcasted_iota(jnp.int32, (rows, 2 * blk), 0) % blk
    kj = lax.broadcasted_iota(jnp.int32, (rows, 2 * blk), 1)
    rel = blk + qi - kj
    band = (rel >= 0) & (rel <= blk)
    bias_s[0] = jnp.where(band, 0.0, -jnp.inf)
    bias_s[1] = jnp.where(band & (kj >= blk), 0.0, -jnp.inf)
    lane_head = lax.broadcasted_iota(jnp.int32, (blk, LANES), 1) // SWA_LSE_LANES

    def unit(idx, carry):
        j = idx // d
        r = idx % d
        bias = bias_s[jnp.where((j == 0) & (n == 0), 1, 0)]
        m_tile = jnp.zeros((blk, LANES), F32)
        l_tile = jnp.zeros((blk, LANES), F32)
        for kvh in range(SWA_KV_HEADS):
            q4 = q_ref[0, 0, j, r, :, kvh * SWA_REP * e:(kvh + 1) * SWA_REP * e]
            qs = jnp.concatenate([q4[:, h * e:(h + 1) * e] for h in range(SWA_REP)], axis=0)
            kcols = slice(kvh * e, (kvh + 1) * e)
            kk = jnp.concatenate([k_s[j, r, :, kcols], k_s[j + 1, r, :, kcols]], axis=0)
            vv = jnp.concatenate([v_s[j, r, :, kcols], v_s[j + 1, r, :, kcols]], axis=0)
            s = lax.dot_general(qs, kk, _NT, preferred_element_type=F32) + bias
            m = jnp.max(s, axis=-1, keepdims=True)
            p = jnp.exp(s - m)
            den = jnp.sum(p, axis=-1, keepdims=True)
            o = jnp.dot(p.astype(BF16), vv, preferred_element_type=F32)
            for h in range(SWA_REP):
                head = kvh * SWA_REP + h
                o_ref[0, 0, j, r, :, head * e:(head + 1) * e] = o[h * blk:(h + 1) * blk].astype(BF16)
                m_tile = jnp.where(lane_head == head, m[h * blk:(h + 1) * blk], m_tile)
                l_tile = jnp.where(lane_head == head, den[h * blk:(h + 1) * blk], l_tile)
        m_ref[0, 0, j, r] = m_tile
        l_ref[0, 0, j, r] = l_tile
        return carry

    lax.fori_loop(0, n_j * d, unit, 0, unroll=4)


def _swa_attention(q, k, v):
    bsz, n_span, n_j, d, blk, _ = q.shape
    cur = lambda w: pl.BlockSpec((1, 1, n_j, d, blk, w), lambda b, n: (b, n, 0, 0, 0, 0))
    prev = pl.BlockSpec((1, 1, 1, d, blk, SWA_KV_W), lambda b, n: (b, jnp.maximum(n - 1, 0), n_j - 1, 0, 0, 0))
    stat_shape = q.shape[:-1] + (LANES,)
    return pl.pallas_call(
        _swa_kernel,
        grid=(bsz, n_span),
        in_specs=[cur(SWA_OUT), cur(SWA_KV_W), prev, cur(SWA_KV_W), prev],
        out_specs=[cur(SWA_OUT), cur(LANES), cur(LANES)],
        out_shape=[jax.ShapeDtypeStruct(q.shape, BF16), jax.ShapeDtypeStruct(stat_shape, F32),
                   jax.ShapeDtypeStruct(stat_shape, F32)],
        scratch_shapes=[pltpu.VMEM((n_j + 1, d, blk, SWA_KV_W), BF16),
                        pltpu.VMEM((n_j + 1, d, blk, SWA_KV_W), BF16),
                        pltpu.VMEM((2, SWA_REP * blk, 2 * blk), F32)],
        compiler_params=_params("parallel", "parallel"),
        name=f"swa_attention_d{d}",
    )(q, k, k, v, v)


def _merge_out_kernel(*refs):
    n_g = SWA_N_GROUPS
    o_refs, m_refs, l_refs = refs[0:n_g], refs[n_g:2 * n_g], refs[2 * n_g:3 * n_g]
    g_ref, x_ref, w_ref, lg_ref, lb_ref, out_ref, stat_s, o_s = refs[3 * n_g:]
    tm = ROW_TILE

    def to_tokens(ref, d, dst):
        jb, rows = _tile_unit_rows(d)
        n_slabs = ref.shape[-1] // LANES
        lanes = lambda c: slice(c * LANES, (c + 1) * LANES)
        if d == 1:
            tok = ref[0, 0].reshape(tm, ref.shape[-1]).astype(F32)
            return [tok[:, lanes(c)] for c in range(n_slabs)]
        for jj in range(jb):
            for r in range(d):
                for c in range(n_slabs):
                    dst[c, _token_rows(d, jj, r, rows), :] = ref[0, 0, jj, r, :, lanes(c)].astype(F32)
        return [dst[c] for c in range(n_slabs)]

    dils = [d for _, d in SWA_GROUPS]
    ms = [to_tokens(m_refs[gi], d, stat_s.at[2 * gi:2 * gi + 1])[0] for gi, d in enumerate(dils)]
    ls = [to_tokens(l_refs[gi], d, stat_s.at[2 * gi + 1:2 * gi + 2])[0] for gi, d in enumerate(dils)]
    mx = functools.reduce(jnp.maximum, ms)
    es = [jnp.exp(mg - mx) for mg in ms]
    tot = functools.reduce(jnp.add, [eg * lg for eg, lg in zip(es, ls)])
    src = lax.broadcasted_iota(jnp.int32, (LANES, SWA_OUT), 0)
    dst = lax.broadcasted_iota(jnp.int32, (LANES, SWA_OUT), 1)
    expand = jnp.where(src == (dst // SWA_HEAD_DIM) * SWA_LSE_LANES, 1.0, 0.0).astype(BF16)
    acc = None
    for gi, d in enumerate(dils):
        w = jnp.dot((es[gi] / tot).astype(BF16), expand, preferred_element_type=F32)
        term = w * jnp.concatenate(to_tokens(o_refs[gi], d, o_s), axis=1)
        acc = term if acc is None else acc + term
    a = (acc * _silu(g_ref[...].astype(F32))).astype(BF16)
    y = jnp.dot(a, w_ref[...], preferred_element_type=F32)
    out_ref[...] = _layer_norm(DEEPNORM_ALPHA * x_ref[...] + y, lg_ref[...], lb_ref[...])


def _merge_out_ln(os, ms, ls, g, x2, w_out, ln_g, ln_b, bsz, seq):
    tm = ROW_TILE
    n_span = seq // SWA_SPAN
    row = lambda w: pl.BlockSpec((tm, w), lambda b, n, t: ((b * n_span + n) * TILES_PER_SPAN + t, 0))
    dils = [d for _, d in SWA_GROUPS]
    return pl.pallas_call(
        _merge_out_kernel,
        grid=(bsz, n_span, TILES_PER_SPAN),
        in_specs=[_tile_unit_spec(d, SWA_OUT) for d in dils] + [_tile_unit_spec(d, LANES) for d in dils] * 2
        + [row(SWA_OUT), row(D_MODEL), _full(w_out.shape), _full(ln_g.shape), _full(ln_b.shape)],
        out_specs=row(D_MODEL),
        out_shape=jax.ShapeDtypeStruct((bsz * seq, D_MODEL), F32),
        scratch_shapes=[pltpu.VMEM((2 * SWA_N_GROUPS, tm, LANES), F32),
                        pltpu.VMEM((SWA_OUT // LANES, tm, LANES), F32)],
        compiler_params=_params("parallel", "parallel", "parallel"),
        name="merge_out_ln",
    )(*os, *ms, *ls, g, x2, w_out, ln_g, ln_b)


def _rope_tables(seq):
    half = SWA_HEAD_DIM // 2
    inv = ROPE_THETA ** (-(jnp.arange(half, dtype=F32) * 2.0) / SWA_HEAD_DIM)
    ang = jnp.arange(seq).astype(F32)[:, None] * inv[None, :]
    cos, sin = jnp.cos(ang), jnp.sin(ang)
    return jnp.concatenate([cos, cos], axis=-1), jnp.concatenate([-sin, sin], axis=-1)


def kernel(x, gla_w_in, gla_w_a2, gla_b_a2, gla_norm_g, gla_w_out, w_kv, swa_w_in, swa_w_out, ln_g, ln_b):
    bsz, seq, _ = x.shape
    assert seq % SWA_SPAN == 0 and seq % GLA_BLOCK == 0
    m = bsz * seq
    x2 = x.reshape(m, D_MODEL)

    n_main = 2 * GLA_DK_TOTAL + 2 * GLA_DV_TOTAL
    w_main = gla_w_in[0, :, :n_main].astype(BF16)
    w_alr = jnp.pad(gla_w_in[0, :, n_main:], ((0, 0), (0, LANES - GLA_GATE_RANK))).astype(BF16)
    w_a2 = jnp.pad(gla_w_a2[0], ((0, LANES - GLA_GATE_RANK), (0, 0))).astype(BF16)
    x2 = _gla_layer(x, w_main, w_alr, w_a2, gla_b_a2[0][None, :], gla_norm_g[0][None, :],
                    gla_w_out[0].astype(BF16), ln_g[0][None, :], ln_b[0][None, :]).reshape(m, D_MODEL)

    w_all = jnp.concatenate([w_kv, swa_w_in[0]], axis=1).astype(BF16)
    cos, sin_signed = _rope_tables(seq)
    ks, vs, qs, g = _swa_in_proj(x2, w_all, cos, sin_signed, bsz, seq)
    os, ms, ls = zip(*[_swa_attention(qs[gi], ks[gi], vs[gi]) for gi in range(SWA_N_GROUPS)])
    out = _merge_out_ln(os, ms, ls, g, x2, swa_w_out[0].astype(BF16), ln_g[1][None, :], ln_b[1][None, :], bsz, seq)
    return out.reshape(bsz, seq, D_MODEL)
```

```python
import functools

import jax
import jax.numpy as jnp
from jax import lax
from jax.experimental import pallas as pl
from jax.experimental.pallas import tpu as pltpu

BF16 = jnp.bfloat16
F32 = jnp.float32

D_MODEL = 1024
DEPTH = 2
DEEPNORM_ALPHA = (2.0 * DEPTH) ** 0.25
LN_EPS = 1e-5
RMS_EPS = 1e-5

GLA_HEADS = 4
GLA_DK = 128
GLA_DV = 256
GLA_DK_TOTAL = GLA_HEADS * GLA_DK
GLA_DV_TOTAL = GLA_HEADS * GLA_DV
GLA_GATE_RANK = 16
GLA_GATE_TAU = 16.0
GLA_SUB = 64
GLA_CHUNK = 2 * GLA_SUB
GLA_BLOCK = 512
GLA_HEAD_GROUP = 2
GLA_FAST_MIN_LOG_DECAY = -40.0

SWA_GROUPS = ((128, 1), (512, 4), (2048, 16))
SWA_N_GROUPS = len(SWA_GROUPS)
SWA_HEAD_DIM = 128
SWA_Q_HEADS = 8
SWA_KV_HEADS = 2
SWA_REP = SWA_Q_HEADS // SWA_KV_HEADS
SWA_OUT = SWA_Q_HEADS * SWA_HEAD_DIM
SWA_KV_W = SWA_KV_HEADS * SWA_HEAD_DIM
SWA_KV_TOTAL = 2 * SWA_N_GROUPS * SWA_KV_W
SWA_BLOCK = 128
SWA_SPAN = 2048
SWA_STAT_LANES = SWA_HEAD_DIM // (2 * SWA_Q_HEADS)
ROPE_THETA = 10000.0

LANES = 128
BF16_ROWS = 16
F32_ROWS = 8
MXU_COLS = 256
ROW_TILE = 1024
TILES_PER_SPAN = SWA_SPAN // ROW_TILE
VMEM_LIMIT = 56 * 1024 * 1024

_NT = (((1,), (1,)), ((), ()))
_TN = (((0,), (0,)), ((), ()))


def _params(*sem, flags=None):
    return pltpu.CompilerParams(dimension_semantics=sem, vmem_limit_bytes=VMEM_LIMIT, flags=flags)


def _full(shape):
    return pl.BlockSpec(shape, lambda *_: (0,) * len(shape), pipeline_mode=pl.Buffered(1))


def _layer_norm(h, g, b):
    mu = jnp.mean(h, axis=-1, keepdims=True)
    hc = h - mu
    var = jnp.mean(hc * hc, axis=-1, keepdims=True)
    return hc * lax.rsqrt(var + LN_EPS) * g + b


def _silu(g):
    return g * (0.5 + 0.5 * jnp.tanh(0.5 * g))


def _gla_project_pieces(x_ref, w_ref, walr_ref, wa2_ref, ba2_ref, xb_s, q_ref, k_ref, v_ref, g_ref, la_ref):
    xb_s[...] = x_ref[0].astype(BF16)

    def tile(dst, col, lo, scale):
        def piece():
            y = jnp.dot(xb_s[...], w_ref[:, col + lo:col + lo + MXU_COLS], preferred_element_type=F32)
            dst[:, lo:lo + MXU_COLS] = (y if scale is None else y * scale).astype(BF16)
        return piece

    def gate_piece():
        a_lr = jnp.dot(xb_s[...], walr_ref[...], preferred_element_type=F32)
        z = jnp.dot(a_lr.astype(BF16), wa2_ref[...], preferred_element_type=F32) + ba2_ref[...]
        log_sig = jnp.minimum(z, 0.0) - jnp.log(1.0 + jnp.exp(-jnp.abs(z)))
        la_ref[...] = log_sig * (1.0 / GLA_GATE_TAU)

    pieces, col = [gate_piece], 0
    for dst, scale in ((q_ref, GLA_DK ** -0.5), (k_ref, None), (v_ref, None), (g_ref, None)):
        pieces += [tile(dst, col, lo, scale) for lo in range(0, dst.shape[1], MXU_COLS)]
        col += dst.shape[1]
    return pieces


def _pairwise_scores(b_s, qf_s, kf_s, row0, kcols):
    c = GLA_CHUNK
    bj = b_s[row0:row0 + c, kcols]
    kj = kf_s[row0:row0 + c, kcols]
    j_idx = lax.broadcasted_iota(jnp.int32, (c, 1), 0)
    lane = lax.broadcasted_iota(jnp.int32, (c, c), 1)

    def query_rows(g, att_t):
        base = pl.multiple_of(row0 + g * F32_ROWS, F32_ROWS)
        b_g = b_s[pl.ds(base, F32_ROWS), kcols]
        q_g = qf_s[pl.ds(base, F32_ROWS), kcols]
        for r in range(F32_ROWS):
            i = g * F32_ROWS + r
            w = jnp.exp(jnp.minimum(b_g[r:r + 1] - bj, 0.0)) * kj * q_g[r:r + 1]
            col = jnp.sum(w, axis=-1, keepdims=True)
            att_t = jnp.where(lane == i, jnp.where(j_idx <= i, col, 0.0), att_t)
        return att_t

    return lax.fori_loop(0, c // F32_ROWS, query_rows, jnp.zeros((c, c), F32)).T


def _gla_recur_pieces(q_ref, k_ref, v_ref, g_ref, la_ref, ng_ref, o_ref,
                      st_ref, qe_s, qi_s, kd_s, kx_s, klt_s, gam_s, pair_s=None):
    t = la_ref.shape[0]
    c, sub = GLA_CHUNK, GLA_SUB

    row = lax.broadcasted_iota(jnp.int32, (c, c), 0)
    col = lax.broadcasted_iota(jnp.int32, (c, c), 1)
    causal = col <= row
    tri = jnp.where(causal & (row // sub == col // sub), 1.0, 0.0).astype(BF16)
    halves = lambda first, second: jnp.concatenate([first, second], axis=0)

    for ic in range(t // c):
        rows = slice(ic * c, (ic + 1) * c)
        first = slice(ic * c, ic * c + sub)
        second = slice(ic * c + sub, (ic + 1) * c)

        la = la_ref[rows, :]
        la_hi = la.astype(BF16)
        la_lo = (la - la_hi.astype(F32)).astype(BF16)
        bs = jnp.dot(tri, la_hi, preferred_element_type=F32) + jnp.dot(tri, la_lo, preferred_element_type=F32)
        if pair_s is None:
            e_a = jnp.exp(bs[sub - 1:sub, :])
            e_b = jnp.exp(bs[c - 1:c, :])
            qi = q_ref[rows, :].astype(F32) * jnp.exp(bs)
            kd = k_ref[rows, :].astype(F32) * jnp.exp(-bs)
            kl_a, kl_b = kd[:sub] * e_a, kd[sub:] * e_b
            qi_s[rows, :] = qi.astype(BF16)
            kd_s[rows, :] = kd.astype(BF16)
            qe_s[rows, :] = halves(qi[:sub], qi[sub:] * e_a).astype(BF16)
            kx_s[rows, :] = halves(kl_a, kd[sub:]).astype(BF16)
            klc = halves(kl_a * e_b, kl_b)
            gam = e_a * e_b
        else:
            b_s, qf_s, kf_s = pair_s
            b = halves(bs[:sub], bs[sub:] + bs[sub - 1:sub, :])
            qf, kf = q_ref[rows, :].astype(F32), k_ref[rows, :].astype(F32)
            b_s[rows, :], qf_s[rows, :], kf_s[rows, :] = b, qf, kf
            qe_s[rows, :] = (qf * jnp.exp(b)).astype(BF16)
            klc = kf * jnp.exp(b[c - 1:c, :] - b)
            gam = jnp.exp(b[c - 1:c, :])
        for h in range(GLA_HEADS):
            kcols = slice(h * GLA_DK, (h + 1) * GLA_DK)
            klt_s[ic, h] = klc[:, kcols].T.astype(BF16)
            gam_s[ic, h] = jnp.broadcast_to(gam[:, kcols], (c, GLA_DK)).T
        yield

        for h0 in range(0, GLA_HEADS, GLA_HEAD_GROUP):
            stage1 = []
            for h in range(h0, h0 + GLA_HEAD_GROUP):
                kcols = slice(h * GLA_DK, (h + 1) * GLA_DK)
                st = st_ref[h]
                o_inter = jnp.dot(qe_s[rows, kcols], st.astype(BF16), preferred_element_type=F32)
                if pair_s is None:
                    att = jnp.concatenate(
                        [lax.dot_general(qi_s[first, kcols], kd_s[rows, kcols], _NT, preferred_element_type=F32),
                         lax.dot_general(qi_s[second, kcols], kx_s[rows, kcols], _NT,
                                         preferred_element_type=F32)], axis=0)
                else:
                    att = _pairwise_scores(*pair_s, ic * c, kcols)
                stage1.append((st, o_inter, att))
            yield
            for h, (st, o_inter, att) in zip(range(h0, h0 + GLA_HEAD_GROUP), stage1):
                vcols = slice(h * GLA_DV, (h + 1) * GLA_DV)
                att = jnp.where(causal, att, 0.0).astype(BF16)
                both = jnp.dot(jnp.concatenate([att, klt_s[ic, h]], axis=0), v_ref[rows, vcols],
                               preferred_element_type=F32)
                gam_t = gam_s[ic, h]
                st_ref[h] = st * jnp.concatenate([gam_t] * (GLA_DV // GLA_DK), axis=1) + both[c:]
                o = o_inter + both[0:c]
                o = o * lax.rsqrt(jnp.mean(o * o, axis=-1, keepdims=True) + RMS_EPS) * ng_ref[:, vcols]
                o_ref[rows, vcols] = o.astype(BF16) * _silu(g_ref[rows, vcols])
            yield


N_PROJ = 5


def _gla_layer_kernel(xp_ref, xr_ref, w_ref, walr_ref, wa2_ref, ba2_ref, ng_ref, wout_ref, lg_ref, lb_ref,
                      wkv_ref, swin_ref, swout_ref, out_ref, wall_ref, wout1_ref, st_ref, *scratch,
                      blocks_per_seq):
    bufs = (scratch[0:N_PROJ], scratch[N_PROJ:2 * N_PROJ])
    xb_s, rec_s, pair_s, og_s = scratch[2 * N_PROJ], scratch[2 * N_PROJ + 1:-4], scratch[-4:-1], scratch[-1]
    j = pl.program_id(0)
    wall_ref[:, 0:SWA_KV_TOTAL] = wkv_ref[...].astype(BF16)
    wall_ref[:, SWA_KV_TOTAL:] = swin_ref[0].astype(BF16)
    wout1_ref[...] = swout_ref[0].astype(BF16)

    @pl.when(j == 0)
    def _():
        for ref in bufs[1]:
            ref[...] = jnp.zeros_like(ref)

    @pl.when((j == 0) | (j % blocks_per_seq == 1 % blocks_per_seq))
    def _():
        st_ref[...] = jnp.zeros_like(st_ref)

    def step(write, read, pairwise):
        pieces = _gla_project_pieces(xp_ref, w_ref, walr_ref, wa2_ref, ba2_ref, xb_s, *write)
        n_pieces = len(pieces)
        n_stages = (xb_s.shape[0] // GLA_CHUNK) * (1 + 2 * (GLA_HEADS // GLA_HEAD_GROUP))
        for i, _ in enumerate(_gla_recur_pieces(*read, ng_ref, og_s, st_ref, *rec_s,
                                                pair_s=pair_s if pairwise else None)):
            issued_after = ((i + 1) * n_pieces + n_stages - 1) // n_stages
            while n_pieces - len(pieces) < issued_after:
                pieces.pop(0)()
        y = jnp.dot(og_s[...], wout_ref[...], preferred_element_type=F32)
        out_ref[0] = _layer_norm(DEEPNORM_ALPHA * xr_ref[0] + y, lg_ref[0:1, :], lb_ref[0:1, :])

    for parity in range(2):
        write, read = bufs[parity], bufs[1 - parity]

        @pl.when(j % 2 == parity)
        def _():
            totals = jnp.sum(read[N_PROJ - 1][...].reshape(-1, GLA_SUB, GLA_DK_TOTAL), axis=1)
            strong = jnp.min(totals) < GLA_FAST_MIN_LOG_DECAY

            @pl.when(jnp.logical_not(strong))
            def _():
                step(write, read, pairwise=False)

            @pl.when(strong)
            def _():
                step(write, read, pairwise=True)


def _gla_layer(x, w_main, w_alr, w_a2, b_a2, norm_g, w_out, ln_g, ln_b, w_kv, swa_w_in, swa_w_out):
    bsz, s, _ = x.shape
    t = GLA_BLOCK
    n_blk = s // t
    n_chunks = t // GLA_CHUNK
    rows = lambda n, dt: pltpu.VMEM((t, n), dt)
    proj_bufs = [rows(GLA_DK_TOTAL, BF16), rows(GLA_DK_TOTAL, BF16), rows(GLA_DV_TOTAL, BF16),
                 rows(GLA_DV_TOTAL, BF16), rows(GLA_DK_TOTAL, F32)]
    assert len(proj_bufs) == N_PROJ
    n_all = bsz * n_blk
    xb = x.reshape(n_all, t, D_MODEL)
    cur = pl.BlockSpec((1, t, D_MODEL), lambda j: (jnp.minimum(j, n_all - 1), 0, 0))
    prev = pl.BlockSpec((1, t, D_MODEL), lambda j: (jnp.maximum(j - 1, 0), 0, 0))
    consts = (w_main, w_alr, w_a2, b_a2, norm_g, w_out, ln_g, ln_b)
    wr = D_MODEL // n_all
    assert wr * n_all == D_MODEL and wr % BF16_ROWS == 0
    wrows2 = lambda n: pl.BlockSpec((wr, n), lambda j: (jnp.minimum(j, n_all - 1), 0))
    wrows3 = lambda n: pl.BlockSpec((1, wr, n), lambda j: (0, jnp.minimum(j, n_all - 1), 0))
    n_in = w_kv.shape[1] + swa_w_in.shape[2]
    return pl.pallas_call(
        functools.partial(_gla_layer_kernel, blocks_per_seq=n_blk),
        grid=(n_all + 1,),
        in_specs=[cur, prev] + [_full(a.shape) for a in consts]
        + [wrows2(w_kv.shape[1]), wrows3(swa_w_in.shape[2]), wrows3(D_MODEL)],
        out_specs=[prev, wrows2(n_in), wrows2(D_MODEL)],
        out_shape=[jax.ShapeDtypeStruct(xb.shape, F32), jax.ShapeDtypeStruct((D_MODEL, n_in), BF16),
                   jax.ShapeDtypeStruct((D_MODEL, D_MODEL), BF16)],
        scratch_shapes=[pltpu.VMEM((GLA_HEADS, GLA_DK, GLA_DV), F32)] + proj_bufs + proj_bufs
        + [rows(D_MODEL, BF16)] + [rows(GLA_DK_TOTAL, BF16)] * 4
        + [pltpu.VMEM((n_chunks, GLA_HEADS, GLA_DK, GLA_CHUNK), BF16),
           pltpu.VMEM((n_chunks, GLA_HEADS, GLA_DK, GLA_CHUNK), F32)]
        + [rows(GLA_DK_TOTAL, F32)] * 3 + [rows(GLA_DV_TOTAL, BF16)],
        compiler_params=_params("arbitrary"),
        name="gla_layer",
    )(xb, xb, *consts, w_kv, swa_w_in, swa_w_out)


def _unit_shape(bsz, seq, d, width):
    return (bsz, seq // SWA_SPAN, SWA_SPAN // (SWA_BLOCK * d), d, SWA_BLOCK, width)


def _tile_unit_spec(d, width):
    rows_per = ROW_TILE // d
    if rows_per >= SWA_BLOCK:
        jb = rows_per // SWA_BLOCK
        return pl.BlockSpec((1, 1, jb, d, SWA_BLOCK, width), lambda b, n, t: (b, n, t, 0, 0, 0))
    per_j = SWA_BLOCK // rows_per
    return pl.BlockSpec((1, 1, 1, d, rows_per, width), lambda b, n, t: (b, n, t // per_j, 0, t % per_j, 0))


def _tile_unit_rows(d):
    rows_per = ROW_TILE // d
    return max(rows_per // SWA_BLOCK, 1), min(rows_per, SWA_BLOCK)


def _token_rows(d, jj, r, rows):
    return pl.ds(jj * SWA_BLOCK * d + r, rows, stride=d) if d > 1 else pl.ds(jj * SWA_BLOCK, rows)


def _rope(h, cos, sin_signed):
    return h * cos + pltpu.roll(h, SWA_HEAD_DIM // 2, 1) * sin_signed


def _swa_in_kernel(x_ref, w_ref, cos_base_ref, sin_base_ref, cos_off_ref, sin_off_ref, *refs):
    k_refs = refs[0:SWA_N_GROUPS]
    v_refs = refs[SWA_N_GROUPS:2 * SWA_N_GROUPS]
    q_refs = refs[2 * SWA_N_GROUPS:3 * SWA_N_GROUPS]
    g_ref = refs[3 * SWA_N_GROUPS]
    tok_s = refs[3 * SWA_N_GROUPS + 1]
    rope_s = refs[3 * SWA_N_GROUPS + 2]
    xb = x_ref[...].astype(BF16)
    e = SWA_HEAD_DIM
    tile = pl.ds(pl.program_id(1) * TILES_PER_SPAN + pl.program_id(2), 1)
    ca, sa = cos_base_ref[tile, :], sin_base_ref[tile, :]
    cb, sb = cos_off_ref[...], sin_off_ref[...]
    cos = ca * cb - sa * sb
    lane = lax.broadcasted_iota(jnp.int32, cb.shape, 1)
    sin = jnp.where(lane < e // 2, -1.0, 1.0) * (sa * cb + ca * sb)
    rope_s[0], rope_s[1] = cos, sin
    rope_s[2], rope_s[3] = cos * (e ** -0.5), sin * (e ** -0.5)
    kv_half = SWA_N_GROUPS * SWA_KV_W
    q0 = 2 * kv_half

    n_slabs = tok_s.shape[0]
    pipeline = {"slab": 0, "pending": None}

    def head_pair(col, out_ref, head0, d, rope):
        y = jnp.dot(xb, w_ref[:, col:col + MXU_COLS], preferred_element_type=F32)
        jb, rows = _tile_unit_rows(d)
        staged = []
        for i in range(MXU_COLS // e):
            h = head0 + i
            yh = y[:, i * e:(i + 1) * e]
            if rope is not None:
                yh = _rope(yh, rope_s[rope], rope_s[rope + 1])
            if d == 1:
                out_ref[0, 0, :, :, :, h * e:(h + 1) * e] = yh.astype(BF16).reshape(jb, d, rows, e)
                continue
            slab = pipeline["slab"]
            pipeline["slab"] = (slab + 1) % n_slabs
            tok_s[slab] = yh
            staged.append((slab, h))

        def scatter():
            for slab, h in staged:
                for jj in range(jb):
                    for r in range(d):
                        out_ref[0, 0, jj, r, :, h * e:(h + 1) * e] = (
                            tok_s[slab, _token_rows(d, jj, r, rows), :].astype(BF16))

        flush()
        pipeline["pending"] = scatter if staged else None

    def flush():
        if pipeline["pending"] is not None:
            pipeline["pending"]()
            pipeline["pending"] = None

    for gi, (_, d) in enumerate(SWA_GROUPS):
        head_pair(gi * SWA_KV_W, k_refs[gi], 0, d, 0)
        head_pair(kv_half + gi * SWA_KV_W, v_refs[gi], 0, d, None)
        for h0 in range(0, SWA_Q_HEADS, MXU_COLS // e):
            head_pair(q0 + gi * SWA_OUT + h0 * e, q_refs[gi], h0, d, 2)
    g0 = q0 + SWA_N_GROUPS * SWA_OUT
    for c in range(0, SWA_OUT, MXU_COLS):
        g_ref[:, c:c + MXU_COLS] = jnp.dot(xb, w_ref[:, g0 + c:g0 + c + MXU_COLS],
                                           preferred_element_type=F32).astype(BF16)
        flush()


def _swa_in_proj(x2, w_all, rope_tables, bsz, seq):
    tm = ROW_TILE
    n_span = seq // SWA_SPAN
    tile = lambda b, n, t: (b * n_span + n) * TILES_PER_SPAN + t
    row = lambda w: pl.BlockSpec((tm, w), lambda b, n, t: (tile(b, n, t), 0))
    dils = [d for _, d in SWA_GROUPS]
    unit_out = [(d, SWA_KV_W) for d in dils] * 2 + [(d, SWA_OUT) for d in dils]
    outs = pl.pallas_call(
        _swa_in_kernel,
        grid=(bsz, n_span, TILES_PER_SPAN),
        in_specs=[row(D_MODEL), _full(w_all.shape)] + [_full(a.shape) for a in rope_tables],
        out_specs=[_tile_unit_spec(d, w) for d, w in unit_out] + [row(SWA_OUT)],
        out_shape=[jax.ShapeDtypeStruct(_unit_shape(bsz, seq, d, w), BF16) for d, w in unit_out]
        + [jax.ShapeDtypeStruct((bsz * seq, SWA_OUT), BF16)],
        scratch_shapes=[pltpu.VMEM((SWA_OUT // LANES, tm, LANES), F32), pltpu.VMEM((4, tm, SWA_HEAD_DIM), F32)],
        compiler_params=_params("parallel", "parallel", "parallel"),
        name="swa_in_proj",
    )(x2, w_all, *rope_tables)
    return outs[0:3], outs[3:6], outs[6:9], outs[9]


def _swa_kernel(q_ref, kc_ref, kp_ref, vc_ref, vp_ref, o_ref, st_ref, k_s, v_s, bias_s):
    blk = SWA_BLOCK
    e = SWA_HEAD_DIM
    n_j, d = q_ref.shape[2], q_ref.shape[3]
    n = pl.program_id(1)

    k_s[0] = kp_ref[0, 0, 0]
    k_s[1:] = kc_ref[0, 0]
    v_s[0] = vp_ref[0, 0, 0]
    v_s[1:] = vc_ref[0, 0]

    rows = SWA_REP * blk
    qi = lax.broadcasted_iota(jnp.int32, (rows, 2 * blk), 0) % blk
    kj = lax.broadcasted_iota(jnp.int32, (rows, 2 * blk), 1)
    rel = blk + qi - kj
    band = (rel >= 0) & (rel <= blk)
    bias_s[0] = jnp.where(band, 0.0, -jnp.inf)
    bias_s[1] = jnp.where(band & (kj >= blk), 0.0, -jnp.inf)
    lane_slot = lax.broadcasted_iota(jnp.int32, (blk, LANES), 1) // SWA_STAT_LANES

    def unit(idx, carry):
        j = idx // d
        r = idx % d
        bias = bias_s[jnp.where((j == 0) & (n == 0), 1, 0)]
        st_tile = jnp.zeros((blk, LANES), F32)
        for kvh in range(SWA_KV_HEADS):
            q4 = q_ref[0, 0, j, r, :, kvh * SWA_REP * e:(kvh + 1) * SWA_REP * e]
            qs = jnp.concatenate([q4[:, h * e:(h + 1) * e] for h in range(SWA_REP)], axis=0)
            kcols = slice(kvh * e, (kvh + 1) * e)
            kk = jnp.concatenate([k_s[j, r, :, kcols], k_s[j + 1, r, :, kcols]], axis=0)
            vv = jnp.concatenate([v_s[j, r, :, kcols], v_s[j + 1, r, :, kcols]], axis=0)
            s = lax.dot_general(qs, kk, _NT, preferred_element_type=F32) + bias
            m = jnp.max(s, axis=-1, keepdims=True)
            p = jnp.exp(s - m)
            den = jnp.sum(p, axis=-1, keepdims=True)
            o = jnp.dot(p.astype(BF16), vv, preferred_element_type=F32)
            for h in range(SWA_REP):
                head = kvh * SWA_REP + h
                o_ref[0, 0, j, r, :, head * e:(head + 1) * e] = o[h * blk:(h + 1) * blk].astype(BF16)
                st_tile = jnp.where(lane_slot == 2 * head, m[h * blk:(h + 1) * blk], st_tile)
                st_tile = jnp.where(lane_slot == 2 * head + 1, den[h * blk:(h + 1) * blk], st_tile)
        st_ref[0, 0, j, r] = st_tile
        return carry

    lax.fori_loop(0, n_j * d, unit, 0, unroll=4)


def _swa_attention(q, k, v):
    bsz, n_span, n_j, d, blk, _ = q.shape
    cur = lambda w: pl.BlockSpec((1, 1, n_j, d, blk, w), lambda b, n: (b, n, 0, 0, 0, 0))
    prev = pl.BlockSpec((1, 1, 1, d, blk, SWA_KV_W), lambda b, n: (b, jnp.maximum(n - 1, 0), n_j - 1, 0, 0, 0))
    stat_shape = q.shape[:-1] + (LANES,)
    return pl.pallas_call(
        _swa_kernel,
        grid=(bsz, n_span),
        in_specs=[cur(SWA_OUT), cur(SWA_KV_W), prev, cur(SWA_KV_W), prev],
        out_specs=[cur(SWA_OUT), cur(LANES)],
        out_shape=[jax.ShapeDtypeStruct(q.shape, BF16), jax.ShapeDtypeStruct(stat_shape, F32)],
        scratch_shapes=[pltpu.VMEM((n_j + 1, d, blk, SWA_KV_W), BF16),
                        pltpu.VMEM((n_j + 1, d, blk, SWA_KV_W), BF16),
                        pltpu.VMEM((2, SWA_REP * blk, 2 * blk), F32)],
        compiler_params=_params("parallel", "parallel"),
        name=f"swa_attention_d{d}",
    )(q, k, k, v, v)


def _merge_out_kernel(*refs):
    n_g = SWA_N_GROUPS
    o_refs, st_refs = refs[0:n_g], refs[n_g:2 * n_g]
    g_ref, x_ref, w_ref, lg_ref, lb_ref, out_ref, stat_s, o_s = refs[2 * n_g:]
    tm = ROW_TILE

    def to_tokens(ref, d, dst):
        jb, rows = _tile_unit_rows(d)
        n_slabs = ref.shape[-1] // LANES
        lanes = lambda c: slice(c * LANES, (c + 1) * LANES)
        if d == 1:
            tok = ref[0, 0].reshape(tm, ref.shape[-1]).astype(F32)
            return [tok[:, lanes(c)] for c in range(n_slabs)]
        for jj in range(jb):
            for r in range(d):
                for c in range(n_slabs):
                    dst[c, _token_rows(d, jj, r, rows), :] = ref[0, 0, jj, r, :, lanes(c)].astype(F32)
        return [dst[c] for c in range(n_slabs)]

    dils = [d for _, d in SWA_GROUPS]
    sts = [to_tokens(st_refs[gi], d, stat_s.at[gi:gi + 1])[0] for gi, d in enumerate(dils)]
    mx = functools.reduce(jnp.maximum, sts)
    es = [jnp.exp(st - mx) for st in sts]
    ls = [pltpu.roll(st, LANES - SWA_STAT_LANES, 1) for st in sts]
    tot = functools.reduce(jnp.add, [eg * lg for eg, lg in zip(es, ls)])
    m_slot = (lax.broadcasted_iota(jnp.int32, (tm, LANES), 1) // SWA_STAT_LANES) % 2 == 0
    src = lax.broadcasted_iota(jnp.int32, (LANES, SWA_OUT), 0)
    dst = lax.broadcasted_iota(jnp.int32, (LANES, SWA_OUT), 1)
    expand = jnp.where(src == (dst // SWA_HEAD_DIM) * 2 * SWA_STAT_LANES, 1.0, 0.0).astype(BF16)
    acc = None
    for gi, d in enumerate(dils):
        w = jnp.dot(jnp.where(m_slot, es[gi] / tot, 0.0).astype(BF16), expand, preferred_element_type=F32)
        term = w * jnp.concatenate(to_tokens(o_refs[gi], d, o_s), axis=1)
        acc = term if acc is None else acc + term
    a = acc.astype(BF16) * _silu(g_ref[...])
    y = jnp.dot(a, w_ref[...], preferred_element_type=F32)
    out_ref[...] = _layer_norm(DEEPNORM_ALPHA * x_ref[...] + y, lg_ref[1:2, :], lb_ref[1:2, :])


def _merge_out_ln(os, sts, g, x2, w_out, ln_g, ln_b, bsz, seq):
    tm = ROW_TILE
    n_span = seq // SWA_SPAN
    row = lambda w: pl.BlockSpec((tm, w), lambda b, n, t: ((b * n_span + n) * TILES_PER_SPAN + t, 0))
    dils = [d for _, d in SWA_GROUPS]
    return pl.pallas_call(
        _merge_out_kernel,
        grid=(bsz, n_span, TILES_PER_SPAN),
        in_specs=[_tile_unit_spec(d, SWA_OUT) for d in dils] + [_tile_unit_spec(d, LANES) for d in dils]
        + [row(SWA_OUT), row(D_MODEL), _full(w_out.shape), _full(ln_g.shape), _full(ln_b.shape)],
        out_specs=row(D_MODEL),
        out_shape=jax.ShapeDtypeStruct((bsz * seq, D_MODEL), F32),
        scratch_shapes=[pltpu.VMEM((SWA_N_GROUPS, tm, LANES), F32),
                        pltpu.VMEM((SWA_OUT // LANES, tm, LANES), F32)],
        compiler_params=_params("parallel", "parallel", "parallel"),
        name="merge_out_ln",
    )(*os, *sts, g, x2, w_out, ln_g, ln_b)


def _rope_tables(seq):
    half = SWA_HEAD_DIM // 2
    inv = ROPE_THETA ** (-(jnp.arange(half, dtype=F32) * 2.0) / SWA_HEAD_DIM)
    inv = jnp.concatenate([inv, inv])[None, :]
    base = (jnp.arange(seq // ROW_TILE) * ROW_TILE).astype(F32)[:, None] * inv
    off = jnp.arange(ROW_TILE).astype(F32)[:, None] * inv
    return jnp.cos(base), jnp.sin(base), jnp.cos(off), jnp.sin(off)


GLA_N_MAIN = 2 * GLA_DK_TOTAL + 2 * GLA_DV_TOTAL
CAST_ROWS = 128


def _cast_weights_kernel(gw_in_t, gw_a2, gw_out, w_main, w_alr, w_a2, w_out0):
    r = CAST_ROWS
    t = gw_in_t[0]
    for c in range(0, GLA_N_MAIN, r):
        w_main[:, c:c + r] = t[c:c + r, :].T.astype(BF16)
    tail = t[GLA_N_MAIN + GLA_GATE_RANK - r:, :].T
    lane = lax.broadcasted_iota(jnp.int32, (r, r), 1)
    w_alr[...] = jnp.where(lane >= r - GLA_GATE_RANK, tail, 0.0).astype(BF16)
    w_a2[...] = jnp.zeros_like(w_a2)
    w_a2[LANES - GLA_GATE_RANK:, :] = gw_a2[0].astype(BF16)
    w_out0[...] = gw_out[0].astype(BF16)


def _cast_weights(gla_w_in, gla_w_a2, gla_w_out):
    r = CAST_ROWS
    assert r == LANES
    rows3 = lambda n: pl.BlockSpec((1, r, n), lambda i: (0, i, 0))
    rows2 = lambda n: pl.BlockSpec((r, n), lambda i: (i, 0))
    gla_w_in_t = jnp.swapaxes(gla_w_in, 1, 2)
    cols_t = pl.BlockSpec((1, gla_w_in_t.shape[1], r), lambda i: (0, 0, i))
    out_widths = [GLA_N_MAIN, LANES, None, D_MODEL]
    out_shapes = [(D_MODEL, w) if w else (LANES, GLA_DK_TOTAL) for w in out_widths]
    return pl.pallas_call(
        _cast_weights_kernel,
        grid=(D_MODEL // r,),
        in_specs=[cols_t, _full(gla_w_a2.shape), rows3(D_MODEL)],
        out_specs=[rows2(w) if w else pl.BlockSpec((LANES, GLA_DK_TOTAL), lambda i: (0, 0)) for w in out_widths],
        out_shape=[jax.ShapeDtypeStruct(s, BF16) for s in out_shapes],
        compiler_params=_params("arbitrary"),
        name="cast_weights",
    )(gla_w_in_t, gla_w_a2, gla_w_out)


def kernel(x, gla_w_in, gla_w_a2, gla_b_a2, gla_norm_g, gla_w_out, w_kv, swa_w_in, swa_w_out, ln_g, ln_b):
    bsz, seq, _ = x.shape
    assert seq % SWA_SPAN == 0 and seq % GLA_BLOCK == 0
    assert gla_w_in.shape[0] == 1 and swa_w_in.shape[0] == 1 and ln_g.shape[0] == DEPTH
    m = bsz * seq

    w_main, w_alr, w_a2, w_out0 = _cast_weights(gla_w_in, gla_w_a2, gla_w_out)
    x2, w_all, w_out1 = _gla_layer(x, w_main, w_alr, w_a2, gla_b_a2, gla_norm_g, w_out0, ln_g, ln_b,
                                   w_kv, swa_w_in, swa_w_out)
    x2 = x2.reshape(m, D_MODEL)

    ks, vs, qs, g = _swa_in_proj(x2, w_all, _rope_tables(seq), bsz, seq)
    os, sts = zip(*[_swa_attention(qs[gi], ks[gi], vs[gi]) for gi in range(SWA_N_GROUPS)])
    out = _merge_out_ln(os, sts, g, x2, w_out1, ln_g, ln_b, bsz, seq)
    return out.reshape(bsz, seq, D_MODEL)
```

```python
import functools

import jax
import jax.numpy as jnp
from jax import lax
from jax.experimental import pallas as pl
from jax.experimental.pallas import tpu as pltpu

BF16 = jnp.bfloat16
F32 = jnp.float32

D_MODEL = 1024
DEPTH = 2
DEEPNORM_ALPHA = (2.0 * DEPTH) ** 0.25
LN_EPS = 1e-5
RMS_EPS = 1e-5

GLA_HEADS = 4
GLA_DK = 128
GLA_DV = 256
GLA_DK_TOTAL = GLA_HEADS * GLA_DK
GLA_DV_TOTAL = GLA_HEADS * GLA_DV
GLA_GATE_RANK = 16
GLA_GATE_TAU = 16.0
GLA_SUB = 64
GLA_CHUNK = 2 * GLA_SUB
GLA_BLOCK = 512
GLA_HEAD_GROUP = 2
GLA_FAST_MIN_LOG_DECAY = -40.0

SWA_GROUPS = ((128, 1), (512, 4), (2048, 16))
SWA_N_GROUPS = len(SWA_GROUPS)
SWA_HEAD_DIM = 128
SWA_Q_HEADS = 8
SWA_KV_HEADS = 2
SWA_REP = SWA_Q_HEADS // SWA_KV_HEADS
SWA_OUT = SWA_Q_HEADS * SWA_HEAD_DIM
SWA_KV_W = SWA_KV_HEADS * SWA_HEAD_DIM
SWA_KV_TOTAL = 2 * SWA_N_GROUPS * SWA_KV_W
SWA_BLOCK = 128
SWA_SPAN = 2048
SWA_STAT_LANES = SWA_HEAD_DIM // (2 * SWA_Q_HEADS)
ROPE_THETA = 10000.0

LANES = 128
BF16_ROWS = 16
F32_ROWS = 8
MXU_COLS = 256
ROW_TILE = 1024
TILES_PER_SPAN = SWA_SPAN // ROW_TILE
VMEM_LIMIT = 56 * 1024 * 1024

_NT = (((1,), (1,)), ((), ()))
_TN = (((0,), (0,)), ((), ()))


def _params(*sem, flags=None):
    return pltpu.CompilerParams(dimension_semantics=sem, vmem_limit_bytes=VMEM_LIMIT, flags=flags)


def _full(shape):
    return pl.BlockSpec(shape, lambda *_: (0,) * len(shape), pipeline_mode=pl.Buffered(1))


def _layer_norm(h, g, b):
    mu = jnp.mean(h, axis=-1, keepdims=True)
    hc = h - mu
    var = jnp.mean(hc * hc, axis=-1, keepdims=True)
    return hc * lax.rsqrt(var + LN_EPS) * g + b


def _silu(g):
    return g * (0.5 + 0.5 * jnp.tanh(0.5 * g))


def _gla_project_pieces(x_ref, w_ref, walr_ref, wa2_ref, ba2_ref, xb_s, q_ref, k_ref, v_ref, g_ref, la_ref):
    xb_s[...] = x_ref[0].astype(BF16)

    def tile(dst, col, lo, scale):
        def piece():
            y = jnp.dot(xb_s[...], w_ref[:, col + lo:col + lo + MXU_COLS], preferred_element_type=F32)
            dst[:, lo:lo + MXU_COLS] = (y if scale is None else y * scale).astype(BF16)
        return piece

    def gate_piece():
        a_lr = jnp.dot(xb_s[...], walr_ref[...], preferred_element_type=F32)
        z = jnp.dot(a_lr.astype(BF16), wa2_ref[...], preferred_element_type=F32) + ba2_ref[...]
        log_sig = jnp.minimum(z, 0.0) - jnp.log(1.0 + jnp.exp(-jnp.abs(z)))
        la_ref[...] = log_sig * (1.0 / GLA_GATE_TAU)

    pieces, col = [gate_piece], 0
    for dst, scale in ((q_ref, GLA_DK ** -0.5), (k_ref, None), (v_ref, None), (g_ref, None)):
        pieces += [tile(dst, col, lo, scale) for lo in range(0, dst.shape[1], MXU_COLS)]
        col += dst.shape[1]
    return pieces


def _gla_recur_pieces(q_ref, k_ref, v_ref, g_ref, la_ref, ng_ref, o_ref,
                      st_ref, qe_s, qi_s, kd_s, kx_s, klt_s, gam_s):
    t = la_ref.shape[0]
    c, sub = GLA_CHUNK, GLA_SUB

    row = lax.broadcasted_iota(jnp.int32, (c, c), 0)
    col = lax.broadcasted_iota(jnp.int32, (c, c), 1)
    causal = col <= row
    tri = jnp.where(causal & (row // sub == col // sub), 1.0, 0.0).astype(BF16)
    halves = lambda first, second: jnp.concatenate([first, second], axis=0)

    for ic in range(t // c):
        rows = slice(ic * c, (ic + 1) * c)
        first = slice(ic * c, ic * c + sub)
        second = slice(ic * c + sub, (ic + 1) * c)

        la = la_ref[rows, :]
        la_hi = la.astype(BF16)
        la_lo = (la - la_hi.astype(F32)).astype(BF16)
        bs = jnp.dot(tri, la_hi, preferred_element_type=F32) + jnp.dot(tri, la_lo, preferred_element_type=F32)
        e_a = jnp.exp(bs[sub - 1:sub, :])
        e_b = jnp.exp(bs[c - 1:c, :])
        qi = q_ref[rows, :].astype(F32) * jnp.exp(bs)
        kd = k_ref[rows, :].astype(F32) * jnp.exp(-bs)
        kl_a, kl_b = kd[:sub] * e_a, kd[sub:] * e_b
        qi_s[rows, :] = qi.astype(BF16)
        kd_s[rows, :] = kd.astype(BF16)
        qe_s[rows, :] = halves(qi[:sub], qi[sub:] * e_a).astype(BF16)
        kx_s[rows, :] = halves(kl_a, kd[sub:]).astype(BF16)
        klc = halves(kl_a * e_b, kl_b)
        gam = e_a * e_b
        for h in range(GLA_HEADS):
            kcols = slice(h * GLA_DK, (h + 1) * GLA_DK)
            klt_s[ic, h] = klc[:, kcols].T.astype(BF16)
            gam_s[ic, h] = jnp.broadcast_to(gam[:, kcols], (c, GLA_DK)).T
        yield

        for h0 in range(0, GLA_HEADS, GLA_HEAD_GROUP):
            stage1 = []
            for h in range(h0, h0 + GLA_HEAD_GROUP):
                kcols = slice(h * GLA_DK, (h + 1) * GLA_DK)
                st = st_ref[h]
                o_inter = jnp.dot(qe_s[rows, kcols], st.astype(BF16), preferred_element_type=F32)
                att = jnp.concatenate(
                    [lax.dot_general(qi_s[first, kcols], kd_s[rows, kcols], _NT, preferred_element_type=F32),
                     lax.dot_general(qi_s[second, kcols], kx_s[rows, kcols], _NT, preferred_element_type=F32)],
                    axis=0)
                stage1.append((st, o_inter, att))
            yield
            for h, (st, o_inter, att) in zip(range(h0, h0 + GLA_HEAD_GROUP), stage1):
                vcols = slice(h * GLA_DV, (h + 1) * GLA_DV)
                att = jnp.where(causal, att, 0.0).astype(BF16)
                both = jnp.dot(jnp.concatenate([att, klt_s[ic, h]], axis=0), v_ref[rows, vcols],
                               preferred_element_type=F32)
                gam_t = gam_s[ic, h]
                st_ref[h] = st * jnp.concatenate([gam_t] * (GLA_DV // GLA_DK), axis=1) + both[c:]
                o = o_inter + both[0:c]
                o = o * lax.rsqrt(jnp.mean(o * o, axis=-1, keepdims=True) + RMS_EPS) * ng_ref[:, vcols]
                o_ref[rows, vcols] = o.astype(BF16) * _silu(g_ref[rows, vcols])
            yield


def _pairwise_scores(b_s, qf_s, kf_s, kcols):
    c = GLA_CHUNK
    bj = b_s[:, kcols]
    kj = kf_s[:, kcols]
    j_idx = lax.broadcasted_iota(jnp.int32, (c, 1), 0)
    lane = lax.broadcasted_iota(jnp.int32, (c, c), 1)

    def query_rows(g, att_t):
        base = pl.multiple_of(g * F32_ROWS, F32_ROWS)
        b_g = b_s[pl.ds(base, F32_ROWS), kcols]
        q_g = qf_s[pl.ds(base, F32_ROWS), kcols]
        for r in range(F32_ROWS):
            i = g * F32_ROWS + r
            w = jnp.exp(jnp.minimum(b_g[r:r + 1] - bj, 0.0)) * kj * q_g[r:r + 1]
            col = jnp.sum(w, axis=-1, keepdims=True)
            att_t = jnp.where(lane == i, jnp.where(j_idx <= i, col, 0.0), att_t)
        return att_t

    return lax.fori_loop(0, c // F32_ROWS, query_rows, jnp.zeros((c, c), F32)).T


def _gla_recur_pairwise(q_ref, k_ref, v_ref, g_ref, la_ref, ng_ref, o_ref, st_ref, b_s, qf_s, kf_s):
    c, sub = GLA_CHUNK, GLA_SUB
    row = lax.broadcasted_iota(jnp.int32, (c, c), 0)
    col = lax.broadcasted_iota(jnp.int32, (c, c), 1)
    tri = jnp.where((col <= row) & (row // sub == col // sub), 1.0, 0.0).astype(BF16)

    def chunk(ic, carry):
        rows = pl.ds(pl.multiple_of(ic * c, c), c)
        la = la_ref[rows, :]
        la_hi = la.astype(BF16)
        la_lo = (la - la_hi.astype(F32)).astype(BF16)
        bs = jnp.dot(tri, la_hi, preferred_element_type=F32) + jnp.dot(tri, la_lo, preferred_element_type=F32)
        b = jnp.concatenate([bs[:sub], bs[sub:] + bs[sub - 1:sub, :]], axis=0)
        qf, kf = q_ref[rows, :].astype(F32), k_ref[rows, :].astype(F32)
        b_s[...], qf_s[...], kf_s[...] = b, qf, kf
        qe = (qf * jnp.exp(b)).astype(BF16)
        kl = kf * jnp.exp(b[c - 1:c, :] - b)
        gam = jnp.exp(b[c - 1:c, :])
        for h in range(GLA_HEADS):
            kcols = slice(h * GLA_DK, (h + 1) * GLA_DK)
            vcols = slice(h * GLA_DV, (h + 1) * GLA_DV)
            st = st_ref[h]
            att = _pairwise_scores(b_s, qf_s, kf_s, kcols).astype(BF16)
            o_inter = jnp.dot(qe[:, kcols], st.astype(BF16), preferred_element_type=F32)
            both = jnp.dot(jnp.concatenate([att, kl[:, kcols].T.astype(BF16)], axis=0), v_ref[rows, vcols],
                           preferred_element_type=F32)
            gam_t = jnp.broadcast_to(gam[:, kcols], (c, GLA_DK)).T
            st_ref[h] = st * jnp.concatenate([gam_t] * (GLA_DV // GLA_DK), axis=1) + both[c:]
            o = o_inter + both[0:c]
            o = o * lax.rsqrt(jnp.mean(o * o, axis=-1, keepdims=True) + RMS_EPS) * ng_ref[:, vcols]
            o_ref[rows, vcols] = o.astype(BF16) * _silu(g_ref[rows, vcols])
        return carry

    lax.fori_loop(0, la_ref.shape[0] // c, chunk, 0)


N_PROJ = 5


def _gla_layer_kernel(xp_ref, xr_ref, w_ref, walr_ref, wa2_ref, ba2_ref, ng_ref, wout_ref, lg_ref, lb_ref,
                      wkv_ref, swin_ref, swout_ref, out_ref, wall_ref, wout1_ref, st_ref, *scratch,
                      blocks_per_seq):
    bufs = (scratch[0:N_PROJ], scratch[N_PROJ:2 * N_PROJ])
    xb_s, rec_s = scratch[2 * N_PROJ], scratch[2 * N_PROJ + 1:-5]
    pair_s, st0_s, og_s = scratch[-5:-2], scratch[-2], scratch[-1]
    j = pl.program_id(0)
    wall_ref[:, 0:SWA_KV_TOTAL] = wkv_ref[...].astype(BF16)
    wall_ref[:, SWA_KV_TOTAL:] = swin_ref[0].astype(BF16)
    wout1_ref[...] = swout_ref[0].astype(BF16)

    @pl.when(j == 0)
    def _():
        for ref in bufs[1]:
            ref[...] = jnp.zeros_like(ref)

    @pl.when((j == 0) | (j % blocks_per_seq == 1 % blocks_per_seq))
    def _():
        st_ref[...] = jnp.zeros_like(st_ref)

    def step(write, read):
        totals = jnp.sum(read[N_PROJ - 1][...].reshape(-1, GLA_SUB, GLA_DK_TOTAL), axis=1)
        out_of_range = jnp.min(totals) < GLA_FAST_MIN_LOG_DECAY
        st0_s[...] = st_ref[...]

        pieces = _gla_project_pieces(xp_ref, w_ref, walr_ref, wa2_ref, ba2_ref, xb_s, *write)
        n_pieces = len(pieces)
        n_stages = (xb_s.shape[0] // GLA_CHUNK) * (1 + 2 * (GLA_HEADS // GLA_HEAD_GROUP))
        for i, _ in enumerate(_gla_recur_pieces(*read, ng_ref, og_s, st_ref, *rec_s)):
            issued_after = ((i + 1) * n_pieces + n_stages - 1) // n_stages
            while n_pieces - len(pieces) < issued_after:
                pieces.pop(0)()

        @pl.when(out_of_range)
        def _():
            st_ref[...] = st0_s[...]
            _gla_recur_pairwise(*read, ng_ref, og_s, st_ref, *pair_s)

        y = jnp.dot(og_s[...], wout_ref[...], preferred_element_type=F32)
        out_ref[0] = _layer_norm(DEEPNORM_ALPHA * xr_ref[0] + y, lg_ref[0:1, :], lb_ref[0:1, :])

    @pl.when(j % 2 == 0)
    def _():
        step(bufs[0], bufs[1])

    @pl.when(j % 2 == 1)
    def _():
        step(bufs[1], bufs[0])


def _gla_layer(x, w_main, w_alr, w_a2, b_a2, norm_g, w_out, ln_g, ln_b, w_kv, swa_w_in, swa_w_out):
    bsz, s, _ = x.shape
    t = GLA_BLOCK
    n_blk = s // t
    n_chunks = t // GLA_CHUNK
    rows = lambda n, dt: pltpu.VMEM((t, n), dt)
    proj_bufs = [rows(GLA_DK_TOTAL, BF16), rows(GLA_DK_TOTAL, BF16), rows(GLA_DV_TOTAL, BF16),
                 rows(GLA_DV_TOTAL, BF16), rows(GLA_DK_TOTAL, F32)]
    assert len(proj_bufs) == N_PROJ
    n_all = bsz * n_blk
    xb = x.reshape(n_all, t, D_MODEL)
    cur = pl.BlockSpec((1, t, D_MODEL), lambda j: (jnp.minimum(j, n_all - 1), 0, 0))
    prev = pl.BlockSpec((1, t, D_MODEL), lambda j: (jnp.maximum(j - 1, 0), 0, 0))
    consts = (w_main, w_alr, w_a2, b_a2, norm_g, w_out, ln_g, ln_b)
    wr = D_MODEL // n_all
    assert wr * n_all == D_MODEL and wr % BF16_ROWS == 0
    wrows2 = lambda n: pl.BlockSpec((wr, n), lambda j: (jnp.minimum(j, n_all - 1), 0))
    wrows3 = lambda n: pl.BlockSpec((1, wr, n), lambda j: (0, jnp.minimum(j, n_all - 1), 0))
    n_in = w_kv.shape[1] + swa_w_in.shape[2]
    return pl.pallas_call(
        functools.partial(_gla_layer_kernel, blocks_per_seq=n_blk),
        grid=(n_all + 1,),
        in_specs=[cur, prev] + [_full(a.shape) for a in consts]
        + [wrows2(w_kv.shape[1]), wrows3(swa_w_in.shape[2]), wrows3(D_MODEL)],
        out_specs=[prev, wrows2(n_in), wrows2(D_MODEL)],
        out_shape=[jax.ShapeDtypeStruct(xb.shape, F32), jax.ShapeDtypeStruct((D_MODEL, n_in), BF16),
                   jax.ShapeDtypeStruct((D_MODEL, D_MODEL), BF16)],
        scratch_shapes=[pltpu.VMEM((GLA_HEADS, GLA_DK, GLA_DV), F32)] + proj_bufs + proj_bufs
        + [rows(D_MODEL, BF16)] + [rows(GLA_DK_TOTAL, BF16)] * 4
        + [pltpu.VMEM((n_chunks, GLA_HEADS, GLA_DK, GLA_CHUNK), BF16),
           pltpu.VMEM((n_chunks, GLA_HEADS, GLA_DK, GLA_CHUNK), F32)]
        + [pltpu.VMEM((GLA_CHUNK, GLA_DK_TOTAL), F32)] * 3 + [pltpu.VMEM((GLA_HEADS, GLA_DK, GLA_DV), F32)]
        + [rows(GLA_DV_TOTAL, BF16)],
        compiler_params=_params("arbitrary"),
        name="gla_layer",
    )(xb, xb, *consts, w_kv, swa_w_in, swa_w_out)


def _unit_shape(bsz, seq, d, width):
    return (bsz, seq // SWA_SPAN, SWA_SPAN // (SWA_BLOCK * d), d, SWA_BLOCK, width)


def _tile_unit_spec(d, width):
    rows_per = ROW_TILE // d
    if rows_per >= SWA_BLOCK:
        jb = rows_per // SWA_BLOCK
        return pl.BlockSpec((1, 1, jb, d, SWA_BLOCK, width), lambda b, n, t: (b, n, t, 0, 0, 0))
    per_j = SWA_BLOCK // rows_per
    return pl.BlockSpec((1, 1, 1, d, rows_per, width), lambda b, n, t: (b, n, t // per_j, 0, t % per_j, 0))


def _tile_unit_rows(d):
    rows_per = ROW_TILE // d
    return max(rows_per // SWA_BLOCK, 1), min(rows_per, SWA_BLOCK)


def _token_rows(d, jj, r, rows):
    return pl.ds(jj * SWA_BLOCK * d + r, rows, stride=d) if d > 1 else pl.ds(jj * SWA_BLOCK, rows)


def _rope(h, cos, sin_signed):
    return h * cos + pltpu.roll(h, SWA_HEAD_DIM // 2, 1) * sin_signed


def _swa_in_kernel(x_ref, w_ref, cos_base_ref, sin_base_ref, cos_off_ref, sin_off_ref, *refs):
    k_refs = refs[0:SWA_N_GROUPS]
    v_refs = refs[SWA_N_GROUPS:2 * SWA_N_GROUPS]
    q_refs = refs[2 * SWA_N_GROUPS:3 * SWA_N_GROUPS]
    g_ref = refs[3 * SWA_N_GROUPS]
    tok_s = refs[3 * SWA_N_GROUPS + 1]
    rope_s = refs[3 * SWA_N_GROUPS + 2]
    xb = x_ref[...].astype(BF16)
    e = SWA_HEAD_DIM
    tile = pl.ds(pl.program_id(1) * TILES_PER_SPAN + pl.program_id(2), 1)
    ca, sa = cos_base_ref[tile, :], sin_base_ref[tile, :]
    cb, sb = cos_off_ref[...], sin_off_ref[...]
    cos = ca * cb - sa * sb
    lane = lax.broadcasted_iota(jnp.int32, cb.shape, 1)
    sin = jnp.where(lane < e // 2, -1.0, 1.0) * (sa * cb + ca * sb)
    rope_s[0], rope_s[1] = cos, sin
    rope_s[2], rope_s[3] = cos * (e ** -0.5), sin * (e ** -0.5)
    kv_half = SWA_N_GROUPS * SWA_KV_W
    q0 = 2 * kv_half

    n_slabs = tok_s.shape[0]
    pipeline = {"slab": 0, "pending": None}

    def head_pair(col, out_ref, head0, d, rope):
        y = jnp.dot(xb, w_ref[:, col:col + MXU_COLS], preferred_element_type=F32)
        jb, rows = _tile_unit_rows(d)
        staged = []
        for i in range(MXU_COLS // e):
            h = head0 + i
            yh = y[:, i * e:(i + 1) * e]
            if rope is not None:
                yh = _rope(yh, rope_s[rope], rope_s[rope + 1])
            if d == 1:
                out_ref[0, 0, :, :, :, h * e:(h + 1) * e] = yh.astype(BF16).reshape(jb, d, rows, e)
                continue
            slab = pipeline["slab"]
            pipeline["slab"] = (slab + 1) % n_slabs
            tok_s[slab] = yh
            staged.append((slab, h))

        def scatter():
            for slab, h in staged:
                for jj in range(jb):
                    for r in range(d):
                        out_ref[0, 0, jj, r, :, h * e:(h + 1) * e] = (
                            tok_s[slab, _token_rows(d, jj, r, rows), :].astype(BF16))

        flush()
        pipeline["pending"] = scatter if staged else None

    def flush():
        if pipeline["pending"] is not None:
            pipeline["pending"]()
            pipeline["pending"] = None

    for gi, (_, d) in enumerate(SWA_GROUPS):
        head_pair(gi * SWA_KV_W, k_refs[gi], 0, d, 0)
        head_pair(kv_half + gi * SWA_KV_W, v_refs[gi], 0, d, None)
        for h0 in range(0, SWA_Q_HEADS, MXU_COLS // e):
            head_pair(q0 + gi * SWA_OUT + h0 * e, q_refs[gi], h0, d, 2)
    g0 = q0 + SWA_N_GROUPS * SWA_OUT
    for c in range(0, SWA_OUT, MXU_COLS):
        g_ref[:, c:c + MXU_COLS] = jnp.dot(xb, w_ref[:, g0 + c:g0 + c + MXU_COLS],
                                           preferred_element_type=F32).astype(BF16)
        flush()


def _swa_in_proj(x2, w_all, rope_tables, bsz, seq):
    tm = ROW_TILE
    n_span = seq // SWA_SPAN
    tile = lambda b, n, t: (b * n_span + n) * TILES_PER_SPAN + t
    row = lambda w: pl.BlockSpec((tm, w), lambda b, n, t: (tile(b, n, t), 0))
    dils = [d for _, d in SWA_GROUPS]
    unit_out = [(d, SWA_KV_W) for d in dils] * 2 + [(d, SWA_OUT) for d in dils]
    outs = pl.pallas_call(
        _swa_in_kernel,
        grid=(bsz, n_span, TILES_PER_SPAN),
        in_specs=[row(D_MODEL), _full(w_all.shape)] + [_full(a.shape) for a in rope_tables],
        out_specs=[_tile_unit_spec(d, w) for d, w in unit_out] + [row(SWA_OUT)],
        out_shape=[jax.ShapeDtypeStruct(_unit_shape(bsz, seq, d, w), BF16) for d, w in unit_out]
        + [jax.ShapeDtypeStruct((bsz * seq, SWA_OUT), BF16)],
        scratch_shapes=[pltpu.VMEM((SWA_OUT // LANES, tm, LANES), F32), pltpu.VMEM((4, tm, SWA_HEAD_DIM), F32)],
        compiler_params=_params("parallel", "parallel", "parallel"),
        name="swa_in_proj",
    )(x2, w_all, *rope_tables)
    return outs[0:3], outs[3:6], outs[6:9], outs[9]


def _swa_kernel(q_ref, kc_ref, kp_ref, vc_ref, vp_ref, o_ref, st_ref, k_s, v_s, bias_s):
    blk = SWA_BLOCK
    e = SWA_HEAD_DIM
    n_j, d = q_ref.shape[2], q_ref.shape[3]
    n = pl.program_id(1)

    k_s[0] = kp_ref[0, 0, 0]
    k_s[1:] = kc_ref[0, 0]
    v_s[0] = vp_ref[0, 0, 0]
    v_s[1:] = vc_ref[0, 0]

    rows = SWA_REP * blk
    qi = lax.broadcasted_iota(jnp.int32, (rows, 2 * blk), 0) % blk
    kj = lax.broadcasted_iota(jnp.int32, (rows, 2 * blk), 1)
    rel = blk + qi - kj
    band = (rel >= 0) & (rel <= blk)
    bias_s[0] = jnp.where(band, 0.0, -jnp.inf)
    bias_s[1] = jnp.where(band & (kj >= blk), 0.0, -jnp.inf)
    lane_slot = lax.broadcasted_iota(jnp.int32, (blk, LANES), 1) // SWA_STAT_LANES

    def unit(idx, carry):
        j = idx // d
        r = idx % d
        bias = bias_s[jnp.where((j == 0) & (n == 0), 1, 0)]
        st_tile = jnp.zeros((blk, LANES), F32)
        for kvh in range(SWA_KV_HEADS):
            q4 = q_ref[0, 0, j, r, :, kvh * SWA_REP * e:(kvh + 1) * SWA_REP * e]
            qs = jnp.concatenate([q4[:, h * e:(h + 1) * e] for h in range(SWA_REP)], axis=0)
            kcols = slice(kvh * e, (kvh + 1) * e)
            kk = jnp.concatenate([k_s[j, r, :, kcols], k_s[j + 1, r, :, kcols]], axis=0)
            vv = jnp.concatenate([v_s[j, r, :, kcols], v_s[j + 1, r, :, kcols]], axis=0)
            s = lax.dot_general(qs, kk, _NT, preferred_element_type=F32) + bias
            m = jnp.max(s, axis=-1, keepdims=True)
            p = jnp.exp(s - m)
            den = jnp.sum(p, axis=-1, keepdims=True)
            o = jnp.dot(p.astype(BF16), vv, preferred_element_type=F32)
            for h in range(SWA_REP):
                head = kvh * SWA_REP + h
                o_ref[0, 0, j, r, :, head * e:(head + 1) * e] = o[h * blk:(h + 1) * blk].astype(BF16)
                st_tile = jnp.where(lane_slot == 2 * head, m[h * blk:(h + 1) * blk], st_tile)
                st_tile = jnp.where(lane_slot == 2 * head + 1, den[h * blk:(h + 1) * blk], st_tile)
        st_ref[0, 0, j, r] = st_tile
        return carry

    lax.fori_loop(0, n_j * d, unit, 0, unroll=4)


def _swa_attention(q, k, v):
    bsz, n_span, n_j, d, blk, _ = q.shape
    cur = lambda w: pl.BlockSpec((1, 1, n_j, d, blk, w), lambda b, n: (b, n, 0, 0, 0, 0))
    prev = pl.BlockSpec((1, 1, 1, d, blk, SWA_KV_W), lambda b, n: (b, jnp.maximum(n - 1, 0), n_j - 1, 0, 0, 0))
    stat_shape = q.shape[:-1] + (LANES,)
    return pl.pallas_call(
        _swa_kernel,
        grid=(bsz, n_span),
        in_specs=[cur(SWA_OUT), cur(SWA_KV_W), prev, cur(SWA_KV_W), prev],
        out_specs=[cur(SWA_OUT), cur(LANES)],
        out_shape=[jax.ShapeDtypeStruct(q.shape, BF16), jax.ShapeDtypeStruct(stat_shape, F32)],
        scratch_shapes=[pltpu.VMEM((n_j + 1, d, blk, SWA_KV_W), BF16),
                        pltpu.VMEM((n_j + 1, d, blk, SWA_KV_W), BF16),
                        pltpu.VMEM((2, SWA_REP * blk, 2 * blk), F32)],
        compiler_params=_params("parallel", "parallel"),
        name=f"swa_attention_d{d}",
    )(q, k, k, v, v)


def _merge_out_kernel(*refs):
    n_g = SWA_N_GROUPS
    o_refs, st_refs = refs[0:n_g], refs[n_g:2 * n_g]
    g_ref, x_ref, w_ref, lg_ref, lb_ref, out_ref, stat_s, o_s = refs[2 * n_g:]
    tm = ROW_TILE

    def to_tokens(ref, d, dst):
        jb, rows = _tile_unit_rows(d)
        n_slabs = ref.shape[-1] // LANES
        lanes = lambda c: slice(c * LANES, (c + 1) * LANES)
        if d == 1:
            tok = ref[0, 0].reshape(tm, ref.shape[-1]).astype(F32)
            return [tok[:, lanes(c)] for c in range(n_slabs)]
        for jj in range(jb):
            for r in range(d):
                for c in range(n_slabs):
                    dst[c, _token_rows(d, jj, r, rows), :] = ref[0, 0, jj, r, :, lanes(c)].astype(F32)
        return [dst[c] for c in range(n_slabs)]

    dils = [d for _, d in SWA_GROUPS]
    sts = [to_tokens(st_refs[gi], d, stat_s.at[gi:gi + 1])[0] for gi, d in enumerate(dils)]
    mx = functools.reduce(jnp.maximum, sts)
    es = [jnp.exp(st - mx) for st in sts]
    ls = [pltpu.roll(st, LANES - SWA_STAT_LANES, 1) for st in sts]
    tot = functools.reduce(jnp.add, [eg * lg for eg, lg in zip(es, ls)])
    m_slot = (lax.broadcasted_iota(jnp.int32, (tm, LANES), 1) // SWA_STAT_LANES) % 2 == 0
    src = lax.broadcasted_iota(jnp.int32, (LANES, SWA_OUT), 0)
    dst = lax.broadcasted_iota(jnp.int32, (LANES, SWA_OUT), 1)
    expand = jnp.where(src == (dst // SWA_HEAD_DIM) * 2 * SWA_STAT_LANES, 1.0, 0.0).astype(BF16)
    acc = None
    for gi, d in enumerate(dils):
        w = jnp.dot(jnp.where(m_slot, es[gi] / tot, 0.0).astype(BF16), expand, preferred_element_type=F32)
        term = w * jnp.concatenate(to_tokens(o_refs[gi], d, o_s), axis=1)
        acc = term if acc is None else acc + term
    a = acc.astype(BF16) * _silu(g_ref[...])
    y = jnp.dot(a, w_ref[...], preferred_element_type=F32)
    out_ref[...] = _layer_norm(DEEPNORM_ALPHA * x_ref[...] + y, lg_ref[1:2, :], lb_ref[1:2, :])


def _merge_out_ln(os, sts, g, x2, w_out, ln_g, ln_b, bsz, seq):
    tm = ROW_TILE
    n_span = seq // SWA_SPAN
    row = lambda w: pl.BlockSpec((tm, w), lambda b, n, t: ((b * n_span + n) * TILES_PER_SPAN + t, 0))
    dils = [d for _, d in SWA_GROUPS]
    return pl.pallas_call(
        _merge_out_kernel,
        grid=(bsz, n_span, TILES_PER_SPAN),
        in_specs=[_tile_unit_spec(d, SWA_OUT) for d in dils] + [_tile_unit_spec(d, LANES) for d in dils]
        + [row(SWA_OUT), row(D_MODEL), _full(w_out.shape), _full(ln_g.shape), _full(ln_b.shape)],
        out_specs=row(D_MODEL),
        out_shape=jax.ShapeDtypeStruct((bsz * seq, D_MODEL), F32),
        scratch_shapes=[pltpu.VMEM((SWA_N_GROUPS, tm, LANES), F32),
                        pltpu.VMEM((SWA_OUT // LANES, tm, LANES), F32)],
        compiler_params=_params("parallel", "parallel", "parallel"),
        name="merge_out_ln",
    )(*os, *sts, g, x2, w_out, ln_g, ln_b)


def _rope_tables(seq):
    half = SWA_HEAD_DIM // 2
    inv = ROPE_THETA ** (-(jnp.arange(half, dtype=F32) * 2.0) / SWA_HEAD_DIM)
    inv = jnp.concatenate([inv, inv])[None, :]
    base = (jnp.arange(seq // ROW_TILE) * ROW_TILE).astype(F32)[:, None] * inv
    off = jnp.arange(ROW_TILE).astype(F32)[:, None] * inv
    return jnp.cos(base), jnp.sin(base), jnp.cos(off), jnp.sin(off)


GLA_N_MAIN = 2 * GLA_DK_TOTAL + 2 * GLA_DV_TOTAL
CAST_ROWS = 128


def _cast_weights_kernel(gw_in_t, gw_a2, gw_out, w_main, w_alr, w_a2, w_out0):
    r = CAST_ROWS
    t = gw_in_t[0]
    for c in range(0, GLA_N_MAIN, r):
        w_main[:, c:c + r] = t[c:c + r, :].T.astype(BF16)
    tail = t[GLA_N_MAIN + GLA_GATE_RANK - r:, :].T
    lane = lax.broadcasted_iota(jnp.int32, (r, r), 1)
    w_alr[...] = jnp.where(lane >= r - GLA_GATE_RANK, tail, 0.0).astype(BF16)
    w_a2[...] = jnp.zeros_like(w_a2)
    w_a2[LANES - GLA_GATE_RANK:, :] = gw_a2[0].astype(BF16)
    w_out0[...] = gw_out[0].astype(BF16)


def _cast_weights(gla_w_in, gla_w_a2, gla_w_out):
    r = CAST_ROWS
    assert r == LANES
    rows3 = lambda n: pl.BlockSpec((1, r, n), lambda i: (0, i, 0))
    rows2 = lambda n: pl.BlockSpec((r, n), lambda i: (i, 0))
    gla_w_in_t = jnp.swapaxes(gla_w_in, 1, 2)
    cols_t = pl.BlockSpec((1, gla_w_in_t.shape[1], r), lambda i: (0, 0, i))
    out_widths = [GLA_N_MAIN, LANES, None, D_MODEL]
    out_shapes = [(D_MODEL, w) if w else (LANES, GLA_DK_TOTAL) for w in out_widths]
    return pl.pallas_call(
        _cast_weights_kernel,
        grid=(D_MODEL // r,),
        in_specs=[cols_t, _full(gla_w_a2.shape), rows3(D_MODEL)],
        out_specs=[rows2(w) if w else pl.BlockSpec((LANES, GLA_DK_TOTAL), lambda i: (0, 0)) for w in out_widths],
        out_shape=[jax.ShapeDtypeStruct(s, BF16) for s in out_shapes],
        compiler_params=_params("arbitrary"),
        name="cast_weights",
    )(gla_w_in_t, gla_w_a2, gla_w_out)


def kernel(x, gla_w_in, gla_w_a2, gla_b_a2, gla_norm_g, gla_w_out, w_kv, swa_w_in, swa_w_out, ln_g, ln_b):
    bsz, seq, _ = x.shape
    assert seq % SWA_SPAN == 0 and seq % GLA_BLOCK == 0
    assert gla_w_in.shape[0] == 1 and swa_w_in.shape[0] == 1 and ln_g.shape[0] == DEPTH
    m = bsz * seq

    w_main, w_alr, w_a2, w_out0 = _cast_weights(gla_w_in, gla_w_a2, gla_w_out)
    x2, w_all, w_out1 = _gla_layer(x, w_main, w_alr, w_a2, gla_b_a2, gla_norm_g, w_out0, ln_g, ln_b,
                                   w_kv, swa_w_in, swa_w_out)
    x2 = x2.reshape(m, D_MODEL)

    ks, vs, qs, g = _swa_in_proj(x2, w_all, _rope_tables(seq), bsz, seq)
    os, sts = zip(*[_swa_attention(qs[gi], ks[gi], vs[gi]) for gi in range(SWA_N_GROUPS)])
    out = _merge_out_ln(os, sts, g, x2, w_out1, ln_g, ln_b, bsz, seq)
    return out.reshape(bsz, seq, D_MODEL)
```

```python
import functools

import jax
import jax.numpy as jnp
from jax import lax
from jax.experimental import pallas as pl
from jax.experimental.pallas import tpu as pltpu

BF16 = jnp.bfloat16
F32 = jnp.float32

D_MODEL = 1024
DEPTH = 2
DEEPNORM_ALPHA = (2.0 * DEPTH) ** 0.25
LN_EPS = 1e-5
RMS_EPS = 1e-5

GLA_HEADS = 4
GLA_DK = 128
GLA_DV = 256
GLA_DK_TOTAL = GLA_HEADS * GLA_DK
GLA_DV_TOTAL = GLA_HEADS * GLA_DV
GLA_GATE_RANK = 16
GLA_GATE_TAU = 16.0
GLA_SUB = 64
GLA_CHUNK = 2 * GLA_SUB
GLA_BLOCK = 512
GLA_HEAD_GROUP = 4
GLA_FAST_MIN_LOG_DECAY = -40.0

SWA_GROUPS = ((128, 1), (512, 4), (2048, 16))
SWA_N_GROUPS = len(SWA_GROUPS)
SWA_HEAD_DIM = 128
SWA_Q_HEADS = 8
SWA_KV_HEADS = 2
SWA_REP = SWA_Q_HEADS // SWA_KV_HEADS
SWA_OUT = SWA_Q_HEADS * SWA_HEAD_DIM
SWA_KV_W = SWA_KV_HEADS * SWA_HEAD_DIM
SWA_KV_TOTAL = 2 * SWA_N_GROUPS * SWA_KV_W
SWA_BLOCK = 128
SWA_SPAN = 2048
SWA_STAT_LANES = SWA_HEAD_DIM // (2 * SWA_Q_HEADS)
ROPE_THETA = 10000.0

LANES = 128
BF16_ROWS = 16
F32_ROWS = 8
MXU_COLS = 256
ROW_TILE = 1024
TILES_PER_SPAN = SWA_SPAN // ROW_TILE
VMEM_LIMIT = 56 * 1024 * 1024

_NT = (((1,), (1,)), ((), ()))
_TN = (((0,), (0,)), ((), ()))


def _params(*sem, flags=None):
    return pltpu.CompilerParams(dimension_semantics=sem, vmem_limit_bytes=VMEM_LIMIT, flags=flags)


def _full(shape):
    return pl.BlockSpec(shape, lambda *_: (0,) * len(shape), pipeline_mode=pl.Buffered(1))


def _layer_norm(h, g, b):
    mu = jnp.mean(h, axis=-1, keepdims=True)
    hc = h - mu
    var = jnp.mean(hc * hc, axis=-1, keepdims=True)
    return hc * lax.rsqrt(var + LN_EPS) * g + b


def _silu(g):
    return g * (0.5 + 0.5 * jnp.tanh(0.5 * g))


def _gla_project_pieces(x_ref, w_ref, walr_ref, wa2_ref, ba2_ref, xb_s, q_ref, k_ref, v_ref, g_ref, la_ref):
    xb_s[...] = x_ref[0].astype(BF16)

    def tile(dst, col, lo, scale):
        def piece():
            y = jnp.dot(xb_s[...], w_ref[:, col + lo:col + lo + MXU_COLS], preferred_element_type=F32)
            dst[:, lo:lo + MXU_COLS] = (y if scale is None else y * scale).astype(BF16)
        return piece

    def gate_piece():
        a_lr = jnp.dot(xb_s[...], walr_ref[...], preferred_element_type=F32)
        z = jnp.dot(a_lr.astype(BF16), wa2_ref[...], preferred_element_type=F32) + ba2_ref[...]
        log_sig = jnp.minimum(z, 0.0) - jnp.log(1.0 + jnp.exp(-jnp.abs(z)))
        la_ref[...] = log_sig * (1.0 / GLA_GATE_TAU)

    pieces, col = [gate_piece], 0
    for dst, scale in ((q_ref, GLA_DK ** -0.5), (k_ref, None), (v_ref, None), (g_ref, None)):
        pieces += [tile(dst, col, lo, scale) for lo in range(0, dst.shape[1], MXU_COLS)]
        col += dst.shape[1]
    return pieces


def _gla_recur_pieces(q_ref, k_ref, v_ref, g_ref, la_ref, ng_ref, o_ref,
                      st_ref, qe_s, qi_s, kd_s, kx_s, klt_s, gam_s):
    t = la_ref.shape[0]
    c, sub = GLA_CHUNK, GLA_SUB

    row = lax.broadcasted_iota(jnp.int32, (c, c), 0)
    col = lax.broadcasted_iota(jnp.int32, (c, c), 1)
    causal = col <= row
    tri = jnp.where(causal & (row // sub == col // sub), 1.0, 0.0).astype(BF16)
    halves = lambda first, second: jnp.concatenate([first, second], axis=0)

    for ic in range(t // c):
        rows = slice(ic * c, (ic + 1) * c)
        first = slice(ic * c, ic * c + sub)
        second = slice(ic * c + sub, (ic + 1) * c)

        la = la_ref[rows, :]
        la_hi = la.astype(BF16)
        la_lo = (la - la_hi.astype(F32)).astype(BF16)
        bs = jnp.dot(tri, la_hi, preferred_element_type=F32) + jnp.dot(tri, la_lo, preferred_element_type=F32)
        e_a = jnp.exp(bs[sub - 1:sub, :])
        e_b = jnp.exp(bs[c - 1:c, :])
        qi = q_ref[rows, :].astype(F32) * jnp.exp(bs)
        kd = k_ref[rows, :].astype(F32) * jnp.exp(-bs)
        kl_a, kl_b = kd[:sub] * e_a, kd[sub:] * e_b
        qi_s[rows, :] = qi.astype(BF16)
        kd_s[rows, :] = kd.astype(BF16)
        qe_s[rows, :] = halves(qi[:sub], qi[sub:] * e_a).astype(BF16)
        kx_s[rows, :] = halves(kl_a, kd[sub:]).astype(BF16)
        klc = halves(kl_a * e_b, kl_b)
        gam = e_a * e_b
        for h in range(GLA_HEADS):
            kcols = slice(h * GLA_DK, (h + 1) * GLA_DK)
            klt_s[ic, h] = klc[:, kcols].T.astype(BF16)
            gam_s[ic, h] = jnp.broadcast_to(gam[:, kcols], (c, GLA_DK)).T
        yield

        for h0 in range(0, GLA_HEADS, GLA_HEAD_GROUP):
            stage1 = []
            for h in range(h0, h0 + GLA_HEAD_GROUP):
                kcols = slice(h * GLA_DK, (h + 1) * GLA_DK)
                st = st_ref[h]
                o_inter = jnp.dot(qe_s[rows, kcols], st.astype(BF16), preferred_element_type=F32)
                att = jnp.concatenate(
                    [lax.dot_general(qi_s[first, kcols], kd_s[rows, kcols], _NT, preferred_element_type=F32),
                     lax.dot_general(qi_s[second, kcols], kx_s[rows, kcols], _NT, preferred_element_type=F32)],
                    axis=0)
                stage1.append((st, o_inter, att))
            yield
            for h, (st, o_inter, att) in zip(range(h0, h0 + GLA_HEAD_GROUP), stage1):
                vcols = slice(h * GLA_DV, (h + 1) * GLA_DV)
                att = jnp.where(causal, att, 0.0).astype(BF16)
                both = jnp.dot(jnp.concatenate([att, klt_s[ic, h]], axis=0), v_ref[rows, vcols],
                               preferred_element_type=F32)
                gam_t = gam_s[ic, h]
                st_ref[h] = st * jnp.concatenate([gam_t] * (GLA_DV // GLA_DK), axis=1) + both[c:]
                o = o_inter + both[0:c]
                o = o * lax.rsqrt(jnp.mean(o * o, axis=-1, keepdims=True) + RMS_EPS) * ng_ref[:, vcols]
                o_ref[rows, vcols] = o.astype(BF16) * _silu(g_ref[rows, vcols])
            yield


def _pairwise_scores(b_s, qf_s, kf_s, kcols):
    c = GLA_CHUNK
    bj = b_s[:, kcols]
    kj = kf_s[:, kcols]
    j_idx = lax.broadcasted_iota(jnp.int32, (c, 1), 0)
    lane = lax.broadcasted_iota(jnp.int32, (c, c), 1)

    def query_rows(g, att_t):
        base = pl.multiple_of(g * F32_ROWS, F32_ROWS)
        b_g = b_s[pl.ds(base, F32_ROWS), kcols]
        q_g = qf_s[pl.ds(base, F32_ROWS), kcols]
        for r in range(F32_ROWS):
            i = g * F32_ROWS + r
            w = jnp.exp(jnp.minimum(b_g[r:r + 1] - bj, 0.0)) * kj * q_g[r:r + 1]
            col = jnp.sum(w, axis=-1, keepdims=True)
            att_t = jnp.where(lane == i, jnp.where(j_idx <= i, col, 0.0), att_t)
        return att_t

    return lax.fori_loop(0, c // F32_ROWS, query_rows, jnp.zeros((c, c), F32)).T


def _gla_recur_pairwise(q_ref, k_ref, v_ref, g_ref, la_ref, ng_ref, o_ref, st_ref, b_s, qf_s, kf_s):
    c, sub = GLA_CHUNK, GLA_SUB
    row = lax.broadcasted_iota(jnp.int32, (c, c), 0)
    col = lax.broadcasted_iota(jnp.int32, (c, c), 1)
    tri = jnp.where((col <= row) & (row // sub == col // sub), 1.0, 0.0).astype(BF16)

    def chunk(ic, carry):
        rows = pl.ds(pl.multiple_of(ic * c, c), c)
        la = la_ref[rows, :]
        la_hi = la.astype(BF16)
        la_lo = (la - la_hi.astype(F32)).astype(BF16)
        bs = jnp.dot(tri, la_hi, preferred_element_type=F32) + jnp.dot(tri, la_lo, preferred_element_type=F32)
        b = jnp.concatenate([bs[:sub], bs[sub:] + bs[sub - 1:sub, :]], axis=0)
        qf, kf = q_ref[rows, :].astype(F32), k_ref[rows, :].astype(F32)
        b_s[...], qf_s[...], kf_s[...] = b, qf, kf
        qe = (qf * jnp.exp(b)).astype(BF16)
        kl = kf * jnp.exp(b[c - 1:c, :] - b)
        gam = jnp.exp(b[c - 1:c, :])
        for h in range(GLA_HEADS):
            kcols = slice(h * GLA_DK, (h + 1) * GLA_DK)
            vcols = slice(h * GLA_DV, (h + 1) * GLA_DV)
            st = st_ref[h]
            att = _pairwise_scores(b_s, qf_s, kf_s, kcols).astype(BF16)
            o_inter = jnp.dot(qe[:, kcols], st.astype(BF16), preferred_element_type=F32)
            both = jnp.dot(jnp.concatenate([att, kl[:, kcols].T.astype(BF16)], axis=0), v_ref[rows, vcols],
                           preferred_element_type=F32)
            gam_t = jnp.broadcast_to(gam[:, kcols], (c, GLA_DK)).T
            st_ref[h] = st * jnp.concatenate([gam_t] * (GLA_DV // GLA_DK), axis=1) + both[c:]
            o = o_inter + both[0:c]
            o = o * lax.rsqrt(jnp.mean(o * o, axis=-1, keepdims=True) + RMS_EPS) * ng_ref[:, vcols]
            o_ref[rows, vcols] = o.astype(BF16) * _silu(g_ref[rows, vcols])
        return carry

    lax.fori_loop(0, la_ref.shape[0] // c, chunk, 0)


N_PROJ = 5


def _gla_layer_kernel(xp_ref, xr_ref, w_ref, walr_ref, wa2_ref, ba2_ref, ng_ref, wout_ref, lg_ref, lb_ref,
                      wkv_ref, swin_ref, swout_ref, out_ref, wall_ref, wout1_ref, st_ref, *scratch,
                      blocks_per_seq):
    bufs = (scratch[0:N_PROJ], scratch[N_PROJ:2 * N_PROJ])
    xb_s, rec_s = scratch[2 * N_PROJ], scratch[2 * N_PROJ + 1:-5]
    pair_s, st0_s, og_s = scratch[-5:-2], scratch[-2], scratch[-1]
    j = pl.program_id(0)
    wall_ref[:, 0:SWA_KV_TOTAL] = wkv_ref[...].astype(BF16)
    wall_ref[:, SWA_KV_TOTAL:] = swin_ref[0].astype(BF16)
    wout1_ref[...] = swout_ref[0].astype(BF16)

    @pl.when(j == 0)
    def _():
        for ref in bufs[1]:
            ref[...] = jnp.zeros_like(ref)

    @pl.when((j == 0) | (j % blocks_per_seq == 1 % blocks_per_seq))
    def _():
        st_ref[...] = jnp.zeros_like(st_ref)

    def step(write, read):
        totals = jnp.sum(read[N_PROJ - 1][...].reshape(-1, GLA_SUB, GLA_DK_TOTAL), axis=1)
        out_of_range = jnp.min(totals) < GLA_FAST_MIN_LOG_DECAY
        st0_s[...] = st_ref[...]

        pieces = _gla_project_pieces(xp_ref, w_ref, walr_ref, wa2_ref, ba2_ref, xb_s, *write)
        n_pieces = len(pieces)
        n_stages = (xb_s.shape[0] // GLA_CHUNK) * (1 + 2 * (GLA_HEADS // GLA_HEAD_GROUP))
        for i, _ in enumerate(_gla_recur_pieces(*read, ng_ref, og_s, st_ref, *rec_s)):
            issued_after = ((i + 1) * n_pieces + n_stages - 1) // n_stages
            while n_pieces - len(pieces) < issued_after:
                pieces.pop(0)()

        @pl.when(out_of_range)
        def _():
            st_ref[...] = st0_s[...]
            _gla_recur_pairwise(*read, ng_ref, og_s, st_ref, *pair_s)

        y = jnp.dot(og_s[...], wout_ref[...], preferred_element_type=F32)
        out_ref[0] = _layer_norm(DEEPNORM_ALPHA * xr_ref[0] + y, lg_ref[0:1, :], lb_ref[0:1, :])

    @pl.when(j % 2 == 0)
    def _():
        step(bufs[0], bufs[1])

    @pl.when(j % 2 == 1)
    def _():
        step(bufs[1], bufs[0])


def _gla_layer(x, w_main, w_alr, w_a2, b_a2, norm_g, w_out, ln_g, ln_b, w_kv, swa_w_in, swa_w_out):
    bsz, s, _ = x.shape
    t = GLA_BLOCK
    n_blk = s // t
    n_chunks = t // GLA_CHUNK
    rows = lambda n, dt: pltpu.VMEM((t, n), dt)
    proj_bufs = [rows(GLA_DK_TOTAL, BF16), rows(GLA_DK_TOTAL, BF16), rows(GLA_DV_TOTAL, BF16),
                 rows(GLA_DV_TOTAL, BF16), rows(GLA_DK_TOTAL, F32)]
    assert len(proj_bufs) == N_PROJ
    n_all = bsz * n_blk
    xb = x.reshape(n_all, t, D_MODEL)
    cur = pl.BlockSpec((1, t, D_MODEL), lambda j: (jnp.minimum(j, n_all - 1), 0, 0))
    prev = pl.BlockSpec((1, t, D_MODEL), lambda j: (jnp.maximum(j - 1, 0), 0, 0))
    consts = (w_main, w_alr, w_a2, b_a2, norm_g, w_out, ln_g, ln_b)
    wr = D_MODEL // n_all
    assert wr * n_all == D_MODEL and wr % BF16_ROWS == 0
    wrows2 = lambda n: pl.BlockSpec((wr, n), lambda j: (jnp.minimum(j, n_all - 1), 0))
    wrows3 = lambda n: pl.BlockSpec((1, wr, n), lambda j: (0, jnp.minimum(j, n_all - 1), 0))
    n_in = w_kv.shape[1] + swa_w_in.shape[2]
    return pl.pallas_call(
        functools.partial(_gla_layer_kernel, blocks_per_seq=n_blk),
        grid=(n_all + 1,),
        in_specs=[cur, prev] + [_full(a.shape) for a in consts]
        + [wrows2(w_kv.shape[1]), wrows3(swa_w_in.shape[2]), wrows3(D_MODEL)],
        out_specs=[prev, wrows2(n_in), wrows2(D_MODEL)],
        out_shape=[jax.ShapeDtypeStruct(xb.shape, F32), jax.ShapeDtypeStruct((D_MODEL, n_in), BF16),
                   jax.ShapeDtypeStruct((D_MODEL, D_MODEL), BF16)],
        scratch_shapes=[pltpu.VMEM((GLA_HEADS, GLA_DK, GLA_DV), F32)] + proj_bufs + proj_bufs
        + [rows(D_MODEL, BF16)] + [rows(GLA_DK_TOTAL, BF16)] * 4
        + [pltpu.VMEM((n_chunks, GLA_HEADS, GLA_DK, GLA_CHUNK), BF16),
           pltpu.VMEM((n_chunks, GLA_HEADS, GLA_DK, GLA_CHUNK), F32)]
        + [pltpu.VMEM((GLA_CHUNK, GLA_DK_TOTAL), F32)] * 3 + [pltpu.VMEM((GLA_HEADS, GLA_DK, GLA_DV), F32)]
        + [rows(GLA_DV_TOTAL, BF16)],
        compiler_params=_params("arbitrary"),
        name="gla_layer",
    )(xb, xb, *consts, w_kv, swa_w_in, swa_w_out)


def _unit_shape(bsz, seq, d, width):
    return (bsz, seq // SWA_SPAN, SWA_SPAN // (SWA_BLOCK * d), d, SWA_BLOCK, width)


def _tile_unit_spec(d, width):
    rows_per = ROW_TILE // d
    if rows_per >= SWA_BLOCK:
        jb = rows_per // SWA_BLOCK
        return pl.BlockSpec((1, 1, jb, d, SWA_BLOCK, width), lambda b, n, t: (b, n, t, 0, 0, 0))
    per_j = SWA_BLOCK // rows_per
    return pl.BlockSpec((1, 1, 1, d, rows_per, width), lambda b, n, t: (b, n, t // per_j, 0, t % per_j, 0))


def _tile_unit_rows(d):
    rows_per = ROW_TILE // d
    return max(rows_per // SWA_BLOCK, 1), min(rows_per, SWA_BLOCK)


def _token_rows(d, jj, r, rows):
    return pl.ds(jj * SWA_BLOCK * d + r, rows, stride=d) if d > 1 else pl.ds(jj * SWA_BLOCK, rows)


def _rope(h, cos, sin_signed):
    return h * cos + pltpu.roll(h, SWA_HEAD_DIM // 2, 1) * sin_signed


def _swa_in_kernel(x_ref, w_ref, cos_base_ref, sin_base_ref, cos_off_ref, sin_off_ref, *refs):
    k_refs = refs[0:SWA_N_GROUPS]
    v_refs = refs[SWA_N_GROUPS:2 * SWA_N_GROUPS]
    q_refs = refs[2 * SWA_N_GROUPS:3 * SWA_N_GROUPS]
    g_ref = refs[3 * SWA_N_GROUPS]
    tok_s = refs[3 * SWA_N_GROUPS + 1]
    rope_s = refs[3 * SWA_N_GROUPS + 2]
    xb = x_ref[...].astype(BF16)
    e = SWA_HEAD_DIM
    tile = pl.ds(pl.program_id(1) * TILES_PER_SPAN + pl.program_id(2), 1)
    ca, sa = cos_base_ref[tile, :], sin_base_ref[tile, :]
    cb, sb = cos_off_ref[...], sin_off_ref[...]
    cos = ca * cb - sa * sb
    lane = lax.broadcasted_iota(jnp.int32, cb.shape, 1)
    sin = jnp.where(lane < e // 2, -1.0, 1.0) * (sa * cb + ca * sb)
    rope_s[0], rope_s[1] = cos, sin
    rope_s[2], rope_s[3] = cos * (e ** -0.5), sin * (e ** -0.5)
    kv_half = SWA_N_GROUPS * SWA_KV_W
    q0 = 2 * kv_half

    n_slabs = tok_s.shape[0]
    pipeline = {"slab": 0, "pending": None}

    def head_pair(col, out_ref, head0, d, rope):
        y = jnp.dot(xb, w_ref[:, col:col + MXU_COLS], preferred_element_type=F32)
        jb, rows = _tile_unit_rows(d)
        staged = []
        for i in range(MXU_COLS // e):
            h = head0 + i
            yh = y[:, i * e:(i + 1) * e]
            if rope is not None:
                yh = _rope(yh, rope_s[rope], rope_s[rope + 1])
            if d == 1:
                out_ref[0, 0, :, :, :, h * e:(h + 1) * e] = yh.astype(BF16).reshape(jb, d, rows, e)
                continue
            slab = pipeline["slab"]
            pipeline["slab"] = (slab + 1) % n_slabs
            tok_s[slab] = yh
            staged.append((slab, h))

        def scatter():
            for slab, h in staged:
                for jj in range(jb):
                    for r in range(d):
                        out_ref[0, 0, jj, r, :, h * e:(h + 1) * e] = (
                            tok_s[slab, _token_rows(d, jj, r, rows), :].astype(BF16))

        flush()
        pipeline["pending"] = scatter if staged else None

    def flush():
        if pipeline["pending"] is not None:
            pipeline["pending"]()
            pipeline["pending"] = None

    for gi, (_, d) in enumerate(SWA_GROUPS):
        head_pair(gi * SWA_KV_W, k_refs[gi], 0, d, 0)
        head_pair(kv_half + gi * SWA_KV_W, v_refs[gi], 0, d, None)
        for h0 in range(0, SWA_Q_HEADS, MXU_COLS // e):
            head_pair(q0 + gi * SWA_OUT + h0 * e, q_refs[gi], h0, d, 2)
    g0 = q0 + SWA_N_GROUPS * SWA_OUT
    for c in range(0, SWA_OUT, MXU_COLS):
        g_ref[:, c:c + MXU_COLS] = jnp.dot(xb, w_ref[:, g0 + c:g0 + c + MXU_COLS],
                                           preferred_element_type=F32).astype(BF16)
        flush()


def _swa_in_proj(x2, w_all, rope_tables, bsz, seq):
    tm = ROW_TILE
    n_span = seq // SWA_SPAN
    tile = lambda b, n, t: (b * n_span + n) * TILES_PER_SPAN + t
    row = lambda w: pl.BlockSpec((tm, w), lambda b, n, t: (tile(b, n, t), 0))
    dils = [d for _, d in SWA_GROUPS]
    unit_out = [(d, SWA_KV_W) for d in dils] * 2 + [(d, SWA_OUT) for d in dils]
    outs = pl.pallas_call(
        _swa_in_kernel,
        grid=(bsz, n_span, TILES_PER_SPAN),
        in_specs=[row(D_MODEL), _full(w_all.shape)] + [_full(a.shape) for a in rope_tables],
        out_specs=[_tile_unit_spec(d, w) for d, w in unit_out] + [row(SWA_OUT)],
        out_shape=[jax.ShapeDtypeStruct(_unit_shape(bsz, seq, d, w), BF16) for d, w in unit_out]
        + [jax.ShapeDtypeStruct((bsz * seq, SWA_OUT), BF16)],
        scratch_shapes=[pltpu.VMEM((SWA_OUT // LANES, tm, LANES), F32), pltpu.VMEM((4, tm, SWA_HEAD_DIM), F32)],
        compiler_params=_params("parallel", "parallel", "parallel"),
        name="swa_in_proj",
    )(x2, w_all, *rope_tables)
    return outs[0:3], outs[3:6], outs[6:9], outs[9]


def _swa_kernel(q_ref, kc_ref, kp_ref, vc_ref, vp_ref, o_ref, st_ref, k_s, v_s, bias_s):
    blk = SWA_BLOCK
    e = SWA_HEAD_DIM
    n_j, d = q_ref.shape[2], q_ref.shape[3]
    n = pl.program_id(1)

    k_s[0] = kp_ref[0, 0, 0]
    k_s[1:] = kc_ref[0, 0]
    v_s[0] = vp_ref[0, 0, 0]
    v_s[1:] = vc_ref[0, 0]

    rows = SWA_REP * blk
    qi = lax.broadcasted_iota(jnp.int32, (rows, 2 * blk), 0) % blk
    kj = lax.broadcasted_iota(jnp.int32, (rows, 2 * blk), 1)
    rel = blk + qi - kj
    band = (rel >= 0) & (rel <= blk)
    bias_s[0] = jnp.where(band, 0.0, -jnp.inf)
    bias_s[1] = jnp.where(band & (kj >= blk), 0.0, -jnp.inf)
    lane_slot = lax.broadcasted_iota(jnp.int32, (blk, LANES), 1) // SWA_STAT_LANES
    ones = jnp.ones((2 * blk, LANES), BF16)

    def unit(idx, carry):
        j = idx // d
        r = idx % d
        bias = bias_s[jnp.where((j == 0) & (n == 0), 1, 0)]
        st_tile = jnp.zeros((blk, LANES), F32)
        for kvh in range(SWA_KV_HEADS):
            q4 = q_ref[0, 0, j, r, :, kvh * SWA_REP * e:(kvh + 1) * SWA_REP * e]
            qs = jnp.concatenate([q4[:, h * e:(h + 1) * e] for h in range(SWA_REP)], axis=0)
            kcols = slice(kvh * e, (kvh + 1) * e)
            kk = jnp.concatenate([k_s[j, r, :, kcols], k_s[j + 1, r, :, kcols]], axis=0)
            vv = jnp.concatenate([v_s[j, r, :, kcols], v_s[j + 1, r, :, kcols]], axis=0)
            s = lax.dot_general(qs, kk, _NT, preferred_element_type=F32) + bias
            m = jnp.max(s, axis=-1, keepdims=True)
            p = jnp.exp(s - m).astype(BF16)
            o = jnp.dot(p, jnp.concatenate([vv, ones], axis=1), preferred_element_type=F32)
            for h in range(SWA_REP):
                head = kvh * SWA_REP + h
                o_ref[0, 0, j, r, :, head * e:(head + 1) * e] = o[h * blk:(h + 1) * blk, 0:e].astype(BF16)
                st_tile = jnp.where(lane_slot == 2 * head, m[h * blk:(h + 1) * blk], st_tile)
                st_tile = jnp.where(lane_slot == 2 * head + 1, o[h * blk:(h + 1) * blk, e:], st_tile)
        st_ref[0, 0, j, r] = st_tile
        return carry

    lax.fori_loop(0, n_j * d, unit, 0, unroll=True)


def _swa_attention(q, k, v):
    bsz, n_span, n_j, d, blk, _ = q.shape
    cur = lambda w: pl.BlockSpec((1, 1, n_j, d, blk, w), lambda b, n: (b, n, 0, 0, 0, 0))
    prev = pl.BlockSpec((1, 1, 1, d, blk, SWA_KV_W), lambda b, n: (b, jnp.maximum(n - 1, 0), n_j - 1, 0, 0, 0))
    stat_shape = q.shape[:-1] + (LANES,)
    return pl.pallas_call(
        _swa_kernel,
        grid=(bsz, n_span),
        in_specs=[cur(SWA_OUT), cur(SWA_KV_W), prev, cur(SWA_KV_W), prev],
        out_specs=[cur(SWA_OUT), cur(LANES)],
        out_shape=[jax.ShapeDtypeStruct(q.shape, BF16), jax.ShapeDtypeStruct(stat_shape, F32)],
        scratch_shapes=[pltpu.VMEM((n_j + 1, d, blk, SWA_KV_W), BF16),
                        pltpu.VMEM((n_j + 1, d, blk, SWA_KV_W), BF16),
                        pltpu.VMEM((2, SWA_REP * blk, 2 * blk), F32)],
        compiler_params=_params("parallel", "parallel"),
        name=f"swa_attention_d{d}",
    )(q, k, k, v, v)


def _merge_out_kernel(*refs):
    n_g = SWA_N_GROUPS
    o_refs, st_refs = refs[0:n_g], refs[n_g:2 * n_g]
    g_ref, x_ref, w_ref, lg_ref, lb_ref, out_ref, stat_s, o_s = refs[2 * n_g:]
    tm = ROW_TILE

    def to_tokens(ref, d, dst):
        jb, rows = _tile_unit_rows(d)
        n_slabs = ref.shape[-1] // LANES
        lanes = lambda c: slice(c * LANES, (c + 1) * LANES)
        if d == 1:
            tok = ref[0, 0].reshape(tm, ref.shape[-1]).astype(F32)
            return [tok[:, lanes(c)] for c in range(n_slabs)]
        for jj in range(jb):
            for r in range(d):
                for c in range(n_slabs):
                    dst[c, _token_rows(d, jj, r, rows), :] = ref[0, 0, jj, r, :, lanes(c)].astype(F32)
        return [dst[c] for c in range(n_slabs)]

    dils = [d for _, d in SWA_GROUPS]
    sts = [to_tokens(st_refs[gi], d, stat_s.at[gi:gi + 1])[0] for gi, d in enumerate(dils)]
    mx = functools.reduce(jnp.maximum, sts)
    es = [jnp.exp(st - mx) for st in sts]
    ls = [pltpu.roll(st, LANES - SWA_STAT_LANES, 1) for st in sts]
    tot = functools.reduce(jnp.add, [eg * lg for eg, lg in zip(es, ls)])
    m_slot = (lax.broadcasted_iota(jnp.int32, (tm, LANES), 1) // SWA_STAT_LANES) % 2 == 0
    src = lax.broadcasted_iota(jnp.int32, (LANES, SWA_OUT), 0)
    dst = lax.broadcasted_iota(jnp.int32, (LANES, SWA_OUT), 1)
    expand = jnp.where(src == (dst // SWA_HEAD_DIM) * 2 * SWA_STAT_LANES, 1.0, 0.0).astype(BF16)
    acc = None
    for gi, d in enumerate(dils):
        w = jnp.dot(jnp.where(m_slot, es[gi] / tot, 0.0).astype(BF16), expand, preferred_element_type=F32)
        term = w * jnp.concatenate(to_tokens(o_refs[gi], d, o_s), axis=1)
        acc = term if acc is None else acc + term
    a = acc.astype(BF16) * _silu(g_ref[...])
    y = jnp.dot(a, w_ref[...], preferred_element_type=F32)
    out_ref[...] = _layer_norm(DEEPNORM_ALPHA * x_ref[...] + y, lg_ref[1:2, :], lb_ref[1:2, :])


def _merge_out_ln(os, sts, g, x2, w_out, ln_g, ln_b, bsz, seq):
    tm = ROW_TILE
    n_span = seq // SWA_SPAN
    row = lambda w: pl.BlockSpec((tm, w), lambda b, n, t: ((b * n_span + n) * TILES_PER_SPAN + t, 0))
    dils = [d for _, d in SWA_GROUPS]
    return pl.pallas_call(
        _merge_out_kernel,
        grid=(bsz, n_span, TILES_PER_SPAN),
        in_specs=[_tile_unit_spec(d, SWA_OUT) for d in dils] + [_tile_unit_spec(d, LANES) for d in dils]
        + [row(SWA_OUT), row(D_MODEL), _full(w_out.shape), _full(ln_g.shape), _full(ln_b.shape)],
        out_specs=row(D_MODEL),
        out_shape=jax.ShapeDtypeStruct((bsz * seq, D_MODEL), F32),
        scratch_shapes=[pltpu.VMEM((SWA_N_GROUPS, tm, LANES), F32),
                        pltpu.VMEM((SWA_OUT // LANES, tm, LANES), F32)],
        compiler_params=_params("parallel", "parallel", "parallel"),
        name="merge_out_ln",
    )(*os, *sts, g, x2, w_out, ln_g, ln_b)


def _rope_tables(seq):
    half = SWA_HEAD_DIM // 2
    inv = ROPE_THETA ** (-(jnp.arange(half, dtype=F32) * 2.0) / SWA_HEAD_DIM)
    inv = jnp.concatenate([inv, inv])[None, :]
    base = (jnp.arange(seq // ROW_TILE) * ROW_TILE).astype(F32)[:, None] * inv
    off = jnp.arange(ROW_TILE).astype(F32)[:, None] * inv
    return jnp.cos(base), jnp.sin(base), jnp.cos(off), jnp.sin(off)


GLA_N_MAIN = 2 * GLA_DK_TOTAL + 2 * GLA_DV_TOTAL
CAST_ROWS = 128


def _cast_weights_kernel(gw_in_t, gw_a2, gw_out, w_main, w_alr, w_a2, w_out0):
    r = CAST_ROWS
    t = gw_in_t[0]
    for c in range(0, GLA_N_MAIN, r):
        w_main[:, c:c + r] = t[c:c + r, :].T.astype(BF16)
    tail = t[GLA_N_MAIN + GLA_GATE_RANK - r:, :].T
    lane = lax.broadcasted_iota(jnp.int32, (r, r), 1)
    w_alr[...] = jnp.where(lane >= r - GLA_GATE_RANK, tail, 0.0).astype(BF16)
    w_a2[...] = jnp.zeros_like(w_a2)
    w_a2[LANES - GLA_GATE_RANK:, :] = gw_a2[0].astype(BF16)
    w_out0[...] = gw_out[0].astype(BF16)


def _cast_weights(gla_w_in, gla_w_a2, gla_w_out):
    r = CAST_ROWS
    assert r == LANES
    rows3 = lambda n: pl.BlockSpec((1, r, n), lambda i: (0, i, 0))
    rows2 = lambda n: pl.BlockSpec((r, n), lambda i: (i, 0))
    gla_w_in_t = jnp.swapaxes(gla_w_in, 1, 2)
    cols_t = pl.BlockSpec((1, gla_w_in_t.shape[1], r), lambda i: (0, 0, i))
    out_widths = [GLA_N_MAIN, LANES, None, D_MODEL]
    out_shapes = [(D_MODEL, w) if w else (LANES, GLA_DK_TOTAL) for w in out_widths]
    return pl.pallas_call(
        _cast_weights_kernel,
        grid=(D_MODEL // r,),
        in_specs=[cols_t, _full(gla_w_a2.shape), rows3(D_MODEL)],
        out_specs=[rows2(w) if w else pl.BlockSpec((LANES, GLA_DK_TOTAL), lambda i: (0, 0)) for w in out_widths],
        out_shape=[jax.ShapeDtypeStruct(s, BF16) for s in out_shapes],
        compiler_params=_params("arbitrary"),
        name="cast_weights",
    )(gla_w_in_t, gla_w_a2, gla_w_out)


def kernel(x, gla_w_in, gla_w_a2, gla_b_a2, gla_norm_g, gla_w_out, w_kv, swa_w_in, swa_w_out, ln_g, ln_b):
    bsz, seq, _ = x.shape
    assert seq % SWA_SPAN == 0 and seq % GLA_BLOCK == 0
    assert gla_w_in.shape[0] == 1 and swa_w_in.shape[0] == 1 and ln_g.shape[0] == DEPTH
    m = bsz * seq

    w_main, w_alr, w_a2, w_out0 = _cast_weights(gla_w_in, gla_w_a2, gla_w_out)
    x2, w_all, w_out1 = _gla_layer(x, w_main, w_alr, w_a2, gla_b_a2, gla_norm_g, w_out0, ln_g, ln_b,
                                   w_kv, swa_w_in, swa_w_out)
    x2 = x2.reshape(m, D_MODEL)

    ks, vs, qs, g = _swa_in_proj(x2, w_all, _rope_tables(seq), bsz, seq)
    os, sts = zip(*[_swa_attention(qs[gi], ks[gi], vs[gi]) for gi in range(SWA_N_GROUPS)])
    out = _merge_out_ln(os, sts, g, x2, w_out1, ln_g, ln_b, bsz, seq)
    return out.reshape(bsz, seq, D_MODEL)
```

```python
import functools

import jax
import jax.numpy as jnp
from jax import lax
from jax.experimental import pallas as pl
from jax.experimental.pallas import tpu as pltpu

BF16 = jnp.bfloat16
F32 = jnp.float32

D_MODEL = 1024
DEPTH = 2
DEEPNORM_ALPHA = (2.0 * DEPTH) ** 0.25
LN_EPS = 1e-5
RMS_EPS = 1e-5

GLA_HEADS = 4
GLA_DK = 128
GLA_DV = 256
GLA_DK_TOTAL = GLA_HEADS * GLA_DK
GLA_DV_TOTAL = GLA_HEADS * GLA_DV
GLA_GATE_RANK = 16
GLA_GATE_TAU = 16.0
GLA_SUB = 64
GLA_CHUNK = 2 * GLA_SUB
GLA_BLOCK = 512
GLA_HEAD_GROUP = 4
GLA_FAST_MIN_LOG_DECAY = -40.0

SWA_GROUPS = ((128, 1), (512, 4), (2048, 16))
SWA_N_GROUPS = len(SWA_GROUPS)
SWA_HEAD_DIM = 128
SWA_Q_HEADS = 8
SWA_KV_HEADS = 2
SWA_REP = SWA_Q_HEADS // SWA_KV_HEADS
SWA_OUT = SWA_Q_HEADS * SWA_HEAD_DIM
SWA_KV_W = SWA_KV_HEADS * SWA_HEAD_DIM
SWA_KV_TOTAL = 2 * SWA_N_GROUPS * SWA_KV_W
SWA_BLOCK = 128
SWA_SPAN = 2048
SWA_STAT_LANES = SWA_HEAD_DIM // (2 * SWA_Q_HEADS)
ROPE_THETA = 10000.0

LANES = 128
BF16_ROWS = 16
F32_ROWS = 8
MXU_COLS = 256
ROW_TILE = 1024
TILES_PER_SPAN = SWA_SPAN // ROW_TILE
VMEM_LIMIT = 56 * 1024 * 1024

_NT = (((1,), (1,)), ((), ()))
_TN = (((0,), (0,)), ((), ()))


def _params(*sem, flags=None):
    return pltpu.CompilerParams(dimension_semantics=sem, vmem_limit_bytes=VMEM_LIMIT, flags=flags)


def _full(shape):
    return pl.BlockSpec(shape, lambda *_: (0,) * len(shape), pipeline_mode=pl.Buffered(1))


def _layer_norm(h, g, b):
    mu = jnp.mean(h, axis=-1, keepdims=True)
    hc = h - mu
    var = jnp.mean(hc * hc, axis=-1, keepdims=True)
    return hc * lax.rsqrt(var + LN_EPS) * g + b


def _silu(g):
    return g * (0.5 + 0.5 * jnp.tanh(0.5 * g))


def _gla_project_pieces(x_ref, w_ref, walr_ref, wa2_ref, ba2_ref, xb_s, q_ref, k_ref, v_ref, g_ref, la_ref):
    xb_s[...] = x_ref[0].astype(BF16)

    def tile(dst, col, lo, scale):
        def piece():
            y = jnp.dot(xb_s[...], w_ref[:, col + lo:col + lo + MXU_COLS], preferred_element_type=F32)
            dst[:, lo:lo + MXU_COLS] = (y if scale is None else y * scale).astype(BF16)
        return piece

    def gate_piece():
        a_lr = jnp.dot(xb_s[...], walr_ref[...], preferred_element_type=F32)
        z = jnp.dot(a_lr.astype(BF16), wa2_ref[...], preferred_element_type=F32) + ba2_ref[...]
        log_sig = jnp.minimum(z, 0.0) - jnp.log(1.0 + jnp.exp(-jnp.abs(z)))
        la_ref[...] = log_sig * (1.0 / GLA_GATE_TAU)

    pieces, col = [gate_piece], 0
    for dst, scale in ((q_ref, GLA_DK ** -0.5), (k_ref, None), (v_ref, None), (g_ref, None)):
        pieces += [tile(dst, col, lo, scale) for lo in range(0, dst.shape[1], MXU_COLS)]
        col += dst.shape[1]
    return pieces


def _gla_recur_pieces(q_ref, k_ref, v_ref, g_ref, la_ref, ng_ref, o_ref,
                      st_ref, qe_s, qi_s, kd_s, kx_s, klt_s, gam_s):
    t = la_ref.shape[0]
    c, sub = GLA_CHUNK, GLA_SUB

    row = lax.broadcasted_iota(jnp.int32, (c, c), 0)
    col = lax.broadcasted_iota(jnp.int32, (c, c), 1)
    causal = col <= row
    tri = jnp.where(causal & (row // sub == col // sub), 1.0, 0.0).astype(BF16)
    halves = lambda first, second: jnp.concatenate([first, second], axis=0)

    for ic in range(t // c):
        rows = slice(ic * c, (ic + 1) * c)
        first = slice(ic * c, ic * c + sub)
        second = slice(ic * c + sub, (ic + 1) * c)

        la = la_ref[rows, :]
        la_hi = la.astype(BF16)
        la_lo = (la - la_hi.astype(F32)).astype(BF16)
        bs = jnp.dot(tri, la_hi, preferred_element_type=F32) + jnp.dot(tri, la_lo, preferred_element_type=F32)
        e_a = jnp.exp(bs[sub - 1:sub, :])
        e_b = jnp.exp(bs[c - 1:c, :])
        qi = q_ref[rows, :].astype(F32) * jnp.exp(bs)
        kd = k_ref[rows, :].astype(F32) * jnp.exp(-bs)
        kl_a, kl_b = kd[:sub] * e_a, kd[sub:] * e_b
        qi_s[rows, :] = qi.astype(BF16)
        kd_s[rows, :] = kd.astype(BF16)
        qe_s[rows, :] = halves(qi[:sub], qi[sub:] * e_a).astype(BF16)
        kx_s[rows, :] = halves(kl_a, kd[sub:]).astype(BF16)
        klc = halves(kl_a * e_b, kl_b)
        gam = e_a * e_b
        for h in range(GLA_HEADS):
            kcols = slice(h * GLA_DK, (h + 1) * GLA_DK)
            klt_s[ic, h] = klc[:, kcols].T.astype(BF16)
            gam_s[ic, h] = jnp.broadcast_to(gam[:, kcols], (c, GLA_DK)).T
        yield

        for h0 in range(0, GLA_HEADS, GLA_HEAD_GROUP):
            stage1 = []
            for h in range(h0, h0 + GLA_HEAD_GROUP):
                kcols = slice(h * GLA_DK, (h + 1) * GLA_DK)
                st = st_ref[h]
                o_inter = jnp.dot(qe_s[rows, kcols], st.astype(BF16), preferred_element_type=F32)
                att = jnp.concatenate(
                    [lax.dot_general(qi_s[first, kcols], kd_s[rows, kcols], _NT, preferred_element_type=F32),
                     lax.dot_general(qi_s[second, kcols], kx_s[rows, kcols], _NT, preferred_element_type=F32)],
                    axis=0)
                stage1.append((st, o_inter, att))
            yield
            for h, (st, o_inter, att) in zip(range(h0, h0 + GLA_HEAD_GROUP), stage1):
                vcols = slice(h * GLA_DV, (h + 1) * GLA_DV)
                att = jnp.where(causal, att, 0.0).astype(BF16)
                both = jnp.dot(jnp.concatenate([att, klt_s[ic, h]], axis=0), v_ref[rows, vcols],
                               preferred_element_type=F32)
                gam_t = gam_s[ic, h]
                st_ref[h] = st * jnp.concatenate([gam_t] * (GLA_DV // GLA_DK), axis=1) + both[c:]
                o = o_inter + both[0:c]
                o = o * lax.rsqrt(jnp.mean(o * o, axis=-1, keepdims=True) + RMS_EPS) * ng_ref[:, vcols]
                o_ref[rows, vcols] = o.astype(BF16) * _silu(g_ref[rows, vcols])
            yield


def _pairwise_scores(b_s, qf_s, kf_s, kcols):
    c = GLA_CHUNK
    bj = b_s[:, kcols]
    kj = kf_s[:, kcols]
    j_idx = lax.broadcasted_iota(jnp.int32, (c, 1), 0)
    lane = lax.broadcasted_iota(jnp.int32, (c, c), 1)

    def query_rows(g, att_t):
        base = pl.multiple_of(g * F32_ROWS, F32_ROWS)
        b_g = b_s[pl.ds(base, F32_ROWS), kcols]
        q_g = qf_s[pl.ds(base, F32_ROWS), kcols]
        for r in range(F32_ROWS):
            i = g * F32_ROWS + r
            w = jnp.exp(jnp.minimum(b_g[r:r + 1] - bj, 0.0)) * kj * q_g[r:r + 1]
            col = jnp.sum(w, axis=-1, keepdims=True)
            att_t = jnp.where(lane == i, jnp.where(j_idx <= i, col, 0.0), att_t)
        return att_t

    return lax.fori_loop(0, c // F32_ROWS, query_rows, jnp.zeros((c, c), F32)).T


def _gla_recur_pairwise(q_ref, k_ref, v_ref, g_ref, la_ref, ng_ref, o_ref, st_ref, b_s, qf_s, kf_s):
    c, sub = GLA_CHUNK, GLA_SUB
    row = lax.broadcasted_iota(jnp.int32, (c, c), 0)
    col = lax.broadcasted_iota(jnp.int32, (c, c), 1)
    tri = jnp.where((col <= row) & (row // sub == col // sub), 1.0, 0.0).astype(BF16)

    def chunk(ic, carry):
        rows = pl.ds(pl.multiple_of(ic * c, c), c)
        la = la_ref[rows, :]
        la_hi = la.astype(BF16)
        la_lo = (la - la_hi.astype(F32)).astype(BF16)
        bs = jnp.dot(tri, la_hi, preferred_element_type=F32) + jnp.dot(tri, la_lo, preferred_element_type=F32)
        b = jnp.concatenate([bs[:sub], bs[sub:] + bs[sub - 1:sub, :]], axis=0)
        qf, kf = q_ref[rows, :].astype(F32), k_ref[rows, :].astype(F32)
        b_s[...], qf_s[...], kf_s[...] = b, qf, kf
        qe = (qf * jnp.exp(b)).astype(BF16)
        kl = kf * jnp.exp(b[c - 1:c, :] - b)
        gam = jnp.exp(b[c - 1:c, :])
        for h in range(GLA_HEADS):
            kcols = slice(h * GLA_DK, (h + 1) * GLA_DK)
            vcols = slice(h * GLA_DV, (h + 1) * GLA_DV)
            st = st_ref[h]
            att = _pairwise_scores(b_s, qf_s, kf_s, kcols).astype(BF16)
            o_inter = jnp.dot(qe[:, kcols], st.astype(BF16), preferred_element_type=F32)
            both = jnp.dot(jnp.concatenate([att, kl[:, kcols].T.astype(BF16)], axis=0), v_ref[rows, vcols],
                           preferred_element_type=F32)
            gam_t = jnp.broadcast_to(gam[:, kcols], (c, GLA_DK)).T
            st_ref[h] = st * jnp.concatenate([gam_t] * (GLA_DV // GLA_DK), axis=1) + both[c:]
            o = o_inter + both[0:c]
            o = o * lax.rsqrt(jnp.mean(o * o, axis=-1, keepdims=True) + RMS_EPS) * ng_ref[:, vcols]
            o_ref[rows, vcols] = o.astype(BF16) * _silu(g_ref[rows, vcols])
        return carry

    lax.fori_loop(0, la_ref.shape[0] // c, chunk, 0)


N_PROJ = 5


def _gla_layer_kernel(xp_ref, xr_ref, w_ref, walr_ref, wa2_ref, ba2_ref, ng_ref, wout_ref, lg_ref, lb_ref,
                      wkv_ref, swin_ref, swout_ref, out_ref, wall_ref, wout1_ref, st_ref, *scratch,
                      blocks_per_seq):
    bufs = (scratch[0:N_PROJ], scratch[N_PROJ:2 * N_PROJ])
    xb_s, rec_s = scratch[2 * N_PROJ], scratch[2 * N_PROJ + 1:-5]
    pair_s, st0_s, og_s = scratch[-5:-2], scratch[-2], scratch[-1]
    j = pl.program_id(0)
    wall_ref[:, 0:SWA_KV_TOTAL] = wkv_ref[...].astype(BF16)
    wall_ref[:, SWA_KV_TOTAL:] = swin_ref[0].astype(BF16)
    wout1_ref[...] = swout_ref[0].astype(BF16)

    @pl.when(j == 0)
    def _():
        for ref in bufs[1]:
            ref[...] = jnp.zeros_like(ref)

    @pl.when((j == 0) | (j % blocks_per_seq == 1 % blocks_per_seq))
    def _():
        st_ref[...] = jnp.zeros_like(st_ref)

    def step(write, read):
        totals = jnp.sum(read[N_PROJ - 1][...].reshape(-1, GLA_SUB, GLA_DK_TOTAL), axis=1)
        out_of_range = jnp.min(totals) < GLA_FAST_MIN_LOG_DECAY
        st0_s[...] = st_ref[...]

        pieces = _gla_project_pieces(xp_ref, w_ref, walr_ref, wa2_ref, ba2_ref, xb_s, *write)
        n_pieces = len(pieces)
        n_stages = (xb_s.shape[0] // GLA_CHUNK) * (1 + 2 * (GLA_HEADS // GLA_HEAD_GROUP))
        for i, _ in enumerate(_gla_recur_pieces(*read, ng_ref, og_s, st_ref, *rec_s)):
            issued_after = ((i + 1) * n_pieces + n_stages - 1) // n_stages
            while n_pieces - len(pieces) < issued_after:
                pieces.pop(0)()

        def project_out():
            y = jnp.dot(og_s[...], wout_ref[...], preferred_element_type=F32)
            out_ref[0] = _layer_norm(DEEPNORM_ALPHA * xr_ref[0] + y, lg_ref[0:1, :], lb_ref[0:1, :])

        project_out()

        @pl.when(out_of_range)
        def _():
            st_ref[...] = st0_s[...]
            _gla_recur_pairwise(*read, ng_ref, og_s, st_ref, *pair_s)
            project_out()

    @pl.when(j % 2 == 0)
    def _():
        step(bufs[0], bufs[1])

    @pl.when(j % 2 == 1)
    def _():
        step(bufs[1], bufs[0])


def _gla_layer(x, w_main, w_alr, w_a2, b_a2, norm_g, w_out, ln_g, ln_b, w_kv, swa_w_in, swa_w_out):
    bsz, s, _ = x.shape
    t = GLA_BLOCK
    n_blk = s // t
    n_chunks = t // GLA_CHUNK
    rows = lambda n, dt: pltpu.VMEM((t, n), dt)
    proj_bufs = [rows(GLA_DK_TOTAL, BF16), rows(GLA_DK_TOTAL, BF16), rows(GLA_DV_TOTAL, BF16),
                 rows(GLA_DV_TOTAL, BF16), rows(GLA_DK_TOTAL, F32)]
    assert len(proj_bufs) == N_PROJ
    n_all = bsz * n_blk
    xb = x.reshape(n_all, t, D_MODEL)
    cur = pl.BlockSpec((1, t, D_MODEL), lambda j: (jnp.minimum(j, n_all - 1), 0, 0))
    prev = pl.BlockSpec((1, t, D_MODEL), lambda j: (jnp.maximum(j - 1, 0), 0, 0))
    consts = (w_main, w_alr, w_a2, b_a2, norm_g, w_out, ln_g, ln_b)
    wr = D_MODEL // n_all
    assert wr * n_all == D_MODEL and wr % BF16_ROWS == 0
    wrows2 = lambda n: pl.BlockSpec((wr, n), lambda j: (jnp.minimum(j, n_all - 1), 0))
    wrows3 = lambda n: pl.BlockSpec((1, wr, n), lambda j: (0, jnp.minimum(j, n_all - 1), 0))
    n_in = w_kv.shape[1] + swa_w_in.shape[2]
    return pl.pallas_call(
        functools.partial(_gla_layer_kernel, blocks_per_seq=n_blk),
        grid=(n_all + 1,),
        in_specs=[cur, prev] + [_full(a.shape) for a in consts]
        + [wrows2(w_kv.shape[1]), wrows3(swa_w_in.shape[2]), wrows3(D_MODEL)],
        out_specs=[prev, wrows2(n_in), wrows2(D_MODEL)],
        out_shape=[jax.ShapeDtypeStruct(xb.shape, F32), jax.ShapeDtypeStruct((D_MODEL, n_in), BF16),
                   jax.ShapeDtypeStruct((D_MODEL, D_MODEL), BF16)],
        scratch_shapes=[pltpu.VMEM((GLA_HEADS, GLA_DK, GLA_DV), F32)] + proj_bufs + proj_bufs
        + [rows(D_MODEL, BF16)] + [rows(GLA_DK_TOTAL, BF16)] * 4
        + [pltpu.VMEM((n_chunks, GLA_HEADS, GLA_DK, GLA_CHUNK), BF16),
           pltpu.VMEM((n_chunks, GLA_HEADS, GLA_DK, GLA_CHUNK), F32)]
        + [pltpu.VMEM((GLA_CHUNK, GLA_DK_TOTAL), F32)] * 3 + [pltpu.VMEM((GLA_HEADS, GLA_DK, GLA_DV), F32)]
        + [rows(GLA_DV_TOTAL, BF16)],
        compiler_params=_params("arbitrary"),
        name="gla_layer",
    )(xb, xb, *consts, w_kv, swa_w_in, swa_w_out)


def _unit_shape(bsz, seq, d, width):
    return (bsz, seq // SWA_SPAN, SWA_SPAN // (SWA_BLOCK * d), d, SWA_BLOCK, width)


def _tile_unit_spec(d, width):
    rows_per = ROW_TILE // d
    if rows_per >= SWA_BLOCK:
        jb = rows_per // SWA_BLOCK
        return pl.BlockSpec((1, 1, jb, d, SWA_BLOCK, width), lambda b, n, t: (b, n, t, 0, 0, 0))
    per_j = SWA_BLOCK // rows_per
    return pl.BlockSpec((1, 1, 1, d, rows_per, width), lambda b, n, t: (b, n, t // per_j, 0, t % per_j, 0))


def _tile_unit_rows(d):
    rows_per = ROW_TILE // d
    return max(rows_per // SWA_BLOCK, 1), min(rows_per, SWA_BLOCK)


def _token_rows(d, jj, r, rows):
    return pl.ds(jj * SWA_BLOCK * d + r, rows, stride=d) if d > 1 else pl.ds(jj * SWA_BLOCK, rows)


def _rope(h, cos, sin_signed):
    return h * cos + pltpu.roll(h, SWA_HEAD_DIM // 2, 1) * sin_signed


def _swa_in_kernel(x_ref, w_ref, cos_base_ref, sin_base_ref, cos_off_ref, sin_off_ref, *refs):
    k_refs = refs[0:SWA_N_GROUPS]
    v_refs = refs[SWA_N_GROUPS:2 * SWA_N_GROUPS]
    q_refs = refs[2 * SWA_N_GROUPS:3 * SWA_N_GROUPS]
    g_ref = refs[3 * SWA_N_GROUPS]
    tok_s = refs[3 * SWA_N_GROUPS + 1]
    rope_s = refs[3 * SWA_N_GROUPS + 2]
    xb = x_ref[...].astype(BF16)
    e = SWA_HEAD_DIM
    tile = pl.ds(pl.program_id(1) * TILES_PER_SPAN + pl.program_id(2), 1)
    ca, sa = cos_base_ref[tile, :], sin_base_ref[tile, :]
    cb, sb = cos_off_ref[...], sin_off_ref[...]
    cos = ca * cb - sa * sb
    lane = lax.broadcasted_iota(jnp.int32, cb.shape, 1)
    sin = jnp.where(lane < e // 2, -1.0, 1.0) * (sa * cb + ca * sb)
    rope_s[0], rope_s[1] = cos, sin
    rope_s[2], rope_s[3] = cos * (e ** -0.5), sin * (e ** -0.5)
    kv_half = SWA_N_GROUPS * SWA_KV_W
    q0 = 2 * kv_half

    n_slabs = tok_s.shape[0]
    pipeline = {"slab": 0, "pending": None}

    def head_pair(col, out_ref, head0, d, rope):
        y = jnp.dot(xb, w_ref[:, col:col + MXU_COLS], preferred_element_type=F32)
        jb, rows = _tile_unit_rows(d)
        staged = []
        for i in range(MXU_COLS // e):
            h = head0 + i
            yh = y[:, i * e:(i + 1) * e]
            if rope is not None:
                yh = _rope(yh, rope_s[rope], rope_s[rope + 1])
            if d == 1:
                out_ref[0, 0, :, :, :, h * e:(h + 1) * e] = yh.astype(BF16).reshape(jb, d, rows, e)
                continue
            slab = pipeline["slab"]
            pipeline["slab"] = (slab + 1) % n_slabs
            tok_s[slab] = yh
            staged.append((slab, h))

        def scatter():
            for slab, h in staged:
                for jj in range(jb):
                    for r in range(d):
                        out_ref[0, 0, jj, r, :, h * e:(h + 1) * e] = (
                            tok_s[slab, _token_rows(d, jj, r, rows), :].astype(BF16))

        flush()
        pipeline["pending"] = scatter if staged else None

    def flush():
        if pipeline["pending"] is not None:
            pipeline["pending"]()
            pipeline["pending"] = None

    for gi, (_, d) in enumerate(SWA_GROUPS):
        head_pair(gi * SWA_KV_W, k_refs[gi], 0, d, 0)
        head_pair(kv_half + gi * SWA_KV_W, v_refs[gi], 0, d, None)
        for h0 in range(0, SWA_Q_HEADS, MXU_COLS // e):
            head_pair(q0 + gi * SWA_OUT + h0 * e, q_refs[gi], h0, d, 2)
    g0 = q0 + SWA_N_GROUPS * SWA_OUT
    for c in range(0, SWA_OUT, MXU_COLS):
        g_ref[:, c:c + MXU_COLS] = jnp.dot(xb, w_ref[:, g0 + c:g0 + c + MXU_COLS],
                                           preferred_element_type=F32).astype(BF16)
        flush()


def _swa_in_proj(x2, w_all, rope_tables, bsz, seq):
    tm = ROW_TILE
    n_span = seq // SWA_SPAN
    tile = lambda b, n, t: (b * n_span + n) * TILES_PER_SPAN + t
    row = lambda w: pl.BlockSpec((tm, w), lambda b, n, t: (tile(b, n, t), 0))
    dils = [d for _, d in SWA_GROUPS]
    unit_out = [(d, SWA_KV_W) for d in dils] * 2 + [(d, SWA_OUT) for d in dils]
    outs = pl.pallas_call(
        _swa_in_kernel,
        grid=(bsz, n_span, TILES_PER_SPAN),
        in_specs=[row(D_MODEL), _full(w_all.shape)] + [_full(a.shape) for a in rope_tables],
        out_specs=[_tile_unit_spec(d, w) for d, w in unit_out] + [row(SWA_OUT)],
        out_shape=[jax.ShapeDtypeStruct(_unit_shape(bsz, seq, d, w), BF16) for d, w in unit_out]
        + [jax.ShapeDtypeStruct((bsz * seq, SWA_OUT), BF16)],
        scratch_shapes=[pltpu.VMEM((SWA_OUT // LANES, tm, LANES), F32), pltpu.VMEM((4, tm, SWA_HEAD_DIM), F32)],
        compiler_params=_params("parallel", "parallel", "parallel"),
        name="swa_in_proj",
    )(x2, w_all, *rope_tables)
    return outs[0:3], outs[3:6], outs[6:9], outs[9]


def _swa_kernel(q_ref, kc_ref, kp_ref, vc_ref, vp_ref, o_ref, st_ref, k_s, v_s, bias_s):
    blk = SWA_BLOCK
    e = SWA_HEAD_DIM
    n_j, d = q_ref.shape[2], q_ref.shape[3]
    n = pl.program_id(1)

    k_s[0] = kp_ref[0, 0, 0]
    k_s[1:] = kc_ref[0, 0]
    v_s[0] = vp_ref[0, 0, 0]
    v_s[1:] = vc_ref[0, 0]

    qi = lax.broadcasted_iota(jnp.int32, (blk, 2 * blk), 0)
    kj = lax.broadcasted_iota(jnp.int32, (blk, 2 * blk), 1)
    rel = blk + qi - kj
    band = (rel >= 0) & (rel <= blk)
    bias_s[0] = jnp.where(band, 0.0, -jnp.inf)
    bias_s[1] = jnp.where(band & (kj >= blk), 0.0, -jnp.inf)
    lane_slot = lax.broadcasted_iota(jnp.int32, (blk, LANES), 1) // SWA_STAT_LANES
    ones = jnp.ones((2 * blk, LANES), BF16)

    def unit(idx, carry):
        j = idx // d
        r = idx % d
        bias = bias_s[jnp.where((j == 0) & (n == 0), 1, 0)]
        st_tile = jnp.zeros((blk, LANES), F32)
        for kvh in range(SWA_KV_HEADS):
            q4 = q_ref[0, 0, j, r, :, kvh * SWA_REP * e:(kvh + 1) * SWA_REP * e]
            qs = jnp.concatenate([q4[:, h * e:(h + 1) * e] for h in range(SWA_REP)], axis=0)
            kcols = slice(kvh * e, (kvh + 1) * e)
            kk = jnp.concatenate([k_s[j, r, :, kcols], k_s[j + 1, r, :, kcols]], axis=0)
            vv = jnp.concatenate([v_s[j, r, :, kcols], v_s[j + 1, r, :, kcols]], axis=0)
            s = lax.dot_general(qs, kk, _NT, preferred_element_type=F32)
            s = (s.reshape(SWA_REP, blk, 2 * blk) + bias).reshape(SWA_REP * blk, 2 * blk)
            m = jnp.max(s, axis=-1, keepdims=True)
            p = jnp.exp(s - m).astype(BF16)
            o = jnp.dot(p, jnp.concatenate([vv, ones], axis=1), preferred_element_type=F32)
            for h in range(SWA_REP):
                head = kvh * SWA_REP + h
                o_ref[0, 0, j, r, :, head * e:(head + 1) * e] = o[h * blk:(h + 1) * blk, 0:e].astype(BF16)
                st_tile = jnp.where(lane_slot == 2 * head, m[h * blk:(h + 1) * blk], st_tile)
                st_tile = jnp.where(lane_slot == 2 * head + 1, o[h * blk:(h + 1) * blk, e:], st_tile)
        st_ref[0, 0, j, r] = st_tile
        return carry

    lax.fori_loop(0, n_j * d, unit, 0, unroll=True)


def _swa_attention(q, k, v):
    bsz, n_span, n_j, d, blk, _ = q.shape
    cur = lambda w: pl.BlockSpec((1, 1, n_j, d, blk, w), lambda b, n: (b, n, 0, 0, 0, 0))
    prev = pl.BlockSpec((1, 1, 1, d, blk, SWA_KV_W), lambda b, n: (b, jnp.maximum(n - 1, 0), n_j - 1, 0, 0, 0))
    stat_shape = q.shape[:-1] + (LANES,)
    return pl.pallas_call(
        _swa_kernel,
        grid=(bsz, n_span),
        in_specs=[cur(SWA_OUT), cur(SWA_KV_W), prev, cur(SWA_KV_W), prev],
        out_specs=[cur(SWA_OUT), cur(LANES)],
        out_shape=[jax.ShapeDtypeStruct(q.shape, BF16), jax.ShapeDtypeStruct(stat_shape, F32)],
        scratch_shapes=[pltpu.VMEM((n_j + 1, d, blk, SWA_KV_W), BF16),
                        pltpu.VMEM((n_j + 1, d, blk, SWA_KV_W), BF16),
                        pltpu.VMEM((2, blk, 2 * blk), F32)],
        compiler_params=_params("parallel", "parallel"),
        name=f"swa_attention_d{d}",
    )(q, k, k, v, v)


def _merge_out_kernel(*refs):
    n_g = SWA_N_GROUPS
    o_refs, st_refs = refs[0:n_g], refs[n_g:2 * n_g]
    g_ref, x_ref, w_ref, lg_ref, lb_ref, out_ref, stat_s, o_s = refs[2 * n_g:]
    tm = ROW_TILE

    def to_tokens(ref, d, dst):
        jb, rows = _tile_unit_rows(d)
        n_slabs = ref.shape[-1] // LANES
        lanes = lambda c: slice(c * LANES, (c + 1) * LANES)
        if d == 1:
            tok = ref[0, 0].reshape(tm, ref.shape[-1]).astype(F32)
            return [tok[:, lanes(c)] for c in range(n_slabs)]
        for jj in range(jb):
            for r in range(d):
                for c in range(n_slabs):
                    dst[c, _token_rows(d, jj, r, rows), :] = ref[0, 0, jj, r, :, lanes(c)].astype(F32)
        return [dst[c] for c in range(n_slabs)]

    dils = [d for _, d in SWA_GROUPS]
    sts = [to_tokens(st_refs[gi], d, stat_s.at[gi:gi + 1])[0] for gi, d in enumerate(dils)]
    mx = functools.reduce(jnp.maximum, sts)
    es = [jnp.exp(st - mx) for st in sts]
    ls = [pltpu.roll(st, LANES - SWA_STAT_LANES, 1) for st in sts]
    tot = functools.reduce(jnp.add, [eg * lg for eg, lg in zip(es, ls)])
    m_slot = (lax.broadcasted_iota(jnp.int32, (tm, LANES), 1) // SWA_STAT_LANES) % 2 == 0
    src = lax.broadcasted_iota(jnp.int32, (LANES, SWA_OUT), 0)
    dst = lax.broadcasted_iota(jnp.int32, (LANES, SWA_OUT), 1)
    expand = jnp.where(src == (dst // SWA_HEAD_DIM) * 2 * SWA_STAT_LANES, 1.0, 0.0).astype(BF16)
    acc = None
    for gi, d in enumerate(dils):
        w = jnp.dot(jnp.where(m_slot, es[gi] / tot, 0.0).astype(BF16), expand, preferred_element_type=F32)
        term = w * jnp.concatenate(to_tokens(o_refs[gi], d, o_s), axis=1)
        acc = term if acc is None else acc + term
    a = acc.astype(BF16) * _silu(g_ref[...])
    y = jnp.dot(a, w_ref[...], preferred_element_type=F32)
    out_ref[...] = _layer_norm(DEEPNORM_ALPHA * x_ref[...] + y, lg_ref[1:2, :], lb_ref[1:2, :])


def _merge_out_ln(os, sts, g, x2, w_out, ln_g, ln_b, bsz, seq):
    tm = ROW_TILE
    n_span = seq // SWA_SPAN
    row = lambda w: pl.BlockSpec((tm, w), lambda b, n, t: ((b * n_span + n) * TILES_PER_SPAN + t, 0))
    dils = [d for _, d in SWA_GROUPS]
    return pl.pallas_call(
        _merge_out_kernel,
        grid=(bsz, n_span, TILES_PER_SPAN),
        in_specs=[_tile_unit_spec(d, SWA_OUT) for d in dils] + [_tile_unit_spec(d, LANES) for d in dils]
        + [row(SWA_OUT), row(D_MODEL), _full(w_out.shape), _full(ln_g.shape), _full(ln_b.shape)],
        out_specs=row(D_MODEL),
        out_shape=jax.ShapeDtypeStruct((bsz * seq, D_MODEL), F32),
        scratch_shapes=[pltpu.VMEM((SWA_N_GROUPS, tm, LANES), F32),
                        pltpu.VMEM((SWA_OUT // LANES, tm, LANES), F32)],
        compiler_params=_params("parallel", "parallel", "parallel"),
        name="merge_out_ln",
    )(*os, *sts, g, x2, w_out, ln_g, ln_b)


def _rope_tables(seq):
    half = SWA_HEAD_DIM // 2
    inv = ROPE_THETA ** (-(jnp.arange(half, dtype=F32) * 2.0) / SWA_HEAD_DIM)
    inv = jnp.concatenate([inv, inv])[None, :]
    base = (jnp.arange(seq // ROW_TILE) * ROW_TILE).astype(F32)[:, None] * inv
    off = jnp.arange(ROW_TILE).astype(F32)[:, None] * inv
    return jnp.cos(base), jnp.sin(base), jnp.cos(off), jnp.sin(off)


GLA_N_MAIN = 2 * GLA_DK_TOTAL + 2 * GLA_DV_TOTAL
CAST_ROWS = 128


def _cast_weights_kernel(gw_in_t, gw_a2, gw_out, w_main, w_alr, w_a2, w_out0):
    r = CAST_ROWS
    t = gw_in_t[0]
    for c in range(0, GLA_N_MAIN, r):
        w_main[:, c:c + r] = t[c:c + r, :].T.astype(BF16)
    tail = t[GLA_N_MAIN + GLA_GATE_RANK - r:, :].T
    lane = lax.broadcasted_iota(jnp.int32, (r, r), 1)
    w_alr[...] = jnp.where(lane >= r - GLA_GATE_RANK, tail, 0.0).astype(BF16)
    w_a2[...] = jnp.zeros_like(w_a2)
    w_a2[LANES - GLA_GATE_RANK:, :] = gw_a2[0].astype(BF16)
    w_out0[...] = gw_out[0].astype(BF16)


def _cast_weights(gla_w_in, gla_w_a2, gla_w_out):
    r = CAST_ROWS
    assert r == LANES
    rows3 = lambda n: pl.BlockSpec((1, r, n), lambda i: (0, i, 0))
    rows2 = lambda n: pl.BlockSpec((r, n), lambda i: (i, 0))
    gla_w_in_t = jnp.swapaxes(gla_w_in, 1, 2)
    cols_t = pl.BlockSpec((1, gla_w_in_t.shape[1], r), lambda i: (0, 0, i))
    out_widths = [GLA_N_MAIN, LANES, None, D_MODEL]
    out_shapes = [(D_MODEL, w) if w else (LANES, GLA_DK_TOTAL) for w in out_widths]
    return pl.pallas_call(
        _cast_weights_kernel,
        grid=(D_MODEL // r,),
        in_specs=[cols_t, _full(gla_w_a2.shape), rows3(D_MODEL)],
        out_specs=[rows2(w) if w else pl.BlockSpec((LANES, GLA_DK_TOTAL), lambda i: (0, 0)) for w in out_widths],
        out_shape=[jax.ShapeDtypeStruct(s, BF16) for s in out_shapes],
        compiler_params=_params("arbitrary"),
        name="cast_weights",
    )(gla_w_in_t, gla_w_a2, gla_w_out)


def kernel(x, gla_w_in, gla_w_a2, gla_b_a2, gla_norm_g, gla_w_out, w_kv, swa_w_in, swa_w_out, ln_g, ln_b):
    bsz, seq, _ = x.shape
    assert seq % SWA_SPAN == 0 and seq % GLA_BLOCK == 0
    assert gla_w_in.shape[0] == 1 and swa_w_in.shape[0] == 1 and ln_g.shape[0] == DEPTH
    m = bsz * seq

    w_main, w_alr, w_a2, w_out0 = _cast_weights(gla_w_in, gla_w_a2, gla_w_out)
    x2, w_all, w_out1 = _gla_layer(x, w_main, w_alr, w_a2, gla_b_a2, gla_norm_g, w_out0, ln_g, ln_b,
                                   w_kv, swa_w_in, swa_w_out)
    x2 = x2.reshape(m, D_MODEL)

    ks, vs, qs, g = _swa_in_proj(x2, w_all, _rope_tables(seq), bsz, seq)
    os, sts = zip(*[_swa_attention(qs[gi], ks[gi], vs[gi]) for gi in range(SWA_N_GROUPS)])
    out = _merge_out_ln(os, sts, g, x2, w_out1, ln_g, ln_b, bsz, seq)
    return out.reshape(bsz, seq, D_MODEL)
```

```python
import functools

import jax
import jax.numpy as jnp
from jax import lax
from jax.experimental import pallas as pl
from jax.experimental.pallas import tpu as pltpu

BF16 = jnp.bfloat16
F32 = jnp.float32

D_MODEL = 1024
DEPTH = 2
DEEPNORM_ALPHA = (2.0 * DEPTH) ** 0.25
LN_EPS = 1e-5
RMS_EPS = 1e-5

GLA_HEADS = 4
GLA_DK = 128
GLA_DV = 256
GLA_DK_TOTAL = GLA_HEADS * GLA_DK
GLA_DV_TOTAL = GLA_HEADS * GLA_DV
GLA_GATE_RANK = 16
GLA_GATE_TAU = 16.0
GLA_SUB = 64
GLA_CHUNK = 2 * GLA_SUB
GLA_BLOCK = 512
GLA_HEAD_GROUP = 4
GLA_FAST_MIN_LOG_DECAY = -40.0

SWA_GROUPS = ((128, 1), (512, 4), (2048, 16))
SWA_N_GROUPS = len(SWA_GROUPS)
SWA_HEAD_DIM = 128
SWA_Q_HEADS = 8
SWA_KV_HEADS = 2
SWA_REP = SWA_Q_HEADS // SWA_KV_HEADS
SWA_OUT = SWA_Q_HEADS * SWA_HEAD_DIM
SWA_KV_W = SWA_KV_HEADS * SWA_HEAD_DIM
SWA_KV_TOTAL = 2 * SWA_N_GROUPS * SWA_KV_W
SWA_BLOCK = 128
SWA_SPAN = 2048
SWA_STAT_LANES = SWA_HEAD_DIM // (2 * SWA_Q_HEADS)
ROPE_THETA = 10000.0

LANES = 128
BF16_ROWS = 16
F32_ROWS = 8
MXU_COLS = 256
ROW_TILE = 1024
TILES_PER_SPAN = SWA_SPAN // ROW_TILE
VMEM_LIMIT = 56 * 1024 * 1024

_NT = (((1,), (1,)), ((), ()))


def _params(*sem, flags=None):
    return pltpu.CompilerParams(dimension_semantics=sem, vmem_limit_bytes=VMEM_LIMIT, flags=flags)


def _full(shape):
    return pl.BlockSpec(shape, lambda *_: (0,) * len(shape), pipeline_mode=pl.Buffered(1))


def _layer_norm(h, g, b):
    mu = jnp.mean(h, axis=-1, keepdims=True)
    hc = h - mu
    var = jnp.mean(hc * hc, axis=-1, keepdims=True)
    return hc * lax.rsqrt(var + LN_EPS) * g + b


def _silu(g):
    return g * (0.5 + 0.5 * jnp.tanh(0.5 * g))


def _gla_project_pieces(x_ref, w_ref, walr_ref, wa2_ref, ba2_ref, xb_s, q_ref, k_ref, v_ref, g_ref, la_ref):
    xb_s[...] = x_ref[0].astype(BF16)

    def tile(dst, col, lo, scale):
        def piece():
            y = jnp.dot(xb_s[...], w_ref[:, col + lo:col + lo + MXU_COLS], preferred_element_type=F32)
            dst[:, lo:lo + MXU_COLS] = (y if scale is None else y * scale).astype(BF16)
        return piece

    def gate_piece():
        a_lr = jnp.dot(xb_s[...], walr_ref[...], preferred_element_type=F32)
        z = jnp.dot(a_lr.astype(BF16), wa2_ref[...], preferred_element_type=F32) + ba2_ref[...]
        log_sig = jnp.minimum(z, 0.0) - jnp.log(1.0 + jnp.exp(-jnp.abs(z)))
        la_ref[...] = log_sig * (1.0 / GLA_GATE_TAU)

    pieces, col = [gate_piece], 0
    for dst, scale in ((q_ref, GLA_DK ** -0.5), (k_ref, None), (v_ref, None), (g_ref, None)):
        pieces += [tile(dst, col, lo, scale) for lo in range(0, dst.shape[1], MXU_COLS)]
        col += dst.shape[1]
    return pieces


def _gla_recur_pieces(q_ref, k_ref, v_ref, g_ref, la_ref, ng_ref, o_ref,
                      st_ref, qe_s, qi_s, kd_s, kx_s, klt_s, gam_s):
    t = la_ref.shape[0]
    c, sub = GLA_CHUNK, GLA_SUB

    row = lax.broadcasted_iota(jnp.int32, (c, c), 0)
    col = lax.broadcasted_iota(jnp.int32, (c, c), 1)
    causal = col <= row
    tri = jnp.where(causal & (row // sub == col // sub), 1.0, 0.0).astype(BF16)
    halves = lambda first, second: jnp.concatenate([first, second], axis=0)

    for ic in range(t // c):
        rows = slice(ic * c, (ic + 1) * c)
        first = slice(ic * c, ic * c + sub)
        second = slice(ic * c + sub, (ic + 1) * c)

        la = la_ref[rows, :]
        la_hi = la.astype(BF16)
        la_lo = (la - la_hi.astype(F32)).astype(BF16)
        bs = jnp.dot(tri, la_hi, preferred_element_type=F32) + jnp.dot(tri, la_lo, preferred_element_type=F32)
        e_a = jnp.exp(bs[sub - 1:sub, :])
        e_b = jnp.exp(bs[c - 1:c, :])
        qi = q_ref[rows, :].astype(F32) * jnp.exp(bs)
        kd = k_ref[rows, :].astype(F32) * jnp.exp(-bs)
        kl_a, kl_b = kd[:sub] * e_a, kd[sub:] * e_b
        qi_s[rows, :] = qi.astype(BF16)
        kd_s[rows, :] = kd.astype(BF16)
        qe_s[rows, :] = halves(qi[:sub], qi[sub:] * e_a).astype(BF16)
        kx_s[rows, :] = halves(kl_a, kd[sub:]).astype(BF16)
        klc = halves(kl_a * e_b, kl_b)
        gam = e_a * e_b
        for h in range(GLA_HEADS):
            kcols = slice(h * GLA_DK, (h + 1) * GLA_DK)
            klt_s[ic, h] = klc[:, kcols].T.astype(BF16)
            gam_s[ic, h] = jnp.broadcast_to(gam[:, kcols], (c, GLA_DK)).T
        yield

        for h0 in range(0, GLA_HEADS, GLA_HEAD_GROUP):
            stage1 = []
            for h in range(h0, h0 + GLA_HEAD_GROUP):
                kcols = slice(h * GLA_DK, (h + 1) * GLA_DK)
                st = st_ref[h]
                o_inter = jnp.dot(qe_s[rows, kcols], st.astype(BF16), preferred_element_type=F32)
                att = jnp.concatenate(
                    [lax.dot_general(qi_s[first, kcols], kd_s[rows, kcols], _NT, preferred_element_type=F32),
                     lax.dot_general(qi_s[second, kcols], kx_s[rows, kcols], _NT, preferred_element_type=F32)],
                    axis=0)
                stage1.append((st, o_inter, att))
            yield
            for h, (st, o_inter, att) in zip(range(h0, h0 + GLA_HEAD_GROUP), stage1):
                vcols = slice(h * GLA_DV, (h + 1) * GLA_DV)
                att = jnp.where(causal, att, 0.0).astype(BF16)
                both = jnp.dot(jnp.concatenate([att, klt_s[ic, h]], axis=0), v_ref[rows, vcols],
                               preferred_element_type=F32)
                gam_t = gam_s[ic, h]
                st_ref[h] = st * jnp.concatenate([gam_t] * (GLA_DV // GLA_DK), axis=1) + both[c:]
                o = o_inter + both[0:c]
                o = o * lax.rsqrt(jnp.mean(o * o, axis=-1, keepdims=True) + RMS_EPS) * ng_ref[:, vcols]
                o_ref[rows, vcols] = o.astype(BF16) * _silu(g_ref[rows, vcols])
            yield


def _pairwise_scores(b_s, qf_s, kf_s, kcols):
    c = GLA_CHUNK
    bj = b_s[:, kcols]
    kj = kf_s[:, kcols]
    j_idx = lax.broadcasted_iota(jnp.int32, (c, 1), 0)
    lane = lax.broadcasted_iota(jnp.int32, (c, c), 1)

    def query_rows(g, att_t):
        base = pl.multiple_of(g * F32_ROWS, F32_ROWS)
        b_g = b_s[pl.ds(base, F32_ROWS), kcols]
        q_g = qf_s[pl.ds(base, F32_ROWS), kcols]
        for r in range(F32_ROWS):
            i = g * F32_ROWS + r
            w = jnp.exp(jnp.minimum(b_g[r:r + 1] - bj, 0.0)) * kj * q_g[r:r + 1]
            col = jnp.sum(w, axis=-1, keepdims=True)
            att_t = jnp.where(lane == i, jnp.where(j_idx <= i, col, 0.0), att_t)
        return att_t

    return lax.fori_loop(0, c // F32_ROWS, query_rows, jnp.zeros((c, c), F32)).T


def _gla_recur_pairwise(q_ref, k_ref, v_ref, g_ref, la_ref, ng_ref, o_ref, st_ref, b_s, qf_s, kf_s):
    c, sub = GLA_CHUNK, GLA_SUB
    row = lax.broadcasted_iota(jnp.int32, (c, c), 0)
    col = lax.broadcasted_iota(jnp.int32, (c, c), 1)
    tri = jnp.where((col <= row) & (row // sub == col // sub), 1.0, 0.0).astype(BF16)

    def chunk(ic, carry):
        rows = pl.ds(pl.multiple_of(ic * c, c), c)
        la = la_ref[rows, :]
        la_hi = la.astype(BF16)
        la_lo = (la - la_hi.astype(F32)).astype(BF16)
        bs = jnp.dot(tri, la_hi, preferred_element_type=F32) + jnp.dot(tri, la_lo, preferred_element_type=F32)
        b = jnp.concatenate([bs[:sub], bs[sub:] + bs[sub - 1:sub, :]], axis=0)
        qf, kf = q_ref[rows, :].astype(F32), k_ref[rows, :].astype(F32)
        b_s[...], qf_s[...], kf_s[...] = b, qf, kf
        qe = (qf * jnp.exp(b)).astype(BF16)
        kl = kf * jnp.exp(b[c - 1:c, :] - b)
        gam = jnp.exp(b[c - 1:c, :])
        for h in range(GLA_HEADS):
            kcols = slice(h * GLA_DK, (h + 1) * GLA_DK)
            vcols = slice(h * GLA_DV, (h + 1) * GLA_DV)
            st = st_ref[h]
            att = _pairwise_scores(b_s, qf_s, kf_s, kcols).astype(BF16)
            o_inter = jnp.dot(qe[:, kcols], st.astype(BF16), preferred_element_type=F32)
            both = jnp.dot(jnp.concatenate([att, kl[:, kcols].T.astype(BF16)], axis=0), v_ref[rows, vcols],
                           preferred_element_type=F32)
            gam_t = jnp.broadcast_to(gam[:, kcols], (c, GLA_DK)).T
            st_ref[h] = st * jnp.concatenate([gam_t] * (GLA_DV // GLA_DK), axis=1) + both[c:]
            o = o_inter + both[0:c]
            o = o * lax.rsqrt(jnp.mean(o * o, axis=-1, keepdims=True) + RMS_EPS) * ng_ref[:, vcols]
            o_ref[rows, vcols] = o.astype(BF16) * _silu(g_ref[rows, vcols])
        return carry

    lax.fori_loop(0, la_ref.shape[0] // c, chunk, 0)


N_PROJ = 5


def _gla_layer_kernel(xp_ref, xr_ref, w_ref, walr_ref, wa2_ref, ba2_ref, ng_ref, wout_ref, lg_ref, lb_ref,
                      wkv_ref, swin_ref, swout_ref, out_ref, wall_ref, wout1_ref, st_ref, *scratch,
                      blocks_per_seq):
    bufs = (scratch[0:N_PROJ], scratch[N_PROJ:2 * N_PROJ])
    xb_s, rec_s = scratch[2 * N_PROJ], scratch[2 * N_PROJ + 1:-5]
    pair_s, st0_s, og_s = scratch[-5:-2], scratch[-2], scratch[-1]
    j = pl.program_id(0)
    wall_ref[:, 0:SWA_KV_TOTAL] = wkv_ref[...].astype(BF16)
    wall_ref[:, SWA_KV_TOTAL:] = swin_ref[0].astype(BF16)
    wout1_ref[...] = swout_ref[0].astype(BF16)

    @pl.when(j == 0)
    def _():
        for ref in bufs[1]:
            ref[...] = jnp.zeros_like(ref)

    @pl.when((j == 0) | (j % blocks_per_seq == 1 % blocks_per_seq))
    def _():
        st_ref[...] = jnp.zeros_like(st_ref)

    def step(write, read):
        totals = jnp.sum(read[N_PROJ - 1][...].reshape(-1, GLA_SUB, GLA_DK_TOTAL), axis=1)
        out_of_range = jnp.min(totals) < GLA_FAST_MIN_LOG_DECAY
        st0_s[...] = st_ref[...]

        pieces = _gla_project_pieces(xp_ref, w_ref, walr_ref, wa2_ref, ba2_ref, xb_s, *write)
        n_pieces = len(pieces)
        n_stages = (xb_s.shape[0] // GLA_CHUNK) * (1 + 2 * (GLA_HEADS // GLA_HEAD_GROUP))
        for i, _ in enumerate(_gla_recur_pieces(*read, ng_ref, og_s, st_ref, *rec_s)):
            issued_after = ((i + 1) * n_pieces + n_stages - 1) // n_stages
            while n_pieces - len(pieces) < issued_after:
                pieces.pop(0)()

        def project_out():
            y = jnp.dot(og_s[...], wout_ref[...], preferred_element_type=F32)
            out_ref[0] = _layer_norm(DEEPNORM_ALPHA * xr_ref[0] + y, lg_ref[0:1, :], lb_ref[0:1, :])

        project_out()

        @pl.when(out_of_range)
        def _():
            st_ref[...] = st0_s[...]
            _gla_recur_pairwise(*read, ng_ref, og_s, st_ref, *pair_s)
            project_out()

    @pl.when(j % 2 == 0)
    def _():
        step(bufs[0], bufs[1])

    @pl.when(j % 2 == 1)
    def _():
        step(bufs[1], bufs[0])


def _gla_layer(x, w_main, w_alr, w_a2, b_a2, norm_g, w_out, ln_g, ln_b, w_kv, swa_w_in, swa_w_out):
    bsz, s, _ = x.shape
    t = GLA_BLOCK
    n_blk = s // t
    n_chunks = t // GLA_CHUNK
    rows = lambda n, dt: pltpu.VMEM((t, n), dt)
    proj_bufs = [rows(GLA_DK_TOTAL, BF16), rows(GLA_DK_TOTAL, BF16), rows(GLA_DV_TOTAL, BF16),
                 rows(GLA_DV_TOTAL, BF16), rows(GLA_DK_TOTAL, F32)]
    assert len(proj_bufs) == N_PROJ
    n_all = bsz * n_blk
    xb = x.reshape(n_all, t, D_MODEL)
    cur = pl.BlockSpec((1, t, D_MODEL), lambda j: (jnp.minimum(j, n_all - 1), 0, 0))
    prev = pl.BlockSpec((1, t, D_MODEL), lambda j: (jnp.maximum(j - 1, 0), 0, 0))
    consts = (w_main, w_alr, w_a2, b_a2, norm_g, w_out, ln_g, ln_b)
    wr = D_MODEL // n_all
    assert wr * n_all == D_MODEL and wr % BF16_ROWS == 0
    wrows2 = lambda n: pl.BlockSpec((wr, n), lambda j: (jnp.minimum(j, n_all - 1), 0))
    wrows3 = lambda n: pl.BlockSpec((1, wr, n), lambda j: (0, jnp.minimum(j, n_all - 1), 0))
    n_in = w_kv.shape[1] + swa_w_in.shape[2]
    return pl.pallas_call(
        functools.partial(_gla_layer_kernel, blocks_per_seq=n_blk),
        grid=(n_all + 1,),
        in_specs=[cur, prev] + [_full(a.shape) for a in consts]
        + [wrows2(w_kv.shape[1]), wrows3(swa_w_in.shape[2]), wrows3(D_MODEL)],
        out_specs=[prev, wrows2(n_in), wrows2(D_MODEL)],
        out_shape=[jax.ShapeDtypeStruct(xb.shape, F32), jax.ShapeDtypeStruct((D_MODEL, n_in), BF16),
                   jax.ShapeDtypeStruct((D_MODEL, D_MODEL), BF16)],
        scratch_shapes=[pltpu.VMEM((GLA_HEADS, GLA_DK, GLA_DV), F32)] + proj_bufs + proj_bufs
        + [rows(D_MODEL, BF16)] + [rows(GLA_DK_TOTAL, BF16)] * 4
        + [pltpu.VMEM((n_chunks, GLA_HEADS, GLA_DK, GLA_CHUNK), BF16),
           pltpu.VMEM((n_chunks, GLA_HEADS, GLA_DK, GLA_CHUNK), F32)]
        + [pltpu.VMEM((GLA_CHUNK, GLA_DK_TOTAL), F32)] * 3 + [pltpu.VMEM((GLA_HEADS, GLA_DK, GLA_DV), F32)]
        + [rows(GLA_DV_TOTAL, BF16)],
        compiler_params=_params("arbitrary"),
        name="gla_layer",
    )(xb, xb, *consts, w_kv, swa_w_in, swa_w_out)


def _unit_shape(bsz, seq, d, width):
    return (bsz, seq // SWA_SPAN, SWA_SPAN // (SWA_BLOCK * d), d, SWA_BLOCK, width)


def _tile_unit_spec(d, width):
    rows_per = ROW_TILE // d
    if rows_per >= SWA_BLOCK:
        jb = rows_per // SWA_BLOCK
        return pl.BlockSpec((1, 1, jb, d, SWA_BLOCK, width), lambda b, n, t: (b, n, t, 0, 0, 0))
    per_j = SWA_BLOCK // rows_per
    return pl.BlockSpec((1, 1, 1, d, rows_per, width), lambda b, n, t: (b, n, t // per_j, 0, t % per_j, 0))


def _tile_unit_rows(d):
    rows_per = ROW_TILE // d
    return max(rows_per // SWA_BLOCK, 1), min(rows_per, SWA_BLOCK)


def _token_rows(d, jj, r, rows):
    return pl.ds(jj * SWA_BLOCK * d + r, rows, stride=d) if d > 1 else pl.ds(jj * SWA_BLOCK, rows)


def _rope(h, cos, sin_signed):
    return h * cos + pltpu.roll(h, SWA_HEAD_DIM // 2, 1) * sin_signed


def _swa_in_kernel(x_ref, w_ref, cos_base_ref, sin_base_ref, cos_off_ref, sin_off_ref, *refs):
    k_refs = refs[0:SWA_N_GROUPS]
    v_refs = refs[SWA_N_GROUPS:2 * SWA_N_GROUPS]
    q_refs = refs[2 * SWA_N_GROUPS:3 * SWA_N_GROUPS]
    g_ref = refs[3 * SWA_N_GROUPS]
    tok_s = refs[3 * SWA_N_GROUPS + 1]
    rope_s = refs[3 * SWA_N_GROUPS + 2]
    xb = x_ref[...].astype(BF16)
    e = SWA_HEAD_DIM
    tile = pl.ds(pl.program_id(1) * TILES_PER_SPAN + pl.program_id(2), 1)
    ca, sa = cos_base_ref[tile, :], sin_base_ref[tile, :]
    cb, sb = cos_off_ref[...], sin_off_ref[...]
    cos = ca * cb - sa * sb
    lane = lax.broadcasted_iota(jnp.int32, cb.shape, 1)
    sin = jnp.where(lane < e // 2, -1.0, 1.0) * (sa * cb + ca * sb)
    rope_s[0], rope_s[1] = cos, sin
    rope_s[2], rope_s[3] = cos * (e ** -0.5), sin * (e ** -0.5)
    kv_half = SWA_N_GROUPS * SWA_KV_W
    q0 = 2 * kv_half

    n_slabs = tok_s.shape[0]
    pipeline = {"slab": 0, "pending": None}

    def head_pair(col, out_ref, head0, d, rope):
        y = jnp.dot(xb, w_ref[:, col:col + MXU_COLS], preferred_element_type=F32)
        jb, rows = _tile_unit_rows(d)
        staged = []
        for i in range(MXU_COLS // e):
            h = head0 + i
            yh = y[:, i * e:(i + 1) * e]
            if rope is not None:
                yh = _rope(yh, rope_s[rope], rope_s[rope + 1])
            if d == 1:
                out_ref[0, 0, :, :, :, h * e:(h + 1) * e] = yh.astype(BF16).reshape(jb, d, rows, e)
                continue
            slab = pipeline["slab"]
            pipeline["slab"] = (slab + 1) % n_slabs
            tok_s[slab] = yh
            staged.append((slab, h))

        def scatter():
            for slab, h in staged:
                for jj in range(jb):
                    for r in range(d):
                        out_ref[0, 0, jj, r, :, h * e:(h + 1) * e] = (
                            tok_s[slab, _token_rows(d, jj, r, rows), :].astype(BF16))

        flush()
        pipeline["pending"] = scatter if staged else None

    def flush():
        if pipeline["pending"] is not None:
            pipeline["pending"]()
            pipeline["pending"] = None

    for gi, (_, d) in enumerate(SWA_GROUPS):
        head_pair(gi * SWA_KV_W, k_refs[gi], 0, d, 0)
        head_pair(kv_half + gi * SWA_KV_W, v_refs[gi], 0, d, None)
        for h0 in range(0, SWA_Q_HEADS, MXU_COLS // e):
            head_pair(q0 + gi * SWA_OUT + h0 * e, q_refs[gi], h0, d, 2)
    g0 = q0 + SWA_N_GROUPS * SWA_OUT
    for c in range(0, SWA_OUT, MXU_COLS):
        g_ref[:, c:c + MXU_COLS] = jnp.dot(xb, w_ref[:, g0 + c:g0 + c + MXU_COLS],
                                           preferred_element_type=F32).astype(BF16)
        flush()


def _swa_in_proj(x2, w_all, rope_tables, bsz, seq):
    tm = ROW_TILE
    n_span = seq // SWA_SPAN
    tile = lambda b, n, t: (b * n_span + n) * TILES_PER_SPAN + t
    row = lambda w: pl.BlockSpec((tm, w), lambda b, n, t: (tile(b, n, t), 0))
    dils = [d for _, d in SWA_GROUPS]
    unit_out = [(d, SWA_KV_W) for d in dils] * 2 + [(d, SWA_OUT) for d in dils]
    outs = pl.pallas_call(
        _swa_in_kernel,
        grid=(bsz, n_span, TILES_PER_SPAN),
        in_specs=[row(D_MODEL), _full(w_all.shape)] + [_full(a.shape) for a in rope_tables],
        out_specs=[_tile_unit_spec(d, w) for d, w in unit_out] + [row(SWA_OUT)],
        out_shape=[jax.ShapeDtypeStruct(_unit_shape(bsz, seq, d, w), BF16) for d, w in unit_out]
        + [jax.ShapeDtypeStruct((bsz * seq, SWA_OUT), BF16)],
        scratch_shapes=[pltpu.VMEM((SWA_OUT // LANES, tm, LANES), F32), pltpu.VMEM((4, tm, SWA_HEAD_DIM), F32)],
        compiler_params=_params("parallel", "parallel", "parallel"),
        name="swa_in_proj",
    )(x2, w_all, *rope_tables)
    return outs[0:3], outs[3:6], outs[6:9], outs[9]


def _swa_kernel(q_ref, kc_ref, kp_ref, vc_ref, vp_ref, o_ref, st_ref, k_s, v_s, bias_s):
    blk = SWA_BLOCK
    e = SWA_HEAD_DIM
    n_j, d = q_ref.shape[2], q_ref.shape[3]
    n = pl.program_id(1)

    k_s[0] = kp_ref[0, 0, 0]
    k_s[1:] = kc_ref[0, 0]
    v_s[0] = vp_ref[0, 0, 0]
    v_s[1:] = vc_ref[0, 0]

    qi = lax.broadcasted_iota(jnp.int32, (blk, 2 * blk), 0)
    kj = lax.broadcasted_iota(jnp.int32, (blk, 2 * blk), 1)
    rel = blk + qi - kj
    band = (rel >= 0) & (rel <= blk)
    bias_s[0] = jnp.where(band, 0.0, -jnp.inf)
    bias_s[1] = jnp.where(band & (kj >= blk), 0.0, -jnp.inf)
    lane_slot = lax.broadcasted_iota(jnp.int32, (blk, LANES), 1) // SWA_STAT_LANES
    ones = jnp.ones((2 * blk, LANES), BF16)

    def unit(idx, carry):
        j = idx // d
        r = idx % d
        bias = bias_s[jnp.where((j == 0) & (n == 0), 1, 0)]
        st_tile = jnp.zeros((blk, LANES), F32)
        for kvh in range(SWA_KV_HEADS):
            q4 = q_ref[0, 0, j, r, :, kvh * SWA_REP * e:(kvh + 1) * SWA_REP * e]
            qs = jnp.concatenate([q4[:, h * e:(h + 1) * e] for h in range(SWA_REP)], axis=0)
            kcols = slice(kvh * e, (kvh + 1) * e)
            kk = jnp.concatenate([k_s[j, r, :, kcols], k_s[j + 1, r, :, kcols]], axis=0)
            vv = jnp.concatenate([v_s[j, r, :, kcols], v_s[j + 1, r, :, kcols]], axis=0)
            s = lax.dot_general(qs, kk, _NT, preferred_element_type=F32)
            s = (s.reshape(SWA_REP, blk, 2 * blk) + bias).reshape(SWA_REP * blk, 2 * blk)
            m = jnp.max(s, axis=-1, keepdims=True)
            p = jnp.exp(s - m).astype(BF16)
            o = jnp.dot(p, jnp.concatenate([vv, ones], axis=1), preferred_element_type=F32)
            for h in range(SWA_REP):
                head = kvh * SWA_REP + h
                o_ref[0, 0, j, r, :, head * e:(head + 1) * e] = o[h * blk:(h + 1) * blk, 0:e].astype(BF16)
                st_tile = jnp.where(lane_slot == 2 * head, m[h * blk:(h + 1) * blk], st_tile)
                st_tile = jnp.where(lane_slot == 2 * head + 1, o[h * blk:(h + 1) * blk, e:], st_tile)
        st_ref[0, 0, j, r] = st_tile
        return carry

    lax.fori_loop(0, n_j * d, unit, 0, unroll=True)


def _swa_attention(q, k, v):
    bsz, n_span, n_j, d, blk, _ = q.shape
    cur = lambda w: pl.BlockSpec((1, 1, n_j, d, blk, w), lambda b, n: (b, n, 0, 0, 0, 0))
    prev = pl.BlockSpec((1, 1, 1, d, blk, SWA_KV_W), lambda b, n: (b, jnp.maximum(n - 1, 0), n_j - 1, 0, 0, 0))
    stat_shape = q.shape[:-1] + (LANES,)
    return pl.pallas_call(
        _swa_kernel,
        grid=(bsz, n_span),
        in_specs=[cur(SWA_OUT), cur(SWA_KV_W), prev, cur(SWA_KV_W), prev],
        out_specs=[cur(SWA_OUT), cur(LANES)],
        out_shape=[jax.ShapeDtypeStruct(q.shape, BF16), jax.ShapeDtypeStruct(stat_shape, F32)],
        scratch_shapes=[pltpu.VMEM((n_j + 1, d, blk, SWA_KV_W), BF16),
                        pltpu.VMEM((n_j + 1, d, blk, SWA_KV_W), BF16),
                        pltpu.VMEM((2, blk, 2 * blk), F32)],
        compiler_params=_params("parallel", "parallel"),
        name=f"swa_attention_d{d}",
    )(q, k, k, v, v)


def _merge_out_kernel(*refs):
    n_g = SWA_N_GROUPS
    o_refs, st_refs = refs[0:n_g], refs[n_g:2 * n_g]
    g_ref, x_ref, w_ref, lg_ref, lb_ref, out_ref, stat_s, o_s = refs[2 * n_g:]
    tm = ROW_TILE

    def to_tokens(ref, d, dst):
        jb, rows = _tile_unit_rows(d)
        n_slabs = ref.shape[-1] // LANES
        lanes = lambda c: slice(c * LANES, (c + 1) * LANES)
        if d == 1:
            tok = ref[0, 0].reshape(tm, ref.shape[-1])
            return [tok[:, lanes(c)] for c in range(n_slabs)]
        for jj in range(jb):
            for r in range(d):
                for c in range(n_slabs):
                    dst[c, _token_rows(d, jj, r, rows), :] = ref[0, 0, jj, r, :, lanes(c)].astype(F32)
        return [dst[c] for c in range(n_slabs)]

    dils = [d for _, d in SWA_GROUPS]
    sts = [to_tokens(st_refs[gi], d, stat_s.at[gi:gi + 1])[0] for gi, d in enumerate(dils)]
    mx = functools.reduce(jnp.maximum, sts)
    es = [jnp.exp(st - mx) for st in sts]
    ls = [pltpu.roll(st, LANES - SWA_STAT_LANES, 1) for st in sts]
    tot = functools.reduce(jnp.add, [eg * lg for eg, lg in zip(es, ls)])
    inv_tot = 1.0 / tot
    m_slot = (lax.broadcasted_iota(jnp.int32, (tm, LANES), 1) // SWA_STAT_LANES) % 2 == 0
    src = lax.broadcasted_iota(jnp.int32, (LANES, SWA_OUT), 0)
    dst = lax.broadcasted_iota(jnp.int32, (LANES, SWA_OUT), 1)
    expand = jnp.where(src == (dst // SWA_HEAD_DIM) * 2 * SWA_STAT_LANES, 1.0, 0.0).astype(BF16)
    acc = None
    for gi, d in enumerate(dils):
        w = jnp.dot(jnp.where(m_slot, es[gi] * inv_tot, 0.0).astype(BF16), expand, preferred_element_type=F32)
        term = w.astype(BF16) * jnp.concatenate(to_tokens(o_refs[gi], d, o_s), axis=1).astype(BF16)
        acc = term if acc is None else acc + term
    a = acc * _silu(g_ref[...])
    y = jnp.dot(a, w_ref[...], preferred_element_type=F32)
    out_ref[...] = _layer_norm(DEEPNORM_ALPHA * x_ref[...] + y, lg_ref[1:2, :], lb_ref[1:2, :])


def _merge_out_ln(os, sts, g, x2, w_out, ln_g, ln_b, bsz, seq):
    tm = ROW_TILE
    n_span = seq // SWA_SPAN
    row = lambda w: pl.BlockSpec((tm, w), lambda b, n, t: ((b * n_span + n) * TILES_PER_SPAN + t, 0))
    dils = [d for _, d in SWA_GROUPS]
    return pl.pallas_call(
        _merge_out_kernel,
        grid=(bsz, n_span, TILES_PER_SPAN),
        in_specs=[_tile_unit_spec(d, SWA_OUT) for d in dils] + [_tile_unit_spec(d, LANES) for d in dils]
        + [row(SWA_OUT), row(D_MODEL), _full(w_out.shape), _full(ln_g.shape), _full(ln_b.shape)],
        out_specs=row(D_MODEL),
        out_shape=jax.ShapeDtypeStruct((bsz * seq, D_MODEL), F32),
        scratch_shapes=[pltpu.VMEM((SWA_N_GROUPS, tm, LANES), F32),
                        pltpu.VMEM((SWA_OUT // LANES, tm, LANES), F32)],
        compiler_params=_params("parallel", "parallel", "parallel"),
        name="merge_out_ln",
    )(*os, *sts, g, x2, w_out, ln_g, ln_b)


def _rope_tables(seq):
    half = SWA_HEAD_DIM // 2
    inv = ROPE_THETA ** (-(jnp.arange(half, dtype=F32) * 2.0) / SWA_HEAD_DIM)
    inv = jnp.concatenate([inv, inv])[None, :]
    base = (jnp.arange(seq // ROW_TILE) * ROW_TILE).astype(F32)[:, None] * inv
    off = jnp.arange(ROW_TILE).astype(F32)[:, None] * inv
    return jnp.cos(base), jnp.sin(base), jnp.cos(off), jnp.sin(off)


GLA_N_MAIN = 2 * GLA_DK_TOTAL + 2 * GLA_DV_TOTAL
CAST_ROWS = 128


def _cast_weights_kernel(gw_in_t, gw_a2, gw_out, w_main, w_alr, w_a2, w_out0):
    r = CAST_ROWS
    t = gw_in_t[0]
    for c in range(0, GLA_N_MAIN, r):
        w_main[:, c:c + r] = t[c:c + r, :].T.astype(BF16)
    tail = t[GLA_N_MAIN + GLA_GATE_RANK - r:, :].T
    lane = lax.broadcasted_iota(jnp.int32, (r, r), 1)
    w_alr[...] = jnp.where(lane >= r - GLA_GATE_RANK, tail, 0.0).astype(BF16)
    w_a2[...] = jnp.zeros_like(w_a2)
    w_a2[LANES - GLA_GATE_RANK:, :] = gw_a2[0].astype(BF16)
    w_out0[...] = gw_out[0].astype(BF16)


def _cast_weights(gla_w_in, gla_w_a2, gla_w_out):
    r = CAST_ROWS
    assert r == LANES
    rows3 = lambda n: pl.BlockSpec((1, r, n), lambda i: (0, i, 0))
    rows2 = lambda n: pl.BlockSpec((r, n), lambda i: (i, 0))
    gla_w_in_t = jnp.swapaxes(gla_w_in, 1, 2)
    cols_t = pl.BlockSpec((1, gla_w_in_t.shape[1], r), lambda i: (0, 0, i))
    out_widths = [GLA_N_MAIN, LANES, None, D_MODEL]
    out_shapes = [(D_MODEL, w) if w else (LANES, GLA_DK_TOTAL) for w in out_widths]
    return pl.pallas_call(
        _cast_weights_kernel,
        grid=(D_MODEL // r,),
        in_specs=[cols_t, _full(gla_w_a2.shape), rows3(D_MODEL)],
        out_specs=[rows2(w) if w else pl.BlockSpec((LANES, GLA_DK_TOTAL), lambda i: (0, 0)) for w in out_widths],
        out_shape=[jax.ShapeDtypeStruct(s, BF16) for s in out_shapes],
        compiler_params=_params("arbitrary"),
        name="cast_weights",
    )(gla_w_in_t, gla_w_a2, gla_w_out)


def kernel(x, gla_w_in, gla_w_a2, gla_b_a2, gla_norm_g, gla_w_out, w_kv, swa_w_in, swa_w_out, ln_g, ln_b):
    bsz, seq, _ = x.shape
    assert seq % SWA_SPAN == 0 and seq % GLA_BLOCK == 0
    assert gla_w_in.shape[0] == 1 and swa_w_in.shape[0] == 1 and ln_g.shape[0] == DEPTH
    m = bsz * seq

    w_main, w_alr, w_a2, w_out0 = _cast_weights(gla_w_in, gla_w_a2, gla_w_out)
    x2, w_all, w_out1 = _gla_layer(x, w_main, w_alr, w_a2, gla_b_a2, gla_norm_g, w_out0, ln_g, ln_b,
                                   w_kv, swa_w_in, swa_w_out)
    x2 = x2.reshape(m, D_MODEL)

    ks, vs, qs, g = _swa_in_proj(x2, w_all, _rope_tables(seq), bsz, seq)
    os, sts = zip(*[_swa_attention(qs[gi], ks[gi], vs[gi]) for gi in range(SWA_N_GROUPS)])
    out = _merge_out_ln(os, sts, g, x2, w_out1, ln_g, ln_b, bsz, seq)
    return out.reshape(bsz, seq, D_MODEL)
```

```python
import functools

import jax
import jax.numpy as jnp
from jax import lax
from jax.experimental import pallas as pl
from jax.experimental.pallas import tpu as pltpu

BF16 = jnp.bfloat16
F32 = jnp.float32

D_MODEL = 1024
DEPTH = 2
DEEPNORM_ALPHA = (2.0 * DEPTH) ** 0.25
LN_EPS = 1e-5
RMS_EPS = 1e-5

GLA_HEADS = 4
GLA_DK = 128
GLA_DV = 256
GLA_DK_TOTAL = GLA_HEADS * GLA_DK
GLA_DV_TOTAL = GLA_HEADS * GLA_DV
GLA_GATE_RANK = 16
GLA_GATE_TAU = 16.0
GLA_SUB = 64
GLA_CHUNK = 2 * GLA_SUB
GLA_BLOCK = 512
GLA_HEAD_GROUP = 4
GLA_FAST_MIN_LOG_DECAY = -40.0

SWA_GROUPS = ((128, 1), (512, 4), (2048, 16))
SWA_N_GROUPS = len(SWA_GROUPS)
SWA_HEAD_DIM = 128
SWA_Q_HEADS = 8
SWA_KV_HEADS = 2
SWA_REP = SWA_Q_HEADS // SWA_KV_HEADS
SWA_OUT = SWA_Q_HEADS * SWA_HEAD_DIM
SWA_KV_W = SWA_KV_HEADS * SWA_HEAD_DIM
SWA_KV_TOTAL = 2 * SWA_N_GROUPS * SWA_KV_W
SWA_BLOCK = 128
SWA_SPAN = 2048
SWA_STAT_LANES = SWA_HEAD_DIM // (2 * SWA_Q_HEADS)
ROPE_THETA = 10000.0

LANES = 128
BF16_ROWS = 16
F32_ROWS = 8
MXU_COLS = 256
ROW_TILE = 1024
TILES_PER_SPAN = SWA_SPAN // ROW_TILE
VMEM_LIMIT = 56 * 1024 * 1024

_NT = (((1,), (1,)), ((), ()))


def _params(*sem, flags=None):
    return pltpu.CompilerParams(dimension_semantics=sem, vmem_limit_bytes=VMEM_LIMIT, flags=flags)


def _full(shape):
    return pl.BlockSpec(shape, lambda *_: (0,) * len(shape), pipeline_mode=pl.Buffered(1))


def _layer_norm(h, g, b):
    mu = jnp.mean(h, axis=-1, keepdims=True)
    hc = h - mu
    var = jnp.mean(hc * hc, axis=-1, keepdims=True)
    return hc * lax.rsqrt(var + LN_EPS) * g + b


def _silu(g):
    return g * (0.5 + 0.5 * jnp.tanh(0.5 * g))


def _gla_project_pieces(x_ref, w_ref, walr_ref, wa2_ref, ba2_ref, xb_s, q_ref, k_ref, v_ref, g_ref, la_ref):
    xb_s[...] = x_ref[0].astype(BF16)

    def tile(dst, col, lo, scale):
        def piece():
            y = jnp.dot(xb_s[...], w_ref[:, col + lo:col + lo + MXU_COLS], preferred_element_type=F32)
            dst[:, lo:lo + MXU_COLS] = (y if scale is None else y * scale).astype(BF16)
        return piece

    def gate_piece():
        a_lr = jnp.dot(xb_s[...], walr_ref[...], preferred_element_type=F32)
        z = jnp.dot(a_lr.astype(BF16), wa2_ref[...], preferred_element_type=F32) + ba2_ref[...]
        log_sig = jnp.minimum(z, 0.0) - jnp.log(1.0 + jnp.exp(-jnp.abs(z)))
        la_ref[...] = log_sig * (1.0 / GLA_GATE_TAU)

    pieces, col = [gate_piece], 0
    for dst, scale in ((q_ref, GLA_DK ** -0.5), (k_ref, None), (v_ref, None), (g_ref, None)):
        pieces += [tile(dst, col, lo, scale) for lo in range(0, dst.shape[1], MXU_COLS)]
        col += dst.shape[1]
    return pieces


def _gla_recur_pieces(q_ref, k_ref, v_ref, g_ref, la_ref, ng_ref, o_ref,
                      st_ref, qe_s, qi_s, kd_s, kx_s, klt_s, gam_s):
    t = la_ref.shape[0]
    c, sub = GLA_CHUNK, GLA_SUB

    row = lax.broadcasted_iota(jnp.int32, (c, c), 0)
    col = lax.broadcasted_iota(jnp.int32, (c, c), 1)
    causal = col <= row
    tri = jnp.where(causal & (row // sub == col // sub), 1.0, 0.0).astype(BF16)
    halves = lambda first, second: jnp.concatenate([first, second], axis=0)

    for ic in range(t // c):
        rows = slice(ic * c, (ic + 1) * c)
        first = slice(ic * c, ic * c + sub)
        second = slice(ic * c + sub, (ic + 1) * c)

        la = la_ref[rows, :]
        la_hi = la.astype(BF16)
        la_lo = (la - la_hi.astype(F32)).astype(BF16)
        bs = jnp.dot(tri, la_hi, preferred_element_type=F32) + jnp.dot(tri, la_lo, preferred_element_type=F32)
        e_a = jnp.exp(bs[sub - 1:sub, :])
        e_b = jnp.exp(bs[c - 1:c, :])
        qi = q_ref[rows, :].astype(F32) * jnp.exp(bs)
        kd = k_ref[rows, :].astype(F32) * jnp.exp(-bs)
        kl_a, kl_b = kd[:sub] * e_a, kd[sub:] * e_b
        qi_s[rows, :] = qi.astype(BF16)
        kd_s[rows, :] = kd.astype(BF16)
        qe_s[rows, :] = halves(qi[:sub], qi[sub:] * e_a).astype(BF16)
        kx_s[rows, :] = halves(kl_a, kd[sub:]).astype(BF16)
        klc = halves(kl_a * e_b, kl_b)
        gam = e_a * e_b
        for h in range(GLA_HEADS):
            kcols = slice(h * GLA_DK, (h + 1) * GLA_DK)
            klt_s[ic, h] = klc[:, kcols].T.astype(BF16)
            gam_s[ic, h] = jnp.broadcast_to(gam[:, kcols], (c, GLA_DK)).T
        yield

        for h0 in range(0, GLA_HEADS, GLA_HEAD_GROUP):
            stage1 = []
            for h in range(h0, h0 + GLA_HEAD_GROUP):
                kcols = slice(h * GLA_DK, (h + 1) * GLA_DK)
                st = st_ref[h]
                o_inter = jnp.dot(qe_s[rows, kcols], st.astype(BF16), preferred_element_type=F32)
                att = jnp.concatenate(
                    [lax.dot_general(qi_s[first, kcols], kd_s[rows, kcols], _NT, preferred_element_type=F32),
                     lax.dot_general(qi_s[second, kcols], kx_s[rows, kcols], _NT, preferred_element_type=F32)],
                    axis=0)
                stage1.append((st, o_inter, att))
            yield
            for h, (st, o_inter, att) in zip(range(h0, h0 + GLA_HEAD_GROUP), stage1):
                vcols = slice(h * GLA_DV, (h + 1) * GLA_DV)
                att = jnp.where(causal, att, 0.0).astype(BF16)
                both = jnp.dot(jnp.concatenate([att, klt_s[ic, h]], axis=0), v_ref[rows, vcols],
                               preferred_element_type=F32)
                gam_t = gam_s[ic, h]
                st_ref[h] = st * jnp.concatenate([gam_t] * (GLA_DV // GLA_DK), axis=1) + both[c:]
                o = o_inter + both[0:c]
                o = o * lax.rsqrt(jnp.mean(o * o, axis=-1, keepdims=True) + RMS_EPS) * ng_ref[:, vcols]
                o_ref[rows, vcols] = o.astype(BF16) * _silu(g_ref[rows, vcols])
            yield


def _pairwise_scores(b_s, qf_s, kf_s, kcols):
    c = GLA_CHUNK
    bj = b_s[:, kcols]
    kj = kf_s[:, kcols]
    j_idx = lax.broadcasted_iota(jnp.int32, (c, 1), 0)
    lane = lax.broadcasted_iota(jnp.int32, (c, c), 1)

    def query_rows(g, att_t):
        base = pl.multiple_of(g * F32_ROWS, F32_ROWS)
        b_g = b_s[pl.ds(base, F32_ROWS), kcols]
        q_g = qf_s[pl.ds(base, F32_ROWS), kcols]
        for r in range(F32_ROWS):
            i = g * F32_ROWS + r
            w = jnp.exp(jnp.minimum(b_g[r:r + 1] - bj, 0.0)) * kj * q_g[r:r + 1]
            col = jnp.sum(w, axis=-1, keepdims=True)
            att_t = jnp.where(lane == i, jnp.where(j_idx <= i, col, 0.0), att_t)
        return att_t

    return lax.fori_loop(0, c // F32_ROWS, query_rows, jnp.zeros((c, c), F32)).T


def _gla_recur_pairwise(q_ref, k_ref, v_ref, g_ref, la_ref, ng_ref, o_ref, st_ref, b_s, qf_s, kf_s):
    c, sub = GLA_CHUNK, GLA_SUB
    row = lax.broadcasted_iota(jnp.int32, (c, c), 0)
    col = lax.broadcasted_iota(jnp.int32, (c, c), 1)
    tri = jnp.where((col <= row) & (row // sub == col // sub), 1.0, 0.0).astype(BF16)

    def chunk(ic, carry):
        rows = pl.ds(pl.multiple_of(ic * c, c), c)
        la = la_ref[rows, :]
        la_hi = la.astype(BF16)
        la_lo = (la - la_hi.astype(F32)).astype(BF16)
        bs = jnp.dot(tri, la_hi, preferred_element_type=F32) + jnp.dot(tri, la_lo, preferred_element_type=F32)
        b = jnp.concatenate([bs[:sub], bs[sub:] + bs[sub - 1:sub, :]], axis=0)
        qf, kf = q_ref[rows, :].astype(F32), k_ref[rows, :].astype(F32)
        b_s[...], qf_s[...], kf_s[...] = b, qf, kf
        qe = (qf * jnp.exp(b)).astype(BF16)
        kl = kf * jnp.exp(b[c - 1:c, :] - b)
        gam = jnp.exp(b[c - 1:c, :])
        for h in range(GLA_HEADS):
            kcols = slice(h * GLA_DK, (h + 1) * GLA_DK)
            vcols = slice(h * GLA_DV, (h + 1) * GLA_DV)
            st = st_ref[h]
            att = _pairwise_scores(b_s, qf_s, kf_s, kcols).astype(BF16)
            o_inter = jnp.dot(qe[:, kcols], st.astype(BF16), preferred_element_type=F32)
            both = jnp.dot(jnp.concatenate([att, kl[:, kcols].T.astype(BF16)], axis=0), v_ref[rows, vcols],
                           preferred_element_type=F32)
            gam_t = jnp.broadcast_to(gam[:, kcols], (c, GLA_DK)).T
            st_ref[h] = st * jnp.concatenate([gam_t] * (GLA_DV // GLA_DK), axis=1) + both[c:]
            o = o_inter + both[0:c]
            o = o * lax.rsqrt(jnp.mean(o * o, axis=-1, keepdims=True) + RMS_EPS) * ng_ref[:, vcols]
            o_ref[rows, vcols] = o.astype(BF16) * _silu(g_ref[rows, vcols])
        return carry

    lax.fori_loop(0, la_ref.shape[0] // c, chunk, 0)


N_PROJ = 5


def _gla_layer_kernel(xp_ref, xr_ref, w_ref, walr_ref, wa2_ref, ba2_ref, ng_ref, wout_ref, lg_ref, lb_ref,
                      wkv_ref, swin_ref, swout_ref, out_ref, wall_ref, wout1_ref, st_ref, *scratch,
                      blocks_per_seq):
    bufs = (scratch[0:N_PROJ], scratch[N_PROJ:2 * N_PROJ])
    xb_s, rec_s = scratch[2 * N_PROJ], scratch[2 * N_PROJ + 1:-5]
    pair_s, st0_s, og_s = scratch[-5:-2], scratch[-2], scratch[-1]
    j = pl.program_id(0)
    wall_ref[:, 0:SWA_KV_TOTAL] = wkv_ref[...].astype(BF16)
    wall_ref[:, SWA_KV_TOTAL:] = swin_ref[0].astype(BF16)
    wout1_ref[...] = swout_ref[0].astype(BF16)

    @pl.when(j == 0)
    def _():
        for ref in bufs[1]:
            ref[...] = jnp.zeros_like(ref)

    @pl.when((j == 0) | (j % blocks_per_seq == 1 % blocks_per_seq))
    def _():
        st_ref[...] = jnp.zeros_like(st_ref)

    def step(write, read):
        totals = jnp.sum(read[N_PROJ - 1][...].reshape(-1, GLA_SUB, GLA_DK_TOTAL), axis=1)
        out_of_range = jnp.min(totals) < GLA_FAST_MIN_LOG_DECAY
        st0_s[...] = st_ref[...]

        pieces = _gla_project_pieces(xp_ref, w_ref, walr_ref, wa2_ref, ba2_ref, xb_s, *write)
        n_pieces = len(pieces)
        n_stages = (xb_s.shape[0] // GLA_CHUNK) * (1 + 2 * (GLA_HEADS // GLA_HEAD_GROUP))
        for i, _ in enumerate(_gla_recur_pieces(*read, ng_ref, og_s, st_ref, *rec_s)):
            issued_after = ((i + 1) * n_pieces + n_stages - 1) // n_stages
            while n_pieces - len(pieces) < issued_after:
                pieces.pop(0)()

        def project_out():
            y = jnp.dot(og_s[...], wout_ref[...], preferred_element_type=F32)
            out_ref[0] = _layer_norm(DEEPNORM_ALPHA * xr_ref[0] + y, lg_ref[0:1, :], lb_ref[0:1, :])

        project_out()

        @pl.when(out_of_range)
        def _():
            st_ref[...] = st0_s[...]
            _gla_recur_pairwise(*read, ng_ref, og_s, st_ref, *pair_s)
            project_out()

    @pl.when(j % 2 == 0)
    def _():
        step(bufs[0], bufs[1])

    @pl.when(j % 2 == 1)
    def _():
        step(bufs[1], bufs[0])


def _gla_layer(x, w_main, w_alr, w_a2, b_a2, norm_g, w_out, ln_g, ln_b, w_kv, swa_w_in, swa_w_out):
    bsz, s, _ = x.shape
    t = GLA_BLOCK
    n_blk = s // t
    n_chunks = t // GLA_CHUNK
    rows = lambda n, dt: pltpu.VMEM((t, n), dt)
    proj_bufs = [rows(GLA_DK_TOTAL, BF16), rows(GLA_DK_TOTAL, BF16), rows(GLA_DV_TOTAL, BF16),
                 rows(GLA_DV_TOTAL, BF16), rows(GLA_DK_TOTAL, F32)]
    assert len(proj_bufs) == N_PROJ
    n_all = bsz * n_blk
    xb = x.reshape(n_all, t, D_MODEL)
    cur = pl.BlockSpec((1, t, D_MODEL), lambda j: (jnp.minimum(j, n_all - 1), 0, 0))
    prev = pl.BlockSpec((1, t, D_MODEL), lambda j: (jnp.maximum(j - 1, 0), 0, 0))
    consts = (w_main, w_alr, w_a2, b_a2, norm_g, w_out, ln_g, ln_b)
    wr = D_MODEL // n_all
    assert wr * n_all == D_MODEL and wr % BF16_ROWS == 0
    wrows2 = lambda n: pl.BlockSpec((wr, n), lambda j: (jnp.minimum(j, n_all - 1), 0))
    wrows3 = lambda n: pl.BlockSpec((1, wr, n), lambda j: (0, jnp.minimum(j, n_all - 1), 0))
    n_in = w_kv.shape[1] + swa_w_in.shape[2]
    return pl.pallas_call(
        functools.partial(_gla_layer_kernel, blocks_per_seq=n_blk),
        grid=(n_all + 1,),
        in_specs=[cur, prev] + [_full(a.shape) for a in consts]
        + [wrows2(w_kv.shape[1]), wrows3(swa_w_in.shape[2]), wrows3(D_MODEL)],
        out_specs=[prev, wrows2(n_in), wrows2(D_MODEL)],
        out_shape=[jax.ShapeDtypeStruct(xb.shape, F32), jax.ShapeDtypeStruct((D_MODEL, n_in), BF16),
                   jax.ShapeDtypeStruct((D_MODEL, D_MODEL), BF16)],
        scratch_shapes=[pltpu.VMEM((GLA_HEADS, GLA_DK, GLA_DV), F32)] + proj_bufs + proj_bufs
        + [rows(D_MODEL, BF16)] + [rows(GLA_DK_TOTAL, BF16)] * 4
        + [pltpu.VMEM((n_chunks, GLA_HEADS, GLA_DK, GLA_CHUNK), BF16),
           pltpu.VMEM((n_chunks, GLA_HEADS, GLA_DK, GLA_CHUNK), F32)]
        + [pltpu.VMEM((GLA_CHUNK, GLA_DK_TOTAL), F32)] * 3 + [pltpu.VMEM((GLA_HEADS, GLA_DK, GLA_DV), F32)]
        + [rows(GLA_DV_TOTAL, BF16)],
        compiler_params=_params("arbitrary"),
        name="gla_layer",
    )(xb, xb, *consts, w_kv, swa_w_in, swa_w_out)


def _unit_shape(bsz, seq, d, width):
    return (bsz, seq // SWA_SPAN, SWA_SPAN // (SWA_BLOCK * d), d, SWA_BLOCK, width)


def _tile_unit_spec(d, width):
    rows_per = ROW_TILE // d
    if rows_per >= SWA_BLOCK:
        jb = rows_per // SWA_BLOCK
        return pl.BlockSpec((1, 1, jb, d, SWA_BLOCK, width), lambda b, n, t: (b, n, t, 0, 0, 0))
    per_j = SWA_BLOCK // rows_per
    return pl.BlockSpec((1, 1, 1, d, rows_per, width), lambda b, n, t: (b, n, t // per_j, 0, t % per_j, 0))


def _tile_unit_rows(d):
    rows_per = ROW_TILE // d
    return max(rows_per // SWA_BLOCK, 1), min(rows_per, SWA_BLOCK)


def _token_rows(d, jj, r, rows):
    return pl.ds(jj * SWA_BLOCK * d + r, rows, stride=d) if d > 1 else pl.ds(jj * SWA_BLOCK, rows)


def _rope(h, cos, sin_signed):
    return h * cos + pltpu.roll(h, SWA_HEAD_DIM // 2, 1) * sin_signed


def _swa_in_kernel(x_ref, w_ref, cos_base_ref, sin_base_ref, cos_off_ref, sin_off_ref, *refs):
    k_refs = refs[0:SWA_N_GROUPS]
    v_refs = refs[SWA_N_GROUPS:2 * SWA_N_GROUPS]
    q_refs = refs[2 * SWA_N_GROUPS:3 * SWA_N_GROUPS]
    g_ref = refs[3 * SWA_N_GROUPS]
    tok_s = refs[3 * SWA_N_GROUPS + 1]
    rope_s = refs[3 * SWA_N_GROUPS + 2]
    xb = x_ref[...].astype(BF16)
    e = SWA_HEAD_DIM
    tile = pl.ds(pl.program_id(1) * TILES_PER_SPAN + pl.program_id(2), 1)
    ca, sa = cos_base_ref[tile, :], sin_base_ref[tile, :]
    cb, sb = cos_off_ref[...], sin_off_ref[...]
    cos = ca * cb - sa * sb
    lane = lax.broadcasted_iota(jnp.int32, cb.shape, 1)
    sin = jnp.where(lane < e // 2, -1.0, 1.0) * (sa * cb + ca * sb)
    rope_s[0], rope_s[1] = cos, sin
    rope_s[2], rope_s[3] = cos * (e ** -0.5), sin * (e ** -0.5)
    kv_half = SWA_N_GROUPS * SWA_KV_W
    q0 = 2 * kv_half

    n_slabs = tok_s.shape[0]
    pipeline = {"slab": 0, "pending": None}

    def head_pair(col, out_ref, head0, d, rope):
        y = jnp.dot(xb, w_ref[:, col:col + MXU_COLS], preferred_element_type=F32)
        jb, rows = _tile_unit_rows(d)
        staged = []
        for i in range(MXU_COLS // e):
            h = head0 + i
            yh = y[:, i * e:(i + 1) * e]
            if rope is not None:
                yh = _rope(yh, rope_s[rope], rope_s[rope + 1])
            if d == 1:
                out_ref[0, 0, :, :, :, h * e:(h + 1) * e] = yh.astype(BF16).reshape(jb, d, rows, e)
                continue
            slab = pipeline["slab"]
            pipeline["slab"] = (slab + 1) % n_slabs
            tok_s[slab] = yh
            staged.append((slab, h))

        def scatter():
            for slab, h in staged:
                for jj in range(jb):
                    for r in range(d):
                        out_ref[0, 0, jj, r, :, h * e:(h + 1) * e] = (
                            tok_s[slab, _token_rows(d, jj, r, rows), :].astype(BF16))

        flush()
        pipeline["pending"] = scatter if staged else None

    def flush():
        if pipeline["pending"] is not None:
            pipeline["pending"]()
            pipeline["pending"] = None

    g0 = q0 + SWA_N_GROUPS * SWA_OUT
    gate_tiles = list(range(0, SWA_OUT, MXU_COLS))

    def gate_tile():
        c = gate_tiles.pop(0)
        g_ref[:, c:c + MXU_COLS] = jnp.dot(xb, w_ref[:, g0 + c:g0 + c + MXU_COLS],
                                           preferred_element_type=F32).astype(BF16)

    def group_tiles(gi, d):
        yield lambda: head_pair(gi * SWA_KV_W, k_refs[gi], 0, d, 0)
        yield lambda: head_pair(kv_half + gi * SWA_KV_W, v_refs[gi], 0, d, None)
        for h0 in range(0, SWA_Q_HEADS, MXU_COLS // e):
            yield functools.partial(head_pair, q0 + gi * SWA_OUT + h0 * e, q_refs[gi], h0, d, 2)

    for tiles in zip(*[list(group_tiles(gi, d)) for gi, (_, d) in enumerate(SWA_GROUPS)]):
        for tile in tiles:
            tile()
        if gate_tiles:
            gate_tile()
    while gate_tiles:
        gate_tile()
    flush()


def _swa_in_proj(x2, w_all, rope_tables, bsz, seq):
    tm = ROW_TILE
    n_span = seq // SWA_SPAN
    tile = lambda b, n, t: (b * n_span + n) * TILES_PER_SPAN + t
    row = lambda w: pl.BlockSpec((tm, w), lambda b, n, t: (tile(b, n, t), 0))
    dils = [d for _, d in SWA_GROUPS]
    unit_out = [(d, SWA_KV_W) for d in dils] * 2 + [(d, SWA_OUT) for d in dils]
    outs = pl.pallas_call(
        _swa_in_kernel,
        grid=(bsz, n_span, TILES_PER_SPAN),
        in_specs=[row(D_MODEL), _full(w_all.shape)] + [_full(a.shape) for a in rope_tables],
        out_specs=[_tile_unit_spec(d, w) for d, w in unit_out] + [row(SWA_OUT)],
        out_shape=[jax.ShapeDtypeStruct(_unit_shape(bsz, seq, d, w), BF16) for d, w in unit_out]
        + [jax.ShapeDtypeStruct((bsz * seq, SWA_OUT), BF16)],
        scratch_shapes=[pltpu.VMEM((SWA_OUT // LANES, tm, LANES), F32), pltpu.VMEM((4, tm, SWA_HEAD_DIM), F32)],
        compiler_params=_params("parallel", "parallel", "parallel"),
        name="swa_in_proj",
    )(x2, w_all, *rope_tables)
    return outs[0:3], outs[3:6], outs[6:9], outs[9]


def _swa_kernel(q_ref, kc_ref, kp_ref, vc_ref, vp_ref, o_ref, st_ref, k_s, v_s, bias_s):
    blk = SWA_BLOCK
    e = SWA_HEAD_DIM
    n_j, d = q_ref.shape[2], q_ref.shape[3]
    n = pl.program_id(1)

    k_s[0] = kp_ref[0, 0, 0]
    k_s[1:] = kc_ref[0, 0]
    v_s[0] = vp_ref[0, 0, 0]
    v_s[1:] = vc_ref[0, 0]

    qi = lax.broadcasted_iota(jnp.int32, (blk, 2 * blk), 0)
    kj = lax.broadcasted_iota(jnp.int32, (blk, 2 * blk), 1)
    rel = blk + qi - kj
    band = (rel >= 0) & (rel <= blk)
    bias_s[0] = jnp.where(band, 0.0, -jnp.inf)
    bias_s[1] = jnp.where(band & (kj >= blk), 0.0, -jnp.inf)
    lane_slot = lax.broadcasted_iota(jnp.int32, (blk, LANES), 1) // SWA_STAT_LANES
    ones = jnp.ones((2 * blk, LANES), BF16)

    def unit(idx, carry):
        j = idx // d
        r = idx % d
        bias = bias_s[jnp.where((j == 0) & (n == 0), 1, 0)]
        st_tile = jnp.zeros((blk, LANES), F32)
        for kvh in range(SWA_KV_HEADS):
            q4 = q_ref[0, 0, j, r, :, kvh * SWA_REP * e:(kvh + 1) * SWA_REP * e]
            qs = jnp.concatenate([q4[:, h * e:(h + 1) * e] for h in range(SWA_REP)], axis=0)
            kcols = slice(kvh * e, (kvh + 1) * e)
            kk = jnp.concatenate([k_s[j, r, :, kcols], k_s[j + 1, r, :, kcols]], axis=0)
            vv = jnp.concatenate([v_s[j, r, :, kcols], v_s[j + 1, r, :, kcols]], axis=0)
            s = lax.dot_general(qs, kk, _NT, preferred_element_type=F32)
            s = (s.reshape(SWA_REP, blk, 2 * blk) + bias).reshape(SWA_REP * blk, 2 * blk)
            m = jnp.max(s, axis=-1, keepdims=True)
            p = jnp.exp(s - m).astype(BF16)
            o = jnp.dot(p, jnp.concatenate([vv, ones], axis=1), preferred_element_type=F32)
            for h in range(SWA_REP):
                head = kvh * SWA_REP + h
                o_ref[0, 0, j, r, :, head * e:(head + 1) * e] = o[h * blk:(h + 1) * blk, 0:e].astype(BF16)
                st_tile = jnp.where(lane_slot == 2 * head, m[h * blk:(h + 1) * blk], st_tile)
                st_tile = jnp.where(lane_slot == 2 * head + 1, o[h * blk:(h + 1) * blk, e:], st_tile)
        st_ref[0, 0, j, r] = st_tile
        return carry

    lax.fori_loop(0, n_j * d, unit, 0, unroll=True)


def _swa_attention(q, k, v):
    bsz, n_span, n_j, d, blk, _ = q.shape
    cur = lambda w: pl.BlockSpec((1, 1, n_j, d, blk, w), lambda b, n: (b, n, 0, 0, 0, 0))
    prev = pl.BlockSpec((1, 1, 1, d, blk, SWA_KV_W), lambda b, n: (b, jnp.maximum(n - 1, 0), n_j - 1, 0, 0, 0))
    stat_shape = q.shape[:-1] + (LANES,)
    return pl.pallas_call(
        _swa_kernel,
        grid=(bsz, n_span),
        in_specs=[cur(SWA_OUT), cur(SWA_KV_W), prev, cur(SWA_KV_W), prev],
        out_specs=[cur(SWA_OUT), cur(LANES)],
        out_shape=[jax.ShapeDtypeStruct(q.shape, BF16), jax.ShapeDtypeStruct(stat_shape, F32)],
        scratch_shapes=[pltpu.VMEM((n_j + 1, d, blk, SWA_KV_W), BF16),
                        pltpu.VMEM((n_j + 1, d, blk, SWA_KV_W), BF16),
                        pltpu.VMEM((2, blk, 2 * blk), F32)],
        compiler_params=_params("parallel", "parallel"),
        name=f"swa_attention_d{d}",
    )(q, k, k, v, v)


def _merge_out_kernel(*refs):
    n_g = SWA_N_GROUPS
    o_refs, st_refs = refs[0:n_g], refs[n_g:2 * n_g]
    g_ref, x_ref, w_ref, lg_ref, lb_ref, out_ref, stat_s, o_s = refs[2 * n_g:]
    tm = ROW_TILE

    def to_tokens(ref, d, dst):
        jb, rows = _tile_unit_rows(d)
        n_slabs = ref.shape[-1] // LANES
        lanes = lambda c: slice(c * LANES, (c + 1) * LANES)
        if d == 1:
            tok = ref[0, 0].reshape(tm, ref.shape[-1])
            return [tok[:, lanes(c)] for c in range(n_slabs)]
        for jj in range(jb):
            for r in range(d):
                for c in range(n_slabs):
                    dst[c, _token_rows(d, jj, r, rows), :] = ref[0, 0, jj, r, :, lanes(c)].astype(F32)
        return [dst[c] for c in range(n_slabs)]

    dils = [d for _, d in SWA_GROUPS]
    sts = [to_tokens(st_refs[gi], d, stat_s.at[gi:gi + 1])[0] for gi, d in enumerate(dils)]
    mx = functools.reduce(jnp.maximum, sts)
    es = [jnp.exp(st - mx) for st in sts]
    ls = [pltpu.roll(st, LANES - SWA_STAT_LANES, 1) for st in sts]
    tot = functools.reduce(jnp.add, [eg * lg for eg, lg in zip(es, ls)])
    inv_tot = 1.0 / tot
    m_slot = (lax.broadcasted_iota(jnp.int32, (tm, LANES), 1) // SWA_STAT_LANES) % 2 == 0
    src = lax.broadcasted_iota(jnp.int32, (LANES, SWA_OUT), 0)
    dst = lax.broadcasted_iota(jnp.int32, (LANES, SWA_OUT), 1)
    expand = jnp.where(src == (dst // SWA_HEAD_DIM) * 2 * SWA_STAT_LANES, 1.0, 0.0).astype(BF16)
    acc = None
    for gi, d in enumerate(dils):
        w = jnp.dot(jnp.where(m_slot, es[gi] * inv_tot, 0.0).astype(BF16), expand, preferred_element_type=F32)
        term = w.astype(BF16) * jnp.concatenate(to_tokens(o_refs[gi], d, o_s), axis=1).astype(BF16)
        acc = term if acc is None else acc + term
    a = acc * _silu(g_ref[...])
    y = jnp.dot(a, w_ref[...], preferred_element_type=F32)
    out_ref[...] = _layer_norm(DEEPNORM_ALPHA * x_ref[...] + y, lg_ref[1:2, :], lb_ref[1:2, :])


def _merge_out_ln(os, sts, g, x2, w_out, ln_g, ln_b, bsz, seq):
    tm = ROW_TILE
    n_span = seq // SWA_SPAN
    row = lambda w: pl.BlockSpec((tm, w), lambda b, n, t: ((b * n_span + n) * TILES_PER_SPAN + t, 0))
    dils = [d for _, d in SWA_GROUPS]
    return pl.pallas_call(
        _merge_out_kernel,
        grid=(bsz, n_span, TILES_PER_SPAN),
        in_specs=[_tile_unit_spec(d, SWA_OUT) for d in dils] + [_tile_unit_spec(d, LANES) for d in dils]
        + [row(SWA_OUT), row(D_MODEL), _full(w_out.shape), _full(ln_g.shape), _full(ln_b.shape)],
        out_specs=row(D_MODEL),
        out_shape=jax.ShapeDtypeStruct((bsz * seq, D_MODEL), F32),
        scratch_shapes=[pltpu.VMEM((SWA_N_GROUPS, tm, LANES), F32),
                        pltpu.VMEM((SWA_OUT // LANES, tm, LANES), F32)],
        compiler_params=_params("parallel", "parallel", "parallel"),
        name="merge_out_ln",
    )(*os, *sts, g, x2, w_out, ln_g, ln_b)


def _rope_tables(seq):
    half = SWA_HEAD_DIM // 2
    inv = ROPE_THETA ** (-(jnp.arange(half, dtype=F32) * 2.0) / SWA_HEAD_DIM)
    inv = jnp.concatenate([inv, inv])[None, :]
    base = (jnp.arange(seq // ROW_TILE) * ROW_TILE).astype(F32)[:, None] * inv
    off = jnp.arange(ROW_TILE).astype(F32)[:, None] * inv
    return jnp.cos(base), jnp.sin(base), jnp.cos(off), jnp.sin(off)


GLA_N_MAIN = 2 * GLA_DK_TOTAL + 2 * GLA_DV_TOTAL
CAST_ROWS = 128


def _cast_weights_kernel(gw_in_t, gw_a2, gw_out, w_main, w_alr, w_a2, w_out0):
    r = CAST_ROWS
    t = gw_in_t[0]
    for c in range(0, GLA_N_MAIN, r):
        w_main[:, c:c + r] = t[c:c + r, :].T.astype(BF16)
    tail = t[GLA_N_MAIN + GLA_GATE_RANK - r:, :].T
    lane = lax.broadcasted_iota(jnp.int32, (r, r), 1)
    w_alr[...] = jnp.where(lane >= r - GLA_GATE_RANK, tail, 0.0).astype(BF16)
    w_a2[...] = jnp.zeros_like(w_a2)
    w_a2[LANES - GLA_GATE_RANK:, :] = gw_a2[0].astype(BF16)
    w_out0[...] = gw_out[0].astype(BF16)


def _cast_weights(gla_w_in, gla_w_a2, gla_w_out):
    r = CAST_ROWS
    assert r == LANES
    rows3 = lambda n: pl.BlockSpec((1, r, n), lambda i: (0, i, 0))
    rows2 = lambda n: pl.BlockSpec((r, n), lambda i: (i, 0))
    gla_w_in_t = jnp.swapaxes(gla_w_in, 1, 2)
    cols_t = pl.BlockSpec((1, gla_w_in_t.shape[1], r), lambda i: (0, 0, i))
    out_widths = [GLA_N_MAIN, LANES, None, D_MODEL]
    out_shapes = [(D_MODEL, w) if w else (LANES, GLA_DK_TOTAL) for w in out_widths]
    return pl.pallas_call(
        _cast_weights_kernel,
        grid=(D_MODEL // r,),
        in_specs=[cols_t, _full(gla_w_a2.shape), rows3(D_MODEL)],
        out_specs=[rows2(w) if w else pl.BlockSpec((LANES, GLA_DK_TOTAL), lambda i: (0, 0)) for w in out_widths],
        out_shape=[jax.ShapeDtypeStruct(s, BF16) for s in out_shapes],
        compiler_params=_params("arbitrary"),
        name="cast_weights",
    )(gla_w_in_t, gla_w_a2, gla_w_out)


def kernel(x, gla_w_in, gla_w_a2, gla_b_a2, gla_norm_g, gla_w_out, w_kv, swa_w_in, swa_w_out, ln_g, ln_b):
    bsz, seq, _ = x.shape
    assert seq % SWA_SPAN == 0 and seq % GLA_BLOCK == 0
    assert gla_w_in.shape[0] == 1 and swa_w_in.shape[0] == 1 and ln_g.shape[0] == DEPTH
    m = bsz * seq

    w_main, w_alr, w_a2, w_out0 = _cast_weights(gla_w_in, gla_w_a2, gla_w_out)
    x2, w_all, w_out1 = _gla_layer(x, w_main, w_alr, w_a2, gla_b_a2, gla_norm_g, w_out0, ln_g, ln_b,
                                   w_kv, swa_w_in, swa_w_out)
    x2 = x2.reshape(m, D_MODEL)

    ks, vs, qs, g = _swa_in_proj(x2, w_all, _rope_tables(seq), bsz, seq)
    os, sts = zip(*[_swa_attention(qs[gi], ks[gi], vs[gi]) for gi in range(SWA_N_GROUPS)])
    out = _merge_out_ln(os, sts, g, x2, w_out1, ln_g, ln_b, bsz, seq)
    return out.reshape(bsz, seq, D_MODEL)
```

```python
import functools

import jax
import jax.numpy as jnp
from jax import lax
from jax.experimental import pallas as pl
from jax.experimental.pallas import tpu as pltpu

BF16 = jnp.bfloat16
F32 = jnp.float32

D_MODEL = 1024
DEPTH = 2
DEEPNORM_ALPHA = (2.0 * DEPTH) ** 0.25
LN_EPS = 1e-5
RMS_EPS = 1e-5

GLA_HEADS = 4
GLA_DK = 128
GLA_DV = 256
GLA_DK_TOTAL = GLA_HEADS * GLA_DK
GLA_DV_TOTAL = GLA_HEADS * GLA_DV
GLA_GATE_RANK = 16
GLA_GATE_TAU = 16.0
GLA_SUB = 64
GLA_CHUNK = 2 * GLA_SUB
GLA_BLOCK = 512
GLA_HEAD_GROUP = 4
GLA_TAIL_PIECES = 2
GLA_FAST_MIN_LOG_DECAY = -40.0

SWA_GROUPS = ((128, 1), (512, 4), (2048, 16))
SWA_N_GROUPS = len(SWA_GROUPS)
SWA_HEAD_DIM = 128
SWA_Q_HEADS = 8
SWA_KV_HEADS = 2
SWA_REP = SWA_Q_HEADS // SWA_KV_HEADS
SWA_OUT = SWA_Q_HEADS * SWA_HEAD_DIM
SWA_KV_W = SWA_KV_HEADS * SWA_HEAD_DIM
SWA_KV_TOTAL = 2 * SWA_N_GROUPS * SWA_KV_W
SWA_BLOCK = 128
SWA_SPAN = 2048
SWA_STAT_LANES = SWA_HEAD_DIM // (2 * SWA_Q_HEADS)
ROPE_THETA = 10000.0

LANES = 128
BF16_ROWS = 16
F32_ROWS = 8
MXU_COLS = 256
ROW_TILE = 1024
TILES_PER_SPAN = SWA_SPAN // ROW_TILE
VMEM_LIMIT = 56 * 1024 * 1024

_NT = (((1,), (1,)), ((), ()))


def _params(*sem, flags=None):
    return pltpu.CompilerParams(dimension_semantics=sem, vmem_limit_bytes=VMEM_LIMIT, flags=flags)


def _full(shape):
    return pl.BlockSpec(shape, lambda *_: (0,) * len(shape), pipeline_mode=pl.Buffered(1))


def _layer_norm(h, g, b):
    mu = jnp.mean(h, axis=-1, keepdims=True)
    hc = h - mu
    var = jnp.mean(hc * hc, axis=-1, keepdims=True)
    return hc * lax.rsqrt(var + LN_EPS) * g + b


def _silu(g):
    return g * (0.5 + 0.5 * jnp.tanh(0.5 * g))


def _gla_project_pieces(x_ref, w_ref, walr_ref, wa2_ref, ba2_ref, xb_s, q_ref, k_ref, v_ref, g_ref, la_ref):
    xb_s[...] = x_ref[0].astype(BF16)

    def tile(dst, col, lo, scale):
        def piece():
            y = jnp.dot(xb_s[...], w_ref[:, col + lo:col + lo + MXU_COLS], preferred_element_type=F32)
            dst[:, lo:lo + MXU_COLS] = (y if scale is None else y * scale).astype(BF16)
        return piece

    def gate_piece():
        a_lr = jnp.dot(xb_s[...], walr_ref[...], preferred_element_type=F32)
        z = jnp.dot(a_lr.astype(BF16), wa2_ref[...], preferred_element_type=F32) + ba2_ref[...]
        log_sig = jnp.minimum(z, 0.0) - jnp.log(1.0 + jnp.exp(-jnp.abs(z)))
        la_ref[...] = log_sig * (1.0 / GLA_GATE_TAU)

    pieces, col = [], 0
    for dst, scale in ((q_ref, GLA_DK ** -0.5), (k_ref, None), (v_ref, None), (g_ref, None)):
        pieces += [tile(dst, col, lo, scale) for lo in range(0, dst.shape[1], MXU_COLS)]
        col += dst.shape[1]
    return pieces[:1] + [gate_piece] + pieces[1:]


def _gla_recur_pieces(q_ref, k_ref, v_ref, g_ref, la_ref, ng_ref, o_ref,
                      st_ref, qe_s, qi_s, kd_s, kx_s, klt_s, gam_s):
    t = la_ref.shape[0]
    c, sub = GLA_CHUNK, GLA_SUB

    row = lax.broadcasted_iota(jnp.int32, (c, c), 0)
    col = lax.broadcasted_iota(jnp.int32, (c, c), 1)
    causal = col <= row
    tri = jnp.where(causal & (row // sub == col // sub), 1.0, 0.0).astype(BF16)
    halves = lambda first, second: jnp.concatenate([first, second], axis=0)

    for ic in range(t // c):
        rows = slice(ic * c, (ic + 1) * c)
        first = slice(ic * c, ic * c + sub)
        second = slice(ic * c + sub, (ic + 1) * c)

        la = la_ref[rows, :]
        la_hi = la.astype(BF16)
        la_lo = (la - la_hi.astype(F32)).astype(BF16)
        bs = jnp.dot(tri, la_hi, preferred_element_type=F32) + jnp.dot(tri, la_lo, preferred_element_type=F32)
        e_a = jnp.exp(bs[sub - 1:sub, :])
        e_b = jnp.exp(bs[c - 1:c, :])
        qi = q_ref[rows, :].astype(F32) * jnp.exp(bs)
        kd = k_ref[rows, :].astype(F32) * jnp.exp(-bs)
        kl_a, kl_b = kd[:sub] * e_a, kd[sub:] * e_b
        qi_s[rows, :] = qi.astype(BF16)
        kd_s[rows, :] = kd.astype(BF16)
        qe_s[rows, :] = halves(qi[:sub], qi[sub:] * e_a).astype(BF16)
        kx_s[rows, :] = halves(kl_a, kd[sub:]).astype(BF16)
        klc = halves(kl_a * e_b, kl_b)
        gam = e_a * e_b
        for h in range(GLA_HEADS):
            kcols = slice(h * GLA_DK, (h + 1) * GLA_DK)
            klt_s[ic, h] = klc[:, kcols].T.astype(BF16)
            gam_s[ic, h] = jnp.broadcast_to(gam[:, kcols], (c, GLA_DK)).T
        yield

        for h0 in range(0, GLA_HEADS, GLA_HEAD_GROUP):
            stage1 = []
            for h in range(h0, h0 + GLA_HEAD_GROUP):
                kcols = slice(h * GLA_DK, (h + 1) * GLA_DK)
                st = st_ref[h]
                o_inter = jnp.dot(qe_s[rows, kcols], st.astype(BF16), preferred_element_type=F32)
                att = jnp.concatenate(
                    [lax.dot_general(qi_s[first, kcols], kd_s[rows, kcols], _NT, preferred_element_type=F32),
                     lax.dot_general(qi_s[second, kcols], kx_s[rows, kcols], _NT, preferred_element_type=F32)],
                    axis=0)
                stage1.append((st, o_inter, att))
            yield
            for h, (st, o_inter, att) in zip(range(h0, h0 + GLA_HEAD_GROUP), stage1):
                vcols = slice(h * GLA_DV, (h + 1) * GLA_DV)
                att = jnp.where(causal, att, 0.0).astype(BF16)
                both = jnp.dot(jnp.concatenate([att, klt_s[ic, h]], axis=0), v_ref[rows, vcols],
                               preferred_element_type=F32)
                gam_t = gam_s[ic, h]
                st_ref[h] = st * jnp.concatenate([gam_t] * (GLA_DV // GLA_DK), axis=1) + both[c:]
                o = o_inter + both[0:c]
                o = o * lax.rsqrt(jnp.mean(o * o, axis=-1, keepdims=True) + RMS_EPS) * ng_ref[:, vcols]
                o_ref[rows, vcols] = o.astype(BF16) * _silu(g_ref[rows, vcols])
            yield


def _pairwise_scores(b_s, qf_s, kf_s, kcols):
    c = GLA_CHUNK
    bj = b_s[:, kcols]
    kj = kf_s[:, kcols]
    j_idx = lax.broadcasted_iota(jnp.int32, (c, 1), 0)
    lane = lax.broadcasted_iota(jnp.int32, (c, c), 1)

    def query_rows(g, att_t):
        base = pl.multiple_of(g * F32_ROWS, F32_ROWS)
        b_g = b_s[pl.ds(base, F32_ROWS), kcols]
        q_g = qf_s[pl.ds(base, F32_ROWS), kcols]
        for r in range(F32_ROWS):
            i = g * F32_ROWS + r
            w = jnp.exp(jnp.minimum(b_g[r:r + 1] - bj, 0.0)) * kj * q_g[r:r + 1]
            col = jnp.sum(w, axis=-1, keepdims=True)
            att_t = jnp.where(lane == i, jnp.where(j_idx <= i, col, 0.0), att_t)
        return att_t

    return lax.fori_loop(0, c // F32_ROWS, query_rows, jnp.zeros((c, c), F32)).T


def _gla_recur_pairwise(q_ref, k_ref, v_ref, g_ref, la_ref, ng_ref, o_ref, st_ref, b_s, qf_s, kf_s):
    c, sub = GLA_CHUNK, GLA_SUB
    row = lax.broadcasted_iota(jnp.int32, (c, c), 0)
    col = lax.broadcasted_iota(jnp.int32, (c, c), 1)
    tri = jnp.where((col <= row) & (row // sub == col // sub), 1.0, 0.0).astype(BF16)

    def chunk(ic, carry):
        rows = pl.ds(pl.multiple_of(ic * c, c), c)
        la = la_ref[rows, :]
        la_hi = la.astype(BF16)
        la_lo = (la - la_hi.astype(F32)).astype(BF16)
        bs = jnp.dot(tri, la_hi, preferred_element_type=F32) + jnp.dot(tri, la_lo, preferred_element_type=F32)
        b = jnp.concatenate([bs[:sub], bs[sub:] + bs[sub - 1:sub, :]], axis=0)
        qf, kf = q_ref[rows, :].astype(F32), k_ref[rows, :].astype(F32)
        b_s[...], qf_s[...], kf_s[...] = b, qf, kf
        qe = (qf * jnp.exp(b)).astype(BF16)
        kl = kf * jnp.exp(b[c - 1:c, :] - b)
        gam = jnp.exp(b[c - 1:c, :])
        for h in range(GLA_HEADS):
            kcols = slice(h * GLA_DK, (h + 1) * GLA_DK)
            vcols = slice(h * GLA_DV, (h + 1) * GLA_DV)
            st = st_ref[h]
            att = _pairwise_scores(b_s, qf_s, kf_s, kcols).astype(BF16)
            o_inter = jnp.dot(qe[:, kcols], st.astype(BF16), preferred_element_type=F32)
            both = jnp.dot(jnp.concatenate([att, kl[:, kcols].T.astype(BF16)], axis=0), v_ref[rows, vcols],
                           preferred_element_type=F32)
            gam_t = jnp.broadcast_to(gam[:, kcols], (c, GLA_DK)).T
            st_ref[h] = st * jnp.concatenate([gam_t] * (GLA_DV // GLA_DK), axis=1) + both[c:]
            o = o_inter + both[0:c]
            o = o * lax.rsqrt(jnp.mean(o * o, axis=-1, keepdims=True) + RMS_EPS) * ng_ref[:, vcols]
            o_ref[rows, vcols] = o.astype(BF16) * _silu(g_ref[rows, vcols])
        return carry

    lax.fori_loop(0, la_ref.shape[0] // c, chunk, 0)


N_PROJ = 5


def _gla_layer_kernel(xp_ref, xr_ref, w_ref, walr_ref, wa2_ref, ba2_ref, ng_ref, wout_ref, lg_ref, lb_ref,
                      wkv_ref, swin_ref, swout_ref, out_ref, wall_ref, wout1_ref, st_ref, *scratch,
                      blocks_per_seq):
    bufs = (scratch[0:N_PROJ], scratch[N_PROJ:2 * N_PROJ])
    xb_s, rec_s = scratch[2 * N_PROJ], scratch[2 * N_PROJ + 1:-5]
    pair_s, st0_s, og_s = scratch[-5:-2], scratch[-2], scratch[-1]
    j = pl.program_id(0)
    wall_ref[:, 0:SWA_KV_TOTAL] = wkv_ref[...].astype(BF16)
    wall_ref[:, SWA_KV_TOTAL:] = swin_ref[0].astype(BF16)
    wout1_ref[...] = swout_ref[0].astype(BF16)

    @pl.when(j == 0)
    def _():
        for ref in bufs[1]:
            ref[...] = jnp.zeros_like(ref)

    @pl.when((j == 0) | (j % blocks_per_seq == 1 % blocks_per_seq))
    def _():
        st_ref[...] = jnp.zeros_like(st_ref)

    def step(write, read):
        totals = jnp.sum(read[N_PROJ - 1][...].reshape(-1, GLA_SUB, GLA_DK_TOTAL), axis=1)
        out_of_range = jnp.min(totals) < GLA_FAST_MIN_LOG_DECAY
        st0_s[...] = st_ref[...]

        pieces = _gla_project_pieces(xp_ref, w_ref, walr_ref, wa2_ref, ba2_ref, xb_s, *write)
        n_pieces = len(pieces) - GLA_TAIL_PIECES
        n_stages = (xb_s.shape[0] // GLA_CHUNK) * (1 + 2 * (GLA_HEADS // GLA_HEAD_GROUP))
        for i, _ in enumerate(_gla_recur_pieces(*read, ng_ref, og_s, st_ref, *rec_s)):
            issued_after = ((i + 1) * n_pieces + n_stages - 1) // n_stages
            while n_pieces + GLA_TAIL_PIECES - len(pieces) < issued_after:
                pieces.pop(0)()

        def project_out(beside=()):
            y = jnp.dot(og_s[...], wout_ref[...], preferred_element_type=F32)
            for piece in beside:
                piece()
            out_ref[0] = _layer_norm(DEEPNORM_ALPHA * xr_ref[0] + y, lg_ref[0:1, :], lb_ref[0:1, :])

        project_out(pieces)

        @pl.when(out_of_range)
        def _():
            st_ref[...] = st0_s[...]
            _gla_recur_pairwise(*read, ng_ref, og_s, st_ref, *pair_s)
            project_out()

    @pl.when(j % 2 == 0)
    def _():
        step(bufs[0], bufs[1])

    @pl.when(j % 2 == 1)
    def _():
        step(bufs[1], bufs[0])


def _gla_layer(x, w_main, w_alr, w_a2, b_a2, norm_g, w_out, ln_g, ln_b, w_kv, swa_w_in, swa_w_out):
    bsz, s, _ = x.shape
    t = GLA_BLOCK
    n_blk = s // t
    n_chunks = t // GLA_CHUNK
    rows = lambda n, dt: pltpu.VMEM((t, n), dt)
    proj_bufs = [rows(GLA_DK_TOTAL, BF16), rows(GLA_DK_TOTAL, BF16), rows(GLA_DV_TOTAL, BF16),
                 rows(GLA_DV_TOTAL, BF16), rows(GLA_DK_TOTAL, F32)]
    assert len(proj_bufs) == N_PROJ
    n_all = bsz * n_blk
    xb = x.reshape(n_all, t, D_MODEL)
    cur = pl.BlockSpec((1, t, D_MODEL), lambda j: (jnp.minimum(j, n_all - 1), 0, 0))
    prev = pl.BlockSpec((1, t, D_MODEL), lambda j: (jnp.maximum(j - 1, 0), 0, 0))
    consts = (w_main, w_alr, w_a2, b_a2, norm_g, w_out, ln_g, ln_b)
    wr = D_MODEL // n_all
    assert wr * n_all == D_MODEL and wr % BF16_ROWS == 0
    wrows2 = lambda n: pl.BlockSpec((wr, n), lambda j: (jnp.minimum(j, n_all - 1), 0))
    wrows3 = lambda n: pl.BlockSpec((1, wr, n), lambda j: (0, jnp.minimum(j, n_all - 1), 0))
    n_in = w_kv.shape[1] + swa_w_in.shape[2]
    return pl.pallas_call(
        functools.partial(_gla_layer_kernel, blocks_per_seq=n_blk),
        grid=(n_all + 1,),
        in_specs=[cur, prev] + [_full(a.shape) for a in consts]
        + [wrows2(w_kv.shape[1]), wrows3(swa_w_in.shape[2]), wrows3(D_MODEL)],
        out_specs=[prev, wrows2(n_in), wrows2(D_MODEL)],
        out_shape=[jax.ShapeDtypeStruct(xb.shape, F32), jax.ShapeDtypeStruct((D_MODEL, n_in), BF16),
                   jax.ShapeDtypeStruct((D_MODEL, D_MODEL), BF16)],
        scratch_shapes=[pltpu.VMEM((GLA_HEADS, GLA_DK, GLA_DV), F32)] + proj_bufs + proj_bufs
        + [rows(D_MODEL, BF16)] + [rows(GLA_DK_TOTAL, BF16)] * 4
        + [pltpu.VMEM((n_chunks, GLA_HEADS, GLA_DK, GLA_CHUNK), BF16),
           pltpu.VMEM((n_chunks, GLA_HEADS, GLA_DK, GLA_CHUNK), F32)]
        + [pltpu.VMEM((GLA_CHUNK, GLA_DK_TOTAL), F32)] * 3 + [pltpu.VMEM((GLA_HEADS, GLA_DK, GLA_DV), F32)]
        + [rows(GLA_DV_TOTAL, BF16)],
        compiler_params=_params("arbitrary"),
        name="gla_layer",
    )(xb, xb, *consts, w_kv, swa_w_in, swa_w_out)


def _unit_shape(bsz, seq, d, width):
    return (bsz, seq // SWA_SPAN, SWA_SPAN // (SWA_BLOCK * d), d, SWA_BLOCK, width)


def _tile_unit_spec(d, width):
    rows_per = ROW_TILE // d
    if rows_per >= SWA_BLOCK:
        jb = rows_per // SWA_BLOCK
        return pl.BlockSpec((1, 1, jb, d, SWA_BLOCK, width), lambda b, n, t: (b, n, t, 0, 0, 0))
    per_j = SWA_BLOCK // rows_per
    return pl.BlockSpec((1, 1, 1, d, rows_per, width), lambda b, n, t: (b, n, t // per_j, 0, t % per_j, 0))


def _tile_unit_rows(d):
    rows_per = ROW_TILE // d
    return max(rows_per // SWA_BLOCK, 1), min(rows_per, SWA_BLOCK)


def _token_rows(d, jj, r, rows):
    return pl.ds(jj * SWA_BLOCK * d + r, rows, stride=d) if d > 1 else pl.ds(jj * SWA_BLOCK, rows)


def _rope(h, cos, sin_signed):
    return h * cos + pltpu.roll(h, SWA_HEAD_DIM // 2, 1) * sin_signed


def _swa_in_kernel(x_ref, w_ref, cos_base_ref, sin_base_ref, cos_off_ref, sin_off_ref, *refs):
    k_refs = refs[0:SWA_N_GROUPS]
    v_refs = refs[SWA_N_GROUPS:2 * SWA_N_GROUPS]
    q_refs = refs[2 * SWA_N_GROUPS:3 * SWA_N_GROUPS]
    g_ref = refs[3 * SWA_N_GROUPS]
    tok_s = refs[3 * SWA_N_GROUPS + 1]
    rope_s = refs[3 * SWA_N_GROUPS + 2]
    xb = x_ref[...].astype(BF16)
    e = SWA_HEAD_DIM
    tile = pl.ds(pl.program_id(1) * TILES_PER_SPAN + pl.program_id(2), 1)
    ca, sa = cos_base_ref[tile, :], sin_base_ref[tile, :]
    cb, sb = cos_off_ref[...], sin_off_ref[...]
    cos = ca * cb - sa * sb
    lane = lax.broadcasted_iota(jnp.int32, cb.shape, 1)
    sin = jnp.where(lane < e // 2, -1.0, 1.0) * (sa * cb + ca * sb)
    rope_s[0], rope_s[1] = cos, sin
    rope_s[2], rope_s[3] = cos * (e ** -0.5), sin * (e ** -0.5)
    kv_half = SWA_N_GROUPS * SWA_KV_W
    q0 = 2 * kv_half

    n_slabs = tok_s.shape[0]
    pipeline = {"slab": 0, "pending": None}

    def head_pair(col, out_ref, head0, d, rope):
        y = jnp.dot(xb, w_ref[:, col:col + MXU_COLS], preferred_element_type=F32)
        jb, rows = _tile_unit_rows(d)
        staged = []
        for i in range(MXU_COLS // e):
            h = head0 + i
            yh = y[:, i * e:(i + 1) * e]
            if rope is not None:
                yh = _rope(yh, rope_s[rope], rope_s[rope + 1])
            if d == 1:
                out_ref[0, 0, :, :, :, h * e:(h + 1) * e] = yh.astype(BF16).reshape(jb, d, rows, e)
                continue
            slab = pipeline["slab"]
            pipeline["slab"] = (slab + 1) % n_slabs
            tok_s[slab] = yh
            staged.append((slab, h))

        def scatter():
            for slab, h in staged:
                for jj in range(jb):
                    for r in range(d):
                        out_ref[0, 0, jj, r, :, h * e:(h + 1) * e] = (
                            tok_s[slab, _token_rows(d, jj, r, rows), :].astype(BF16))

        flush()
        pipeline["pending"] = scatter if staged else None

    def flush():
        if pipeline["pending"] is not None:
            pipeline["pending"]()
            pipeline["pending"] = None

    g0 = q0 + SWA_N_GROUPS * SWA_OUT
    gate_tiles = list(range(0, SWA_OUT, MXU_COLS))

    def gate_tile():
        c = gate_tiles.pop(0)
        g_ref[:, c:c + MXU_COLS] = jnp.dot(xb, w_ref[:, g0 + c:g0 + c + MXU_COLS],
                                           preferred_element_type=F32).astype(BF16)

    def group_tiles(gi, d):
        yield lambda: head_pair(gi * SWA_KV_W, k_refs[gi], 0, d, 0)
        yield lambda: head_pair(kv_half + gi * SWA_KV_W, v_refs[gi], 0, d, None)
        for h0 in range(0, SWA_Q_HEADS, MXU_COLS // e):
            yield functools.partial(head_pair, q0 + gi * SWA_OUT + h0 * e, q_refs[gi], h0, d, 2)

    for tiles in zip(*[list(group_tiles(gi, d)) for gi, (_, d) in enumerate(SWA_GROUPS)]):
        for tile in tiles:
            tile()
        if gate_tiles:
            gate_tile()
    while gate_tiles:
        gate_tile()
    flush()


def _swa_in_proj(x2, w_all, rope_tables, bsz, seq):
    tm = ROW_TILE
    n_span = seq // SWA_SPAN
    tile = lambda b, n, t: (b * n_span + n) * TILES_PER_SPAN + t
    row = lambda w: pl.BlockSpec((tm, w), lambda b, n, t: (tile(b, n, t), 0))
    dils = [d for _, d in SWA_GROUPS]
    unit_out = [(d, SWA_KV_W) for d in dils] * 2 + [(d, SWA_OUT) for d in dils]
    outs = pl.pallas_call(
        _swa_in_kernel,
        grid=(bsz, n_span, TILES_PER_SPAN),
        in_specs=[row(D_MODEL), _full(w_all.shape)] + [_full(a.shape) for a in rope_tables],
        out_specs=[_tile_unit_spec(d, w) for d, w in unit_out] + [row(SWA_OUT)],
        out_shape=[jax.ShapeDtypeStruct(_unit_shape(bsz, seq, d, w), BF16) for d, w in unit_out]
        + [jax.ShapeDtypeStruct((bsz * seq, SWA_OUT), BF16)],
        scratch_shapes=[pltpu.VMEM((SWA_OUT // LANES, tm, LANES), F32), pltpu.VMEM((4, tm, SWA_HEAD_DIM), F32)],
        compiler_params=_params("parallel", "parallel", "parallel"),
        name="swa_in_proj",
    )(x2, w_all, *rope_tables)
    return outs[0:3], outs[3:6], outs[6:9], outs[9]


def _swa_kernel(q_ref, kc_ref, kp_ref, vc_ref, vp_ref, o_ref, st_ref, k_s, v_s, bias_s):
    blk = SWA_BLOCK
    e = SWA_HEAD_DIM
    n_j, d = q_ref.shape[2], q_ref.shape[3]
    n = pl.program_id(1)

    k_s[0] = kp_ref[0, 0, 0]
    k_s[1:] = kc_ref[0, 0]
    v_s[0] = vp_ref[0, 0, 0]
    v_s[1:] = vc_ref[0, 0]

    qi = lax.broadcasted_iota(jnp.int32, (blk, 2 * blk), 0)
    kj = lax.broadcasted_iota(jnp.int32, (blk, 2 * blk), 1)
    rel = blk + qi - kj
    band = (rel >= 0) & (rel <= blk)
    bias_s[0] = jnp.where(band, 0.0, -jnp.inf)
    bias_s[1] = jnp.where(band & (kj >= blk), 0.0, -jnp.inf)
    lane_slot = lax.broadcasted_iota(jnp.int32, (blk, LANES), 1) // SWA_STAT_LANES
    ones = jnp.ones((2 * blk, LANES), BF16)

    def unit(idx, carry):
        j = idx // d
        r = idx % d
        bias = bias_s[jnp.where((j == 0) & (n == 0), 1, 0)]
        st_tile = jnp.zeros((blk, LANES), F32)
        for kvh in range(SWA_KV_HEADS):
            q4 = q_ref[0, 0, j, r, :, kvh * SWA_REP * e:(kvh + 1) * SWA_REP * e]
            qs = jnp.concatenate([q4[:, h * e:(h + 1) * e] for h in range(SWA_REP)], axis=0)
            kcols = slice(kvh * e, (kvh + 1) * e)
            kk = jnp.concatenate([k_s[j, r, :, kcols], k_s[j + 1, r, :, kcols]], axis=0)
            vv = jnp.concatenate([v_s[j, r, :, kcols], v_s[j + 1, r, :, kcols]], axis=0)
            s = lax.dot_general(qs, kk, _NT, preferred_element_type=F32)
            s = (s.reshape(SWA_REP, blk, 2 * blk) + bias).reshape(SWA_REP * blk, 2 * blk)
            m = jnp.max(s, axis=-1, keepdims=True)
            p = jnp.exp(s - m).astype(BF16)
            o = jnp.dot(p, jnp.concatenate([vv, ones], axis=1), preferred_element_type=F32)
            for h in range(SWA_REP):
                head = kvh * SWA_REP + h
                o_ref[0, 0, j, r, :, head * e:(head + 1) * e] = o[h * blk:(h + 1) * blk, 0:e].astype(BF16)
                st_tile = jnp.where(lane_slot == 2 * head, m[h * blk:(h + 1) * blk], st_tile)
                st_tile = jnp.where(lane_slot == 2 * head + 1, o[h * blk:(h + 1) * blk, e:], st_tile)
        st_ref[0, 0, j, r] = st_tile
        return carry

    lax.fori_loop(0, n_j * d, unit, 0, unroll=True)


def _swa_attention(q, k, v):
    bsz, n_span, n_j, d, blk, _ = q.shape
    cur = lambda w: pl.BlockSpec((1, 1, n_j, d, blk, w), lambda b, n: (b, n, 0, 0, 0, 0))
    prev = pl.BlockSpec((1, 1, 1, d, blk, SWA_KV_W), lambda b, n: (b, jnp.maximum(n - 1, 0), n_j - 1, 0, 0, 0))
    stat_shape = q.shape[:-1] + (LANES,)
    return pl.pallas_call(
        _swa_kernel,
        grid=(bsz, n_span),
        in_specs=[cur(SWA_OUT), cur(SWA_KV_W), prev, cur(SWA_KV_W), prev],
        out_specs=[cur(SWA_OUT), cur(LANES)],
        out_shape=[jax.ShapeDtypeStruct(q.shape, BF16), jax.ShapeDtypeStruct(stat_shape, F32)],
        scratch_shapes=[pltpu.VMEM((n_j + 1, d, blk, SWA_KV_W), BF16),
                        pltpu.VMEM((n_j + 1, d, blk, SWA_KV_W), BF16),
                        pltpu.VMEM((2, blk, 2 * blk), F32)],
        compiler_params=_params("parallel", "parallel"),
        name=f"swa_attention_d{d}",
    )(q, k, k, v, v)


def _merge_out_kernel(*refs):
    n_g = SWA_N_GROUPS
    o_refs, st_refs = refs[0:n_g], refs[n_g:2 * n_g]
    g_ref, x_ref, w_ref, lg_ref, lb_ref, out_ref, stat_s, o_s = refs[2 * n_g:]
    tm = ROW_TILE

    def to_tokens(ref, d, dst):
        jb, rows = _tile_unit_rows(d)
        n_slabs = ref.shape[-1] // LANES
        lanes = lambda c: slice(c * LANES, (c + 1) * LANES)
        if d == 1:
            tok = ref[0, 0].reshape(tm, ref.shape[-1])
            return [tok[:, lanes(c)] for c in range(n_slabs)]
        for jj in range(jb):
            for r in range(d):
                for c in range(n_slabs):
                    dst[c, _token_rows(d, jj, r, rows), :] = ref[0, 0, jj, r, :, lanes(c)].astype(F32)
        return [dst[c] for c in range(n_slabs)]

    dils = [d for _, d in SWA_GROUPS]
    sts = [to_tokens(st_refs[gi], d, stat_s.at[gi:gi + 1])[0] for gi, d in enumerate(dils)]
    mx = functools.reduce(jnp.maximum, sts)
    es = [jnp.exp(st - mx) for st in sts]
    ls = [pltpu.roll(st, LANES - SWA_STAT_LANES, 1) for st in sts]
    tot = functools.reduce(jnp.add, [eg * lg for eg, lg in zip(es, ls)])
    inv_tot = 1.0 / tot
    m_slot = (lax.broadcasted_iota(jnp.int32, (tm, LANES), 1) // SWA_STAT_LANES) % 2 == 0
    src = lax.broadcasted_iota(jnp.int32, (LANES, SWA_OUT), 0)
    dst = lax.broadcasted_iota(jnp.int32, (LANES, SWA_OUT), 1)
    expand = jnp.where(src == (dst // SWA_HEAD_DIM) * 2 * SWA_STAT_LANES, 1.0, 0.0).astype(BF16)
    acc = None
    for gi, d in enumerate(dils):
        w = jnp.dot(jnp.where(m_slot, es[gi] * inv_tot, 0.0).astype(BF16), expand, preferred_element_type=F32)
        term = w.astype(BF16) * jnp.concatenate(to_tokens(o_refs[gi], d, o_s), axis=1).astype(BF16)
        acc = term if acc is None else acc + term
    a = acc * _silu(g_ref[...])
    y = jnp.dot(a, w_ref[...], preferred_element_type=F32)
    out_ref[...] = _layer_norm(DEEPNORM_ALPHA * x_ref[...] + y, lg_ref[1:2, :], lb_ref[1:2, :])


def _merge_out_ln(os, sts, g, x2, w_out, ln_g, ln_b, bsz, seq):
    tm = ROW_TILE
    n_span = seq // SWA_SPAN
    row = lambda w: pl.BlockSpec((tm, w), lambda b, n, t: ((b * n_span + n) * TILES_PER_SPAN + t, 0))
    dils = [d for _, d in SWA_GROUPS]
    return pl.pallas_call(
        _merge_out_kernel,
        grid=(bsz, n_span, TILES_PER_SPAN),
        in_specs=[_tile_unit_spec(d, SWA_OUT) for d in dils] + [_tile_unit_spec(d, LANES) for d in dils]
        + [row(SWA_OUT), row(D_MODEL), _full(w_out.shape), _full(ln_g.shape), _full(ln_b.shape)],
        out_specs=row(D_MODEL),
        out_shape=jax.ShapeDtypeStruct((bsz * seq, D_MODEL), F32),
        scratch_shapes=[pltpu.VMEM((SWA_N_GROUPS, tm, LANES), F32),
                        pltpu.VMEM((SWA_OUT // LANES, tm, LANES), F32)],
        compiler_params=_params("parallel", "parallel", "parallel"),
        name="merge_out_ln",
    )(*os, *sts, g, x2, w_out, ln_g, ln_b)


def _rope_tables(seq):
    half = SWA_HEAD_DIM // 2
    inv = ROPE_THETA ** (-(jnp.arange(half, dtype=F32) * 2.0) / SWA_HEAD_DIM)
    inv = jnp.concatenate([inv, inv])[None, :]
    base = (jnp.arange(seq // ROW_TILE) * ROW_TILE).astype(F32)[:, None] * inv
    off = jnp.arange(ROW_TILE).astype(F32)[:, None] * inv
    return jnp.cos(base), jnp.sin(base), jnp.cos(off), jnp.sin(off)


GLA_N_MAIN = 2 * GLA_DK_TOTAL + 2 * GLA_DV_TOTAL
CAST_ROWS = 128


def _cast_weights_kernel(gw_in_t, gw_a2, gw_out, w_main, w_alr, w_a2, w_out0):
    r = CAST_ROWS
    t = gw_in_t[0]
    for c in range(0, GLA_N_MAIN, r):
        w_main[:, c:c + r] = t[c:c + r, :].T.astype(BF16)
    tail = t[GLA_N_MAIN + GLA_GATE_RANK - r:, :].T
    lane = lax.broadcasted_iota(jnp.int32, (r, r), 1)
    w_alr[...] = jnp.where(lane >= r - GLA_GATE_RANK, tail, 0.0).astype(BF16)
    w_a2[...] = jnp.zeros_like(w_a2)
    w_a2[LANES - GLA_GATE_RANK:, :] = gw_a2[0].astype(BF16)
    w_out0[...] = gw_out[0].astype(BF16)


def _cast_weights(gla_w_in, gla_w_a2, gla_w_out):
    r = CAST_ROWS
    assert r == LANES
    rows3 = lambda n: pl.BlockSpec((1, r, n), lambda i: (0, i, 0))
    rows2 = lambda n: pl.BlockSpec((r, n), lambda i: (i, 0))
    gla_w_in_t = jnp.swapaxes(gla_w_in, 1, 2)
    cols_t = pl.BlockSpec((1, gla_w_in_t.shape[1], r), lambda i: (0, 0, i))
    out_widths = [GLA_N_MAIN, LANES, None, D_MODEL]
    out_shapes = [(D_MODEL, w) if w else (LANES, GLA_DK_TOTAL) for w in out_widths]
    return pl.pallas_call(
        _cast_weights_kernel,
        grid=(D_MODEL // r,),
        in_specs=[cols_t, _full(gla_w_a2.shape), rows3(D_MODEL)],
        out_specs=[rows2(w) if w else pl.BlockSpec((LANES, GLA_DK_TOTAL), lambda i: (0, 0)) for w in out_widths],
        out_shape=[jax.ShapeDtypeStruct(s, BF16) for s in out_shapes],
        compiler_params=_params("arbitrary"),
        name="cast_weights",
    )(gla_w_in_t, gla_w_a2, gla_w_out)


def kernel(x, gla_w_in, gla_w_a2, gla_b_a2, gla_norm_g, gla_w_out, w_kv, swa_w_in, swa_w_out, ln_g, ln_b):
    bsz, seq, _ = x.shape
    assert seq % SWA_SPAN == 0 and seq % GLA_BLOCK == 0
    assert gla_w_in.shape[0] == 1 and swa_w_in.shape[0] == 1 and ln_g.shape[0] == DEPTH
    m = bsz * seq

    w_main, w_alr, w_a2, w_out0 = _cast_weights(gla_w_in, gla_w_a2, gla_w_out)
    x2, w_all, w_out1 = _gla_layer(x, w_main, w_alr, w_a2, gla_b_a2, gla_norm_g, w_out0, ln_g, ln_b,
                                   w_kv, swa_w_in, swa_w_out)
    x2 = x2.reshape(m, D_MODEL)

    ks, vs, qs, g = _swa_in_proj(x2, w_all, _rope_tables(seq), bsz, seq)
    os, sts = zip(*[_swa_attention(qs[gi], ks[gi], vs[gi]) for gi in range(SWA_N_GROUPS)])
    out = _merge_out_ln(os, sts, g, x2, w_out1, ln_g, ln_b, bsz, seq)
    return out.reshape(bsz, seq, D_MODEL)
```

```python
import functools

import jax
import jax.numpy as jnp
from jax import lax
from jax.experimental import pallas as pl
from jax.experimental.pallas import tpu as pltpu

BF16 = jnp.bfloat16
F32 = jnp.float32

D_MODEL = 1024
DEPTH = 2
DEEPNORM_ALPHA = (2.0 * DEPTH) ** 0.25
LN_EPS = 1e-5
RMS_EPS = 1e-5

GLA_HEADS = 4
GLA_DK = 128
GLA_DV = 256
GLA_DK_TOTAL = GLA_HEADS * GLA_DK
GLA_DV_TOTAL = GLA_HEADS * GLA_DV
GLA_GATE_RANK = 16
GLA_GATE_TAU = 16.0
GLA_SUB = 64
GLA_CHUNK = 2 * GLA_SUB
GLA_BLOCK = 512
GLA_HEAD_GROUP = 4
GLA_TAIL_PIECES = 2
GLA_FAST_MIN_LOG_DECAY = -40.0

SWA_GROUPS = ((128, 1), (512, 4), (2048, 16))
SWA_N_GROUPS = len(SWA_GROUPS)
SWA_HEAD_DIM = 128
SWA_Q_HEADS = 8
SWA_KV_HEADS = 2
SWA_REP = SWA_Q_HEADS // SWA_KV_HEADS
SWA_OUT = SWA_Q_HEADS * SWA_HEAD_DIM
SWA_KV_W = SWA_KV_HEADS * SWA_HEAD_DIM
SWA_KV_TOTAL = 2 * SWA_N_GROUPS * SWA_KV_W
SWA_BLOCK = 128
SWA_SPAN = 2048
SWA_STAT_LANES = SWA_HEAD_DIM // (2 * SWA_Q_HEADS)
ROPE_THETA = 10000.0

LANES = 128
BF16_ROWS = 16
F32_ROWS = 8
MXU_COLS = 256
ROW_TILE = 1024
TILES_PER_SPAN = SWA_SPAN // ROW_TILE
VMEM_LIMIT = 56 * 1024 * 1024

_NT = (((1,), (1,)), ((), ()))


def _params(*sem, flags=None):
    return pltpu.CompilerParams(dimension_semantics=sem, vmem_limit_bytes=VMEM_LIMIT, flags=flags)


def _full(shape):
    return pl.BlockSpec(shape, lambda *_: (0,) * len(shape), pipeline_mode=pl.Buffered(1))


def _layer_norm(h, g, b):
    mu = jnp.mean(h, axis=-1, keepdims=True)
    hc = h - mu
    var = jnp.mean(hc * hc, axis=-1, keepdims=True)
    return hc * lax.rsqrt(var + LN_EPS) * g + b


def _silu(g):
    return g * (0.5 + 0.5 * jnp.tanh(0.5 * g))


def _gla_project_pieces(x_ref, w_ref, walr_ref, wa2_ref, ba2_ref, xb_s, q_ref, k_ref, v_ref, g_ref, la_ref):
    xb_s[...] = x_ref[0].astype(BF16)

    def tile(dst, col, lo, scale):
        def piece():
            y = jnp.dot(xb_s[...], w_ref[:, col + lo:col + lo + MXU_COLS], preferred_element_type=F32)
            dst[:, lo:lo + MXU_COLS] = (y if scale is None else y * scale).astype(BF16)
        return piece

    def gate_piece():
        a_lr = jnp.dot(xb_s[...], walr_ref[...], preferred_element_type=F32)
        z = jnp.dot(a_lr.astype(BF16), wa2_ref[...], preferred_element_type=F32) + ba2_ref[...]
        log_sig = jnp.minimum(z, 0.0) - jnp.log(1.0 + jnp.exp(-jnp.abs(z)))
        la_ref[...] = log_sig * (1.0 / GLA_GATE_TAU)

    pieces, col = [], 0
    for dst, scale in ((q_ref, GLA_DK ** -0.5), (k_ref, None), (v_ref, None), (g_ref, None)):
        pieces += [tile(dst, col, lo, scale) for lo in range(0, dst.shape[1], MXU_COLS)]
        col += dst.shape[1]
    return pieces[:1] + [gate_piece] + pieces[1:]


def _gla_recur_pieces(q_ref, k_ref, v_ref, g_ref, la_ref, ng_ref, o_ref,
                      st_ref, qe_s, qi_s, kd_s, kx_s, klt_s, gam_s):
    t = la_ref.shape[0]
    c, sub = GLA_CHUNK, GLA_SUB

    row = lax.broadcasted_iota(jnp.int32, (c, c), 0)
    col = lax.broadcasted_iota(jnp.int32, (c, c), 1)
    causal = col <= row
    tri = jnp.where(causal & (row // sub == col // sub), 1.0, 0.0).astype(BF16)
    halves = lambda first, second: jnp.concatenate([first, second], axis=0)

    for ic in range(t // c):
        rows = slice(ic * c, (ic + 1) * c)
        first = slice(ic * c, ic * c + sub)
        second = slice(ic * c + sub, (ic + 1) * c)

        la = la_ref[rows, :]
        la_hi = la.astype(BF16)
        la_lo = (la - la_hi.astype(F32)).astype(BF16)
        bs = jnp.dot(tri, la_hi, preferred_element_type=F32) + jnp.dot(tri, la_lo, preferred_element_type=F32)
        e_a = jnp.exp(bs[sub - 1:sub, :])
        e_b = jnp.exp(bs[c - 1:c, :])
        qi = q_ref[rows, :].astype(F32) * jnp.exp(bs)
        kd = k_ref[rows, :].astype(F32) * jnp.exp(-bs)
        kl_a, kl_b = kd[:sub] * e_a, kd[sub:] * e_b
        qi_s[rows, :] = qi.astype(BF16)
        kd_s[rows, :] = kd.astype(BF16)
        qe_s[rows, :] = halves(qi[:sub], qi[sub:] * e_a).astype(BF16)
        kx_s[rows, :] = halves(kl_a, kd[sub:]).astype(BF16)
        klc = halves(kl_a * e_b, kl_b)
        gam = e_a * e_b
        for h in range(GLA_HEADS):
            kcols = slice(h * GLA_DK, (h + 1) * GLA_DK)
            klt_s[ic, h] = klc[:, kcols].T.astype(BF16)
            gam_s[ic, h] = jnp.broadcast_to(gam[:, kcols], (c, GLA_DK)).T
        yield

        for h0 in range(0, GLA_HEADS, GLA_HEAD_GROUP):
            stage1 = []
            for h in range(h0, h0 + GLA_HEAD_GROUP):
                kcols = slice(h * GLA_DK, (h + 1) * GLA_DK)
                st = st_ref[h]
                o_inter = jnp.dot(qe_s[rows, kcols], st.astype(BF16), preferred_element_type=F32)
                att = jnp.concatenate(
                    [lax.dot_general(qi_s[first, kcols], kd_s[rows, kcols], _NT, preferred_element_type=F32),
                     lax.dot_general(qi_s[second, kcols], kx_s[rows, kcols], _NT, preferred_element_type=F32)],
                    axis=0)
                stage1.append((st, o_inter, att))
            yield
            for h, (st, o_inter, att) in zip(range(h0, h0 + GLA_HEAD_GROUP), stage1):
                vcols = slice(h * GLA_DV, (h + 1) * GLA_DV)
                att = jnp.where(causal, att, 0.0).astype(BF16)
                both = jnp.dot(jnp.concatenate([att, klt_s[ic, h]], axis=0), v_ref[rows, vcols],
                               preferred_element_type=F32)
                gam_t = gam_s[ic, h]
                st_ref[h] = st * jnp.concatenate([gam_t] * (GLA_DV // GLA_DK), axis=1) + both[c:]
                o = o_inter + both[0:c]
                o = o * lax.rsqrt(jnp.mean(o * o, axis=-1, keepdims=True) + RMS_EPS) * ng_ref[:, vcols]
                o_ref[rows, vcols] = o.astype(BF16) * _silu(g_ref[rows, vcols])
            yield


def _pairwise_scores(b_s, qf_s, kf_s, kcols):
    c = GLA_CHUNK
    bj = b_s[:, kcols]
    kj = kf_s[:, kcols]
    j_idx = lax.broadcasted_iota(jnp.int32, (c, 1), 0)
    lane = lax.broadcasted_iota(jnp.int32, (c, c), 1)

    def query_rows(g, att_t):
        base = pl.multiple_of(g * F32_ROWS, F32_ROWS)
        b_g = b_s[pl.ds(base, F32_ROWS), kcols]
        q_g = qf_s[pl.ds(base, F32_ROWS), kcols]
        for r in range(F32_ROWS):
            i = g * F32_ROWS + r
            w = jnp.exp(jnp.minimum(b_g[r:r + 1] - bj, 0.0)) * kj * q_g[r:r + 1]
            col = jnp.sum(w, axis=-1, keepdims=True)
            att_t = jnp.where(lane == i, jnp.where(j_idx <= i, col, 0.0), att_t)
        return att_t

    return lax.fori_loop(0, c // F32_ROWS, query_rows, jnp.zeros((c, c), F32)).T


def _gla_recur_pairwise(q_ref, k_ref, v_ref, g_ref, la_ref, ng_ref, o_ref, st_ref, b_s, qf_s, kf_s):
    c, sub = GLA_CHUNK, GLA_SUB
    row = lax.broadcasted_iota(jnp.int32, (c, c), 0)
    col = lax.broadcasted_iota(jnp.int32, (c, c), 1)
    tri = jnp.where((col <= row) & (row // sub == col // sub), 1.0, 0.0).astype(BF16)

    def chunk(ic, carry):
        rows = pl.ds(pl.multiple_of(ic * c, c), c)
        la = la_ref[rows, :]
        la_hi = la.astype(BF16)
        la_lo = (la - la_hi.astype(F32)).astype(BF16)
        bs = jnp.dot(tri, la_hi, preferred_element_type=F32) + jnp.dot(tri, la_lo, preferred_element_type=F32)
        b = jnp.concatenate([bs[:sub], bs[sub:] + bs[sub - 1:sub, :]], axis=0)
        qf, kf = q_ref[rows, :].astype(F32), k_ref[rows, :].astype(F32)
        b_s[...], qf_s[...], kf_s[...] = b, qf, kf
        qe = (qf * jnp.exp(b)).astype(BF16)
        kl = kf * jnp.exp(b[c - 1:c, :] - b)
        gam = jnp.exp(b[c - 1:c, :])
        for h in range(GLA_HEADS):
            kcols = slice(h * GLA_DK, (h + 1) * GLA_DK)
            vcols = slice(h * GLA_DV, (h + 1) * GLA_DV)
            st = st_ref[h]
            att = _pairwise_scores(b_s, qf_s, kf_s, kcols).astype(BF16)
            o_inter = jnp.dot(qe[:, kcols], st.astype(BF16), preferred_element_type=F32)
            both = jnp.dot(jnp.concatenate([att, kl[:, kcols].T.astype(BF16)], axis=0), v_ref[rows, vcols],
                           preferred_element_type=F32)
            gam_t = jnp.broadcast_to(gam[:, kcols], (c, GLA_DK)).T
            st_ref[h] = st * jnp.concatenate([gam_t] * (GLA_DV // GLA_DK), axis=1) + both[c:]
            o = o_inter + both[0:c]
            o = o * lax.rsqrt(jnp.mean(o * o, axis=-1, keepdims=True) + RMS_EPS) * ng_ref[:, vcols]
            o_ref[rows, vcols] = o.astype(BF16) * _silu(g_ref[rows, vcols])
        return carry

    lax.fori_loop(0, la_ref.shape[0] // c, chunk, 0)


N_PROJ = 5


def _gla_layer_kernel(xp_ref, xr_ref, w_ref, walr_ref, wa2_ref, ba2_ref, ng_ref, wout_ref, lg_ref, lb_ref,
                      wkv_ref, swin_ref, swout_ref, out_ref, wall_ref, wout1_ref, st_ref, *scratch,
                      blocks_per_seq):
    bufs = (scratch[0:N_PROJ], scratch[N_PROJ:2 * N_PROJ])
    xb_s, rec_s = scratch[2 * N_PROJ], scratch[2 * N_PROJ + 1:-5]
    pair_s, st0_s, og_s = scratch[-5:-2], scratch[-2], scratch[-1]
    j = pl.program_id(0)
    wall_ref[:, 0:SWA_KV_TOTAL] = wkv_ref[...].astype(BF16)
    wall_ref[:, SWA_KV_TOTAL:] = swin_ref[0].astype(BF16)
    wout1_ref[...] = swout_ref[0].astype(BF16)

    @pl.when(j == 0)
    def _():
        for ref in bufs[1]:
            ref[...] = jnp.zeros_like(ref)

    @pl.when((j == 0) | (j % blocks_per_seq == 1 % blocks_per_seq))
    def _():
        st_ref[...] = jnp.zeros_like(st_ref)

    def step(write, read):
        totals = jnp.sum(read[N_PROJ - 1][...].reshape(-1, GLA_SUB, GLA_DK_TOTAL), axis=1)
        out_of_range = jnp.min(totals) < GLA_FAST_MIN_LOG_DECAY
        st0_s[...] = st_ref[...]

        pieces = _gla_project_pieces(xp_ref, w_ref, walr_ref, wa2_ref, ba2_ref, xb_s, *write)
        n_pieces = len(pieces) - GLA_TAIL_PIECES
        n_stages = (xb_s.shape[0] // GLA_CHUNK) * (1 + 2 * (GLA_HEADS // GLA_HEAD_GROUP))
        for i, _ in enumerate(_gla_recur_pieces(*read, ng_ref, og_s, st_ref, *rec_s)):
            issued_after = ((i + 1) * n_pieces + n_stages - 1) // n_stages
            while n_pieces + GLA_TAIL_PIECES - len(pieces) < issued_after:
                pieces.pop(0)()

        def project_out(beside=()):
            y = jnp.dot(og_s[...], wout_ref[...], preferred_element_type=F32)
            for piece in beside:
                piece()
            out_ref[0] = _layer_norm(DEEPNORM_ALPHA * xr_ref[0] + y, lg_ref[0:1, :], lb_ref[0:1, :])

        project_out(pieces)

        @pl.when(out_of_range)
        def _():
            st_ref[...] = st0_s[...]
            _gla_recur_pairwise(*read, ng_ref, og_s, st_ref, *pair_s)
            project_out()

    @pl.when(j % 2 == 0)
    def _():
        step(bufs[0], bufs[1])

    @pl.when(j % 2 == 1)
    def _():
        step(bufs[1], bufs[0])


def _gla_layer(x, w_main, w_alr, w_a2, b_a2, norm_g, w_out, ln_g, ln_b, w_kv, swa_w_in, swa_w_out):
    bsz, s, _ = x.shape
    t = GLA_BLOCK
    n_blk = s // t
    n_chunks = t // GLA_CHUNK
    rows = lambda n, dt: pltpu.VMEM((t, n), dt)
    proj_bufs = [rows(GLA_DK_TOTAL, BF16), rows(GLA_DK_TOTAL, BF16), rows(GLA_DV_TOTAL, BF16),
                 rows(GLA_DV_TOTAL, BF16), rows(GLA_DK_TOTAL, F32)]
    assert len(proj_bufs) == N_PROJ
    n_all = bsz * n_blk
    xb = x.reshape(n_all, t, D_MODEL)
    cur = pl.BlockSpec((1, t, D_MODEL), lambda j: (jnp.minimum(j, n_all - 1), 0, 0))
    prev = pl.BlockSpec((1, t, D_MODEL), lambda j: (jnp.maximum(j - 1, 0), 0, 0))
    consts = (w_main, w_alr, w_a2, b_a2, norm_g, w_out, ln_g, ln_b)
    wr = D_MODEL // n_all
    assert wr * n_all == D_MODEL and wr % BF16_ROWS == 0
    wrows2 = lambda n: pl.BlockSpec((wr, n), lambda j: (jnp.minimum(j, n_all - 1), 0))
    wrows3 = lambda n: pl.BlockSpec((1, wr, n), lambda j: (0, jnp.minimum(j, n_all - 1), 0))
    n_in = w_kv.shape[1] + swa_w_in.shape[2]
    return pl.pallas_call(
        functools.partial(_gla_layer_kernel, blocks_per_seq=n_blk),
        grid=(n_all + 1,),
        in_specs=[cur, prev] + [_full(a.shape) for a in consts]
        + [wrows2(w_kv.shape[1]), wrows3(swa_w_in.shape[2]), wrows3(D_MODEL)],
        out_specs=[prev, wrows2(n_in), wrows2(D_MODEL)],
        out_shape=[jax.ShapeDtypeStruct(xb.shape, F32), jax.ShapeDtypeStruct((D_MODEL, n_in), BF16),
                   jax.ShapeDtypeStruct((D_MODEL, D_MODEL), BF16)],
        scratch_shapes=[pltpu.VMEM((GLA_HEADS, GLA_DK, GLA_DV), F32)] + proj_bufs + proj_bufs
        + [rows(D_MODEL, BF16)] + [rows(GLA_DK_TOTAL, BF16)] * 4
        + [pltpu.VMEM((n_chunks, GLA_HEADS, GLA_DK, GLA_CHUNK), BF16),
           pltpu.VMEM((n_chunks, GLA_HEADS, GLA_DK, GLA_CHUNK), F32)]
        + [pltpu.VMEM((GLA_CHUNK, GLA_DK_TOTAL), F32)] * 3 + [pltpu.VMEM((GLA_HEADS, GLA_DK, GLA_DV), F32)]
        + [rows(GLA_DV_TOTAL, BF16)],
        compiler_params=_params("arbitrary"),
        name="gla_layer",
    )(xb, xb, *consts, w_kv, swa_w_in, swa_w_out)


def _unit_shape(bsz, seq, d, width):
    return (bsz, seq // SWA_SPAN, SWA_SPAN // (SWA_BLOCK * d), d, SWA_BLOCK, width)


def _tile_unit_spec(d, width):
    rows_per = ROW_TILE // d
    if rows_per >= SWA_BLOCK:
        jb = rows_per // SWA_BLOCK
        return pl.BlockSpec((1, 1, jb, d, SWA_BLOCK, width), lambda b, n, t: (b, n, t, 0, 0, 0))
    per_j = SWA_BLOCK // rows_per
    return pl.BlockSpec((1, 1, 1, d, rows_per, width), lambda b, n, t: (b, n, t // per_j, 0, t % per_j, 0))


def _tile_unit_rows(d):
    rows_per = ROW_TILE // d
    return max(rows_per // SWA_BLOCK, 1), min(rows_per, SWA_BLOCK)


def _token_rows(d, jj, r, rows):
    return pl.ds(jj * SWA_BLOCK * d + r, rows, stride=d) if d > 1 else pl.ds(jj * SWA_BLOCK, rows)


def _rope(h, cos, sin_signed):
    return h * cos + pltpu.roll(h, SWA_HEAD_DIM // 2, 1) * sin_signed


def _swa_in_kernel(x_ref, w_ref, cos_base_ref, sin_base_ref, cos_off_ref, sin_off_ref, *refs):
    k_refs = refs[0:SWA_N_GROUPS]
    v_refs = refs[SWA_N_GROUPS:2 * SWA_N_GROUPS]
    q_refs = refs[2 * SWA_N_GROUPS:3 * SWA_N_GROUPS]
    g_ref = refs[3 * SWA_N_GROUPS]
    tok_s = refs[3 * SWA_N_GROUPS + 1]
    rope_s = refs[3 * SWA_N_GROUPS + 2]
    xb = x_ref[...].astype(BF16)
    e = SWA_HEAD_DIM
    tile = pl.ds(pl.program_id(1) * TILES_PER_SPAN + pl.program_id(2), 1)
    ca, sa = cos_base_ref[tile, :], sin_base_ref[tile, :]
    cb, sb = cos_off_ref[...], sin_off_ref[...]
    cos = ca * cb - sa * sb
    lane = lax.broadcasted_iota(jnp.int32, cb.shape, 1)
    sin = jnp.where(lane < e // 2, -1.0, 1.0) * (sa * cb + ca * sb)
    rope_s[0], rope_s[1] = cos, sin
    rope_s[2], rope_s[3] = cos * (e ** -0.5), sin * (e ** -0.5)
    kv_half = SWA_N_GROUPS * SWA_KV_W
    q0 = 2 * kv_half

    n_slabs = tok_s.shape[0]
    pipeline = {"slab": 0, "pending": None}

    def head_pair(col, out_ref, head0, d, rope):
        y = jnp.dot(xb, w_ref[:, col:col + MXU_COLS], preferred_element_type=F32)
        jb, rows = _tile_unit_rows(d)
        staged = []
        for i in range(MXU_COLS // e):
            h = head0 + i
            yh = y[:, i * e:(i + 1) * e]
            if rope is not None:
                yh = _rope(yh, rope_s[rope], rope_s[rope + 1])
            if d == 1:
                out_ref[0, 0, :, :, :, h * e:(h + 1) * e] = yh.astype(BF16).reshape(jb, d, rows, e)
                continue
            slab = pipeline["slab"]
            pipeline["slab"] = (slab + 1) % n_slabs
            tok_s[slab] = yh
            staged.append((slab, h))

        def scatter():
            for slab, h in staged:
                for jj in range(jb):
                    for r in range(d):
                        out_ref[0, 0, jj, r, :, h * e:(h + 1) * e] = (
                            tok_s[slab, _token_rows(d, jj, r, rows), :].astype(BF16))

        flush()
        pipeline["pending"] = scatter if staged else None

    def flush():
        if pipeline["pending"] is not None:
            pipeline["pending"]()
            pipeline["pending"] = None

    g0 = q0 + SWA_N_GROUPS * SWA_OUT
    gate_tiles = list(range(0, SWA_OUT, MXU_COLS))

    def gate_tile():
        c = gate_tiles.pop(0)
        g_ref[:, c:c + MXU_COLS] = jnp.dot(xb, w_ref[:, g0 + c:g0 + c + MXU_COLS],
                                           preferred_element_type=F32).astype(BF16)

    def group_tiles(gi, d):
        yield lambda: head_pair(gi * SWA_KV_W, k_refs[gi], 0, d, 0)
        yield lambda: head_pair(kv_half + gi * SWA_KV_W, v_refs[gi], 0, d, None)
        for h0 in range(0, SWA_Q_HEADS, MXU_COLS // e):
            yield functools.partial(head_pair, q0 + gi * SWA_OUT + h0 * e, q_refs[gi], h0, d, 2)

    for tiles in zip(*[list(group_tiles(gi, d)) for gi, (_, d) in enumerate(SWA_GROUPS)]):
        for tile in tiles:
            tile()
        if gate_tiles:
            gate_tile()
    while gate_tiles:
        gate_tile()
    flush()


def _swa_in_proj(x2, w_all, rope_tables, bsz, seq):
    tm = ROW_TILE
    n_span = seq // SWA_SPAN
    tile = lambda b, n, t: (b * n_span + n) * TILES_PER_SPAN + t
    row = lambda w: pl.BlockSpec((tm, w), lambda b, n, t: (tile(b, n, t), 0))
    dils = [d for _, d in SWA_GROUPS]
    unit_out = [(d, SWA_KV_W) for d in dils] * 2 + [(d, SWA_OUT) for d in dils]
    outs = pl.pallas_call(
        _swa_in_kernel,
        grid=(bsz, n_span, TILES_PER_SPAN),
        in_specs=[row(D_MODEL), _full(w_all.shape)] + [_full(a.shape) for a in rope_tables],
        out_specs=[_tile_unit_spec(d, w) for d, w in unit_out] + [row(SWA_OUT)],
        out_shape=[jax.ShapeDtypeStruct(_unit_shape(bsz, seq, d, w), BF16) for d, w in unit_out]
        + [jax.ShapeDtypeStruct((bsz * seq, SWA_OUT), BF16)],
        scratch_shapes=[pltpu.VMEM((SWA_OUT // LANES, tm, LANES), F32), pltpu.VMEM((4, tm, SWA_HEAD_DIM), F32)],
        compiler_params=_params("parallel", "parallel", "parallel"),
        name="swa_in_proj",
    )(x2, w_all, *rope_tables)
    return outs[0:3], outs[3:6], outs[6:9], outs[9]


def _swa_kernel(q_ref, kc_ref, kp_ref, vc_ref, vp_ref, o_ref, st_ref, bias_s):
    blk = SWA_BLOCK
    e = SWA_HEAD_DIM
    n_j, d = q_ref.shape[2], q_ref.shape[3]
    n = pl.program_id(1)

    qi = lax.broadcasted_iota(jnp.int32, (blk, 2 * blk), 0)
    kj = lax.broadcasted_iota(jnp.int32, (blk, 2 * blk), 1)
    rel = blk + qi - kj
    band = (rel >= 0) & (rel <= blk)
    bias_s[0] = jnp.where(band, 0.0, -jnp.inf)
    bias_s[1] = jnp.where(band & (kj >= blk), 0.0, -jnp.inf)
    lane_slot = lax.broadcasted_iota(jnp.int32, (blk, LANES), 1) // SWA_STAT_LANES
    ones = jnp.ones((2 * blk, LANES), BF16)

    for j in range(n_j):
        for r in range(d):
            bias = bias_s[jnp.where(n == 0, 1, 0)] if j == 0 else bias_s[0]
            st_tile = jnp.zeros((blk, LANES), F32)
            for kvh in range(SWA_KV_HEADS):
                q4 = q_ref[0, 0, j, r, :, kvh * SWA_REP * e:(kvh + 1) * SWA_REP * e]
                qs = jnp.concatenate([q4[:, h * e:(h + 1) * e] for h in range(SWA_REP)], axis=0)
                kcols = slice(kvh * e, (kvh + 1) * e)
                k_prev = kp_ref[0, 0, 0, r, :, kcols] if j == 0 else kc_ref[0, 0, j - 1, r, :, kcols]
                v_prev = vp_ref[0, 0, 0, r, :, kcols] if j == 0 else vc_ref[0, 0, j - 1, r, :, kcols]
                kk = jnp.concatenate([k_prev, kc_ref[0, 0, j, r, :, kcols]], axis=0)
                vv = jnp.concatenate([v_prev, vc_ref[0, 0, j, r, :, kcols]], axis=0)
                s = lax.dot_general(qs, kk, _NT, preferred_element_type=F32)
                s = (s.reshape(SWA_REP, blk, 2 * blk) + bias).reshape(SWA_REP * blk, 2 * blk)
                m = jnp.max(s, axis=-1, keepdims=True)
                p = jnp.exp(s - m).astype(BF16)
                o = jnp.dot(p, jnp.concatenate([vv, ones], axis=1), preferred_element_type=F32)
                for h in range(SWA_REP):
                    head = kvh * SWA_REP + h
                    o_ref[0, 0, j, r, :, head * e:(head + 1) * e] = o[h * blk:(h + 1) * blk, 0:e].astype(BF16)
                    st_tile = jnp.where(lane_slot == 2 * head, m[h * blk:(h + 1) * blk], st_tile)
                    st_tile = jnp.where(lane_slot == 2 * head + 1, o[h * blk:(h + 1) * blk, e:], st_tile)
            st_ref[0, 0, j, r] = st_tile


def _swa_attention(q, k, v):
    bsz, n_span, n_j, d, blk, _ = q.shape
    cur = lambda w: pl.BlockSpec((1, 1, n_j, d, blk, w), lambda b, n: (b, n, 0, 0, 0, 0))
    prev = pl.BlockSpec((1, 1, 1, d, blk, SWA_KV_W), lambda b, n: (b, jnp.maximum(n - 1, 0), n_j - 1, 0, 0, 0))
    stat_shape = q.shape[:-1] + (LANES,)
    return pl.pallas_call(
        _swa_kernel,
        grid=(bsz, n_span),
        in_specs=[cur(SWA_OUT), cur(SWA_KV_W), prev, cur(SWA_KV_W), prev],
        out_specs=[cur(SWA_OUT), cur(LANES)],
        out_shape=[jax.ShapeDtypeStruct(q.shape, BF16), jax.ShapeDtypeStruct(stat_shape, F32)],
        scratch_shapes=[pltpu.VMEM((2, blk, 2 * blk), F32)],
        compiler_params=_params("parallel", "parallel"),
        name=f"swa_attention_d{d}",
    )(q, k, k, v, v)


def _merge_out_kernel(*refs):
    n_g = SWA_N_GROUPS
    o_refs, st_refs = refs[0:n_g], refs[n_g:2 * n_g]
    g_ref, x_ref, w_ref, lg_ref, lb_ref, out_ref, stat_s, o_s = refs[2 * n_g:]
    tm = ROW_TILE

    def to_tokens(ref, d, dst):
        jb, rows = _tile_unit_rows(d)
        n_slabs = ref.shape[-1] // LANES
        lanes = lambda c: slice(c * LANES, (c + 1) * LANES)
        if d == 1:
            tok = ref[0, 0].reshape(tm, ref.shape[-1])
            return [tok[:, lanes(c)] for c in range(n_slabs)]
        for jj in range(jb):
            for r in range(d):
                for c in range(n_slabs):
                    dst[c, _token_rows(d, jj, r, rows), :] = ref[0, 0, jj, r, :, lanes(c)].astype(F32)
        return [dst[c] for c in range(n_slabs)]

    dils = [d for _, d in SWA_GROUPS]
    sts = [to_tokens(st_refs[gi], d, stat_s.at[gi:gi + 1])[0] for gi, d in enumerate(dils)]
    mx = functools.reduce(jnp.maximum, sts)
    es = [jnp.exp(st - mx) for st in sts]
    ls = [pltpu.roll(st, LANES - SWA_STAT_LANES, 1) for st in sts]
    tot = functools.reduce(jnp.add, [eg * lg for eg, lg in zip(es, ls)])
    inv_tot = 1.0 / tot
    m_slot = (lax.broadcasted_iota(jnp.int32, (tm, LANES), 1) // SWA_STAT_LANES) % 2 == 0
    src = lax.broadcasted_iota(jnp.int32, (LANES, SWA_OUT), 0)
    dst = lax.broadcasted_iota(jnp.int32, (LANES, SWA_OUT), 1)
    expand = jnp.where(src == (dst // SWA_HEAD_DIM) * 2 * SWA_STAT_LANES, 1.0, 0.0).astype(BF16)
    acc = None
    for gi, d in enumerate(dils):
        w = jnp.dot(jnp.where(m_slot, es[gi] * inv_tot, 0.0).astype(BF16), expand, preferred_element_type=F32)
        term = w.astype(BF16) * jnp.concatenate(to_tokens(o_refs[gi], d, o_s), axis=1).astype(BF16)
        acc = term if acc is None else acc + term
    a = acc * _silu(g_ref[...])
    y = jnp.dot(a, w_ref[...], preferred_element_type=F32)
    out_ref[...] = _layer_norm(DEEPNORM_ALPHA * x_ref[...] + y, lg_ref[1:2, :], lb_ref[1:2, :])


def _merge_out_ln(os, sts, g, x2, w_out, ln_g, ln_b, bsz, seq):
    tm = ROW_TILE
    n_span = seq // SWA_SPAN
    row = lambda w: pl.BlockSpec((tm, w), lambda b, n, t: ((b * n_span + n) * TILES_PER_SPAN + t, 0))
    dils = [d for _, d in SWA_GROUPS]
    return pl.pallas_call(
        _merge_out_kernel,
        grid=(bsz, n_span, TILES_PER_SPAN),
        in_specs=[_tile_unit_spec(d, SWA_OUT) for d in dils] + [_tile_unit_spec(d, LANES) for d in dils]
        + [row(SWA_OUT), row(D_MODEL), _full(w_out.shape), _full(ln_g.shape), _full(ln_b.shape)],
        out_specs=row(D_MODEL),
        out_shape=jax.ShapeDtypeStruct((bsz * seq, D_MODEL), F32),
        scratch_shapes=[pltpu.VMEM((SWA_N_GROUPS, tm, LANES), F32),
                        pltpu.VMEM((SWA_OUT // LANES, tm, LANES), F32)],
        compiler_params=_params("parallel", "parallel", "parallel"),
        name="merge_out_ln",
    )(*os, *sts, g, x2, w_out, ln_g, ln_b)


def _rope_tables(seq):
    half = SWA_HEAD_DIM // 2
    inv = ROPE_THETA ** (-(jnp.arange(half, dtype=F32) * 2.0) / SWA_HEAD_DIM)
    inv = jnp.concatenate([inv, inv])[None, :]
    base = (jnp.arange(seq // ROW_TILE) * ROW_TILE).astype(F32)[:, None] * inv
    off = jnp.arange(ROW_TILE).astype(F32)[:, None] * inv
    return jnp.cos(base), jnp.sin(base), jnp.cos(off), jnp.sin(off)


GLA_N_MAIN = 2 * GLA_DK_TOTAL + 2 * GLA_DV_TOTAL
CAST_ROWS = 128


def _cast_weights_kernel(gw_in_t, gw_a2, gw_out, w_main, w_alr, w_a2, w_out0):
    r = CAST_ROWS
    t = gw_in_t[0]
    for c in range(0, GLA_N_MAIN, r):
        w_main[:, c:c + r] = t[c:c + r, :].T.astype(BF16)
    tail = t[GLA_N_MAIN + GLA_GATE_RANK - r:, :].T
    lane = lax.broadcasted_iota(jnp.int32, (r, r), 1)
    w_alr[...] = jnp.where(lane >= r - GLA_GATE_RANK, tail, 0.0).astype(BF16)
    w_a2[...] = jnp.zeros_like(w_a2)
    w_a2[LANES - GLA_GATE_RANK:, :] = gw_a2[0].astype(BF16)
    w_out0[...] = gw_out[0].astype(BF16)


def _cast_weights(gla_w_in, gla_w_a2, gla_w_out):
    r = CAST_ROWS
    assert r == LANES
    rows3 = lambda n: pl.BlockSpec((1, r, n), lambda i: (0, i, 0))
    rows2 = lambda n: pl.BlockSpec((r, n), lambda i: (i, 0))
    gla_w_in_t = jnp.swapaxes(gla_w_in, 1, 2)
    cols_t = pl.BlockSpec((1, gla_w_in_t.shape[1], r), lambda i: (0, 0, i))
    out_widths = [GLA_N_MAIN, LANES, None, D_MODEL]
    out_shapes = [(D_MODEL, w) if w else (LANES, GLA_DK_TOTAL) for w in out_widths]
    return pl.pallas_call(
        _cast_weights_kernel,
        grid=(D_MODEL // r,),
        in_specs=[cols_t, _full(gla_w_a2.shape), rows3(D_MODEL)],
        out_specs=[rows2(w) if w else pl.BlockSpec((LANES, GLA_DK_TOTAL), lambda i: (0, 0)) for w in out_widths],
        out_shape=[jax.ShapeDtypeStruct(s, BF16) for s in out_shapes],
        compiler_params=_params("arbitrary"),
        name="cast_weights",
    )(gla_w_in_t, gla_w_a2, gla_w_out)


def kernel(x, gla_w_in, gla_w_a2, gla_b_a2, gla_norm_g, gla_w_out, w_kv, swa_w_in, swa_w_out, ln_g, ln_b):
    bsz, seq, _ = x.shape
    assert seq % SWA_SPAN == 0 and seq % GLA_BLOCK == 0
    assert gla_w_in.shape[0] == 1 and swa_w_in.shape[0] == 1 and ln_g.shape[0] == DEPTH
    m = bsz * seq

    w_main, w_alr, w_a2, w_out0 = _cast_weights(gla_w_in, gla_w_a2, gla_w_out)
    x2, w_all, w_out1 = _gla_layer(x, w_main, w_alr, w_a2, gla_b_a2, gla_norm_g, w_out0, ln_g, ln_b,
                                   w_kv, swa_w_in, swa_w_out)
    x2 = x2.reshape(m, D_MODEL)

    ks, vs, qs, g = _swa_in_proj(x2, w_all, _rope_tables(seq), bsz, seq)
    os, sts = zip(*[_swa_attention(qs[gi], ks[gi], vs[gi]) for gi in range(SWA_N_GROUPS)])
    out = _merge_out_ln(os, sts, g, x2, w_out1, ln_g, ln_b, bsz, seq)
    return out.reshape(bsz, seq, D_MODEL)
```

```python
import functools

import jax
import jax.numpy as jnp
from jax import lax
from jax.experimental import pallas as pl
from jax.experimental.pallas import tpu as pltpu

BF16 = jnp.bfloat16
F32 = jnp.float32

D_MODEL = 1024
DEPTH = 2
DEEPNORM_ALPHA = (2.0 * DEPTH) ** 0.25
LN_EPS = 1e-5
RMS_EPS = 1e-5

GLA_HEADS = 4
GLA_DK = 128
GLA_DV = 256
GLA_DK_TOTAL = GLA_HEADS * GLA_DK
GLA_DV_TOTAL = GLA_HEADS * GLA_DV
GLA_GATE_RANK = 16
GLA_GATE_TAU = 16.0
GLA_SUB = 64
GLA_CHUNK = 2 * GLA_SUB
GLA_BLOCK = 512
GLA_HEAD_GROUP = 4
GLA_TAIL_PIECES = 2
GLA_FAST_MIN_LOG_DECAY = -40.0

SWA_GROUPS = ((128, 1), (512, 4), (2048, 16))
SWA_N_GROUPS = len(SWA_GROUPS)
SWA_HEAD_DIM = 128
SWA_Q_HEADS = 8
SWA_KV_HEADS = 2
SWA_REP = SWA_Q_HEADS // SWA_KV_HEADS
SWA_OUT = SWA_Q_HEADS * SWA_HEAD_DIM
SWA_KV_W = SWA_KV_HEADS * SWA_HEAD_DIM
SWA_KV_TOTAL = 2 * SWA_N_GROUPS * SWA_KV_W
SWA_BLOCK = 128
SWA_SPAN = 2048
SWA_STAT_LANES = SWA_HEAD_DIM // (2 * SWA_Q_HEADS)
ROPE_THETA = 10000.0

LANES = 128
BF16_ROWS = 16
F32_ROWS = 8
MXU_COLS = 256
ROW_TILE = 1024
TILES_PER_SPAN = SWA_SPAN // ROW_TILE
VMEM_LIMIT = 56 * 1024 * 1024

_NT = (((1,), (1,)), ((), ()))


def _params(*sem, flags=None):
    return pltpu.CompilerParams(dimension_semantics=sem, vmem_limit_bytes=VMEM_LIMIT, flags=flags)


def _full(shape):
    return pl.BlockSpec(shape, lambda *_: (0,) * len(shape), pipeline_mode=pl.Buffered(1))


def _layer_norm(h, g, b):
    mu = jnp.mean(h, axis=-1, keepdims=True)
    hc = h - mu
    var = jnp.mean(hc * hc, axis=-1, keepdims=True)
    return hc * lax.rsqrt(var + LN_EPS) * g + b


def _silu(g):
    return g * (0.5 + 0.5 * jnp.tanh(0.5 * g))


def _gla_project_pieces(x_ref, w_ref, walr_ref, wa2_ref, ba2_ref, xb_s, q_ref, k_ref, v_ref, g_ref, la_ref):
    xb_s[...] = x_ref[0].astype(BF16)

    def tile(dst, col, lo, scale):
        def piece():
            y = jnp.dot(xb_s[...], w_ref[:, col + lo:col + lo + MXU_COLS], preferred_element_type=F32)
            dst[:, lo:lo + MXU_COLS] = (y if scale is None else y * scale).astype(BF16)
        return piece

    def gate_piece():
        a_lr = jnp.dot(xb_s[...], walr_ref[...], preferred_element_type=F32)
        z = jnp.dot(a_lr.astype(BF16), wa2_ref[...], preferred_element_type=F32) + ba2_ref[...]
        log_sig = jnp.minimum(z, 0.0) - jnp.log(1.0 + jnp.exp(-jnp.abs(z)))
        la_ref[...] = log_sig * (1.0 / GLA_GATE_TAU)

    pieces, col = [], 0
    for dst, scale in ((q_ref, GLA_DK ** -0.5), (k_ref, None), (v_ref, None), (g_ref, None)):
        pieces += [tile(dst, col, lo, scale) for lo in range(0, dst.shape[1], MXU_COLS)]
        col += dst.shape[1]
    return pieces[:1] + [gate_piece] + pieces[1:]


def _gla_recur_pieces(q_ref, k_ref, v_ref, g_ref, la_ref, ng_ref, o_ref,
                      st_ref, qe_s, qi_s, kd_s, kx_s, klt_s, gam_s):
    t = la_ref.shape[0]
    c, sub = GLA_CHUNK, GLA_SUB

    row = lax.broadcasted_iota(jnp.int32, (c, c), 0)
    col = lax.broadcasted_iota(jnp.int32, (c, c), 1)
    causal = col <= row
    tri = jnp.where(causal & (row // sub == col // sub), 1.0, 0.0).astype(BF16)
    halves = lambda first, second: jnp.concatenate([first, second], axis=0)

    for ic in range(t // c):
        rows = slice(ic * c, (ic + 1) * c)
        first = slice(ic * c, ic * c + sub)
        second = slice(ic * c + sub, (ic + 1) * c)

        la = la_ref[rows, :]
        la_hi = la.astype(BF16)
        la_lo = (la - la_hi.astype(F32)).astype(BF16)
        bs = jnp.dot(tri, la_hi, preferred_element_type=F32) + jnp.dot(tri, la_lo, preferred_element_type=F32)
        e_a = jnp.exp(bs[sub - 1:sub, :])
        e_b = jnp.exp(bs[c - 1:c, :])
        qi = q_ref[rows, :].astype(F32) * jnp.exp(bs)
        kd = k_ref[rows, :].astype(F32) * jnp.exp(-bs)
        kl_a, kl_b = kd[:sub] * e_a, kd[sub:] * e_b
        qi_s[rows, :] = qi.astype(BF16)
        kd_s[rows, :] = kd.astype(BF16)
        qe_s[rows, :] = halves(qi[:sub], qi[sub:] * e_a).astype(BF16)
        kx_s[rows, :] = halves(kl_a, kd[sub:]).astype(BF16)
        klc = halves(kl_a * e_b, kl_b)
        gam = e_a * e_b
        for h in range(GLA_HEADS):
            kcols = slice(h * GLA_DK, (h + 1) * GLA_DK)
            klt_s[ic, h] = klc[:, kcols].T.astype(BF16)
            gam_s[ic, h] = jnp.broadcast_to(gam[:, kcols], (c, GLA_DK)).T
        yield

        for h0 in range(0, GLA_HEADS, GLA_HEAD_GROUP):
            stage1 = []
            for h in range(h0, h0 + GLA_HEAD_GROUP):
                kcols = slice(h * GLA_DK, (h + 1) * GLA_DK)
                st = st_ref[h]
                o_inter = jnp.dot(qe_s[rows, kcols], st.astype(BF16), preferred_element_type=F32)
                att = jnp.concatenate(
                    [lax.dot_general(qi_s[first, kcols], kd_s[rows, kcols], _NT, preferred_element_type=F32),
                     lax.dot_general(qi_s[second, kcols], kx_s[rows, kcols], _NT, preferred_element_type=F32)],
                    axis=0)
                stage1.append((st, o_inter, att))
            yield
            for h, (st, o_inter, att) in zip(range(h0, h0 + GLA_HEAD_GROUP), stage1):
                vcols = slice(h * GLA_DV, (h + 1) * GLA_DV)
                att = jnp.where(causal, att, 0.0).astype(BF16)
                both = jnp.dot(jnp.concatenate([att, klt_s[ic, h]], axis=0), v_ref[rows, vcols],
                               preferred_element_type=F32)
                gam_t = gam_s[ic, h]
                st_ref[h] = st * jnp.concatenate([gam_t] * (GLA_DV // GLA_DK), axis=1) + both[c:]
                o = o_inter + both[0:c]
                o = o * lax.rsqrt(jnp.mean(o * o, axis=-1, keepdims=True) + RMS_EPS) * ng_ref[:, vcols]
                o_ref[rows, vcols] = o.astype(BF16) * _silu(g_ref[rows, vcols])
            yield


def _pairwise_scores(b_s, qf_s, kf_s, kcols):
    c = GLA_CHUNK
    bj = b_s[:, kcols]
    kj = kf_s[:, kcols]
    j_idx = lax.broadcasted_iota(jnp.int32, (c, 1), 0)
    lane = lax.broadcasted_iota(jnp.int32, (c, c), 1)

    def query_rows(g, att_t):
        base = pl.multiple_of(g * F32_ROWS, F32_ROWS)
        b_g = b_s[pl.ds(base, F32_ROWS), kcols]
        q_g = qf_s[pl.ds(base, F32_ROWS), kcols]
        for r in range(F32_ROWS):
            i = g * F32_ROWS + r
            w = jnp.exp(jnp.minimum(b_g[r:r + 1] - bj, 0.0)) * kj * q_g[r:r + 1]
            col = jnp.sum(w, axis=-1, keepdims=True)
            att_t = jnp.where(lane == i, jnp.where(j_idx <= i, col, 0.0), att_t)
        return att_t

    return lax.fori_loop(0, c // F32_ROWS, query_rows, jnp.zeros((c, c), F32)).T


def _gla_recur_pairwise(q_ref, k_ref, v_ref, g_ref, la_ref, ng_ref, o_ref, st_ref, b_s, qf_s, kf_s):
    c, sub = GLA_CHUNK, GLA_SUB
    row = lax.broadcasted_iota(jnp.int32, (c, c), 0)
    col = lax.broadcasted_iota(jnp.int32, (c, c), 1)
    tri = jnp.where((col <= row) & (row // sub == col // sub), 1.0, 0.0).astype(BF16)

    def chunk(ic, carry):
        rows = pl.ds(pl.multiple_of(ic * c, c), c)
        la = la_ref[rows, :]
        la_hi = la.astype(BF16)
        la_lo = (la - la_hi.astype(F32)).astype(BF16)
        bs = jnp.dot(tri, la_hi, preferred_element_type=F32) + jnp.dot(tri, la_lo, preferred_element_type=F32)
        b = jnp.concatenate([bs[:sub], bs[sub:] + bs[sub - 1:sub, :]], axis=0)
        qf, kf = q_ref[rows, :].astype(F32), k_ref[rows, :].astype(F32)
        b_s[...], qf_s[...], kf_s[...] = b, qf, kf
        qe = (qf * jnp.exp(b)).astype(BF16)
        kl = kf * jnp.exp(b[c - 1:c, :] - b)
        gam = jnp.exp(b[c - 1:c, :])
        for h in range(GLA_HEADS):
            kcols = slice(h * GLA_DK, (h + 1) * GLA_DK)
            vcols = slice(h * GLA_DV, (h + 1) * GLA_DV)
            st = st_ref[h]
            att = _pairwise_scores(b_s, qf_s, kf_s, kcols).astype(BF16)
            o_inter = jnp.dot(qe[:, kcols], st.astype(BF16), preferred_element_type=F32)
            both = jnp.dot(jnp.concatenate([att, kl[:, kcols].T.astype(BF16)], axis=0), v_ref[rows, vcols],
                           preferred_element_type=F32)
            gam_t = jnp.broadcast_to(gam[:, kcols], (c, GLA_DK)).T
            st_ref[h] = st * jnp.concatenate([gam_t] * (GLA_DV // GLA_DK), axis=1) + both[c:]
            o = o_inter + both[0:c]
            o = o * lax.rsqrt(jnp.mean(o * o, axis=-1, keepdims=True) + RMS_EPS) * ng_ref[:, vcols]
            o_ref[rows, vcols] = o.astype(BF16) * _silu(g_ref[rows, vcols])
        return carry

    lax.fori_loop(0, la_ref.shape[0] // c, chunk, 0)


N_PROJ = 5


def _gla_layer_kernel(xp_ref, xr_ref, w_ref, walr_ref, wa2_ref, ba2_ref, ng_ref, wout_ref, lg_ref, lb_ref,
                      wkv_ref, swin_ref, swout_ref, out_ref, wall_ref, wout1_ref, st_ref, *scratch,
                      blocks_per_seq):
    bufs = (scratch[0:N_PROJ], scratch[N_PROJ:2 * N_PROJ])
    xb_s, rec_s = scratch[2 * N_PROJ], scratch[2 * N_PROJ + 1:-5]
    pair_s, st0_s, og_s = scratch[-5:-2], scratch[-2], scratch[-1]
    j = pl.program_id(0)
    wall_ref[:, 0:SWA_KV_TOTAL] = wkv_ref[...].astype(BF16)
    wall_ref[:, SWA_KV_TOTAL:] = swin_ref[0].astype(BF16)
    wout1_ref[...] = swout_ref[0].astype(BF16)

    @pl.when(j == 0)
    def _():
        for ref in bufs[1]:
            ref[...] = jnp.zeros_like(ref)

    @pl.when((j == 0) | (j % blocks_per_seq == 1 % blocks_per_seq))
    def _():
        st_ref[...] = jnp.zeros_like(st_ref)

    def step(write, read):
        totals = jnp.sum(read[N_PROJ - 1][...].reshape(-1, GLA_SUB, GLA_DK_TOTAL), axis=1)
        out_of_range = jnp.min(totals) < GLA_FAST_MIN_LOG_DECAY
        st0_s[...] = st_ref[...]

        pieces = _gla_project_pieces(xp_ref, w_ref, walr_ref, wa2_ref, ba2_ref, xb_s, *write)
        n_pieces = len(pieces) - GLA_TAIL_PIECES
        n_stages = (xb_s.shape[0] // GLA_CHUNK) * (1 + 2 * (GLA_HEADS // GLA_HEAD_GROUP))
        for i, _ in enumerate(_gla_recur_pieces(*read, ng_ref, og_s, st_ref, *rec_s)):
            issued_after = ((i + 1) * n_pieces + n_stages - 1) // n_stages
            while n_pieces + GLA_TAIL_PIECES - len(pieces) < issued_after:
                pieces.pop(0)()

        def project_out(beside=()):
            y = jnp.dot(og_s[...], wout_ref[...], preferred_element_type=F32)
            for piece in beside:
                piece()
            out_ref[0] = _layer_norm(DEEPNORM_ALPHA * xr_ref[0] + y, lg_ref[0:1, :], lb_ref[0:1, :])

        project_out(pieces)

        @pl.when(out_of_range)
        def _():
            st_ref[...] = st0_s[...]
            _gla_recur_pairwise(*read, ng_ref, og_s, st_ref, *pair_s)
            project_out()

    @pl.when(j % 2 == 0)
    def _():
        step(bufs[0], bufs[1])

    @pl.when(j % 2 == 1)
    def _():
        step(bufs[1], bufs[0])


def _gla_layer(x, w_main, w_alr, w_a2, b_a2, norm_g, w_out, ln_g, ln_b, w_kv, swa_w_in, swa_w_out):
    bsz, s, _ = x.shape
    t = GLA_BLOCK
    n_blk = s // t
    n_chunks = t // GLA_CHUNK
    rows = lambda n, dt: pltpu.VMEM((t, n), dt)
    proj_bufs = [rows(GLA_DK_TOTAL, BF16), rows(GLA_DK_TOTAL, BF16), rows(GLA_DV_TOTAL, BF16),
                 rows(GLA_DV_TOTAL, BF16), rows(GLA_DK_TOTAL, F32)]
    assert len(proj_bufs) == N_PROJ
    n_all = bsz * n_blk
    xb = x.reshape(n_all, t, D_MODEL)
    cur = pl.BlockSpec((1, t, D_MODEL), lambda j: (jnp.minimum(j, n_all - 1), 0, 0))
    prev = pl.BlockSpec((1, t, D_MODEL), lambda j: (jnp.maximum(j - 1, 0), 0, 0))
    consts = (w_main, w_alr, w_a2, b_a2, norm_g, w_out, ln_g, ln_b)
    wr = D_MODEL // n_all
    assert wr * n_all == D_MODEL and wr % BF16_ROWS == 0
    wrows2 = lambda n: pl.BlockSpec((wr, n), lambda j: (jnp.minimum(j, n_all - 1), 0))
    wrows3 = lambda n: pl.BlockSpec((1, wr, n), lambda j: (0, jnp.minimum(j, n_all - 1), 0))
    n_in = w_kv.shape[1] + swa_w_in.shape[2]
    return pl.pallas_call(
        functools.partial(_gla_layer_kernel, blocks_per_seq=n_blk),
        grid=(n_all + 1,),
        in_specs=[cur, prev] + [_full(a.shape) for a in consts]
        + [wrows2(w_kv.shape[1]), wrows3(swa_w_in.shape[2]), wrows3(D_MODEL)],
        out_specs=[prev, wrows2(n_in), wrows2(D_MODEL)],
        out_shape=[jax.ShapeDtypeStruct(xb.shape, F32), jax.ShapeDtypeStruct((D_MODEL, n_in), BF16),
                   jax.ShapeDtypeStruct((D_MODEL, D_MODEL), BF16)],
        scratch_shapes=[pltpu.VMEM((GLA_HEADS, GLA_DK, GLA_DV), F32)] + proj_bufs + proj_bufs
        + [rows(D_MODEL, BF16)] + [rows(GLA_DK_TOTAL, BF16)] * 4
        + [pltpu.VMEM((n_chunks, GLA_HEADS, GLA_DK, GLA_CHUNK), BF16),
           pltpu.VMEM((n_chunks, GLA_HEADS, GLA_DK, GLA_CHUNK), F32)]
        + [pltpu.VMEM((GLA_CHUNK, GLA_DK_TOTAL), F32)] * 3 + [pltpu.VMEM((GLA_HEADS, GLA_DK, GLA_DV), F32)]
        + [rows(GLA_DV_TOTAL, BF16)],
        compiler_params=_params("arbitrary"),
        name="gla_layer",
    )(xb, xb, *consts, w_kv, swa_w_in, swa_w_out)


def _unit_shape(bsz, seq, d, width):
    return (bsz, seq // SWA_SPAN, SWA_SPAN // (SWA_BLOCK * d), d, SWA_BLOCK, width)


def _tile_unit_spec(d, width):
    rows_per = ROW_TILE // d
    if rows_per >= SWA_BLOCK:
        jb = rows_per // SWA_BLOCK
        return pl.BlockSpec((1, 1, jb, d, SWA_BLOCK, width), lambda b, n, t: (b, n, t, 0, 0, 0))
    per_j = SWA_BLOCK // rows_per
    return pl.BlockSpec((1, 1, 1, d, rows_per, width), lambda b, n, t: (b, n, t // per_j, 0, t % per_j, 0))


def _tile_unit_rows(d):
    rows_per = ROW_TILE // d
    return max(rows_per // SWA_BLOCK, 1), min(rows_per, SWA_BLOCK)


def _token_rows(d, jj, r, rows):
    return pl.ds(jj * SWA_BLOCK * d + r, rows, stride=d) if d > 1 else pl.ds(jj * SWA_BLOCK, rows)


def _rope(h, cos, sin_signed):
    return h * cos + pltpu.roll(h, SWA_HEAD_DIM // 2, 1) * sin_signed


def _swa_in_kernel(x_ref, w_ref, cos_base_ref, sin_base_ref, cos_off_ref, sin_off_ref, *refs):
    k_refs = refs[0:SWA_N_GROUPS]
    v_refs = refs[SWA_N_GROUPS:2 * SWA_N_GROUPS]
    q_refs = refs[2 * SWA_N_GROUPS:3 * SWA_N_GROUPS]
    g_ref = refs[3 * SWA_N_GROUPS]
    tok_s = refs[3 * SWA_N_GROUPS + 1]
    rope_s = refs[3 * SWA_N_GROUPS + 2]
    xb = x_ref[...].astype(BF16)
    e = SWA_HEAD_DIM
    tile = pl.ds(pl.program_id(1) * TILES_PER_SPAN + pl.program_id(2), 1)
    ca, sa = cos_base_ref[tile, :], sin_base_ref[tile, :]
    cb, sb = cos_off_ref[...], sin_off_ref[...]
    cos = ca * cb - sa * sb
    lane = lax.broadcasted_iota(jnp.int32, cb.shape, 1)
    sin = jnp.where(lane < e // 2, -1.0, 1.0) * (sa * cb + ca * sb)
    rope_s[0], rope_s[1] = cos, sin
    rope_s[2], rope_s[3] = cos * (e ** -0.5), sin * (e ** -0.5)
    kv_half = SWA_N_GROUPS * SWA_KV_W
    q0 = 2 * kv_half

    n_slabs = tok_s.shape[0]
    pipeline = {"slab": 0, "pending": None}

    def head_pair(col, out_ref, head0, d, rope):
        y = jnp.dot(xb, w_ref[:, col:col + MXU_COLS], preferred_element_type=F32)
        jb, rows = _tile_unit_rows(d)
        staged = []
        for i in range(MXU_COLS // e):
            h = head0 + i
            yh = y[:, i * e:(i + 1) * e]
            if rope is not None:
                yh = _rope(yh, rope_s[rope], rope_s[rope + 1])
            if d == 1:
                out_ref[0, 0, :, :, :, h * e:(h + 1) * e] = yh.astype(BF16).reshape(jb, d, rows, e)
                continue
            slab = pipeline["slab"]
            pipeline["slab"] = (slab + 1) % n_slabs
            tok_s[slab] = yh
            staged.append((slab, h))

        def scatter():
            for slab, h in staged:
                for jj in range(jb):
                    for r in range(d):
                        out_ref[0, 0, jj, r, :, h * e:(h + 1) * e] = (
                            tok_s[slab, _token_rows(d, jj, r, rows), :].astype(BF16))

        flush()
        pipeline["pending"] = scatter if staged else None

    def flush():
        if pipeline["pending"] is not None:
            pipeline["pending"]()
            pipeline["pending"] = None

    g0 = q0 + SWA_N_GROUPS * SWA_OUT
    gate_tiles = list(range(0, SWA_OUT, MXU_COLS))

    def gate_tile():
        c = gate_tiles.pop(0)
        g_ref[:, c:c + MXU_COLS] = jnp.dot(xb, w_ref[:, g0 + c:g0 + c + MXU_COLS],
                                           preferred_element_type=F32).astype(BF16)

    def group_tiles(gi, d):
        yield lambda: head_pair(gi * SWA_KV_W, k_refs[gi], 0, d, 0)
        yield lambda: head_pair(kv_half + gi * SWA_KV_W, v_refs[gi], 0, d, None)
        for h0 in range(0, SWA_Q_HEADS, MXU_COLS // e):
            yield functools.partial(head_pair, q0 + gi * SWA_OUT + h0 * e, q_refs[gi], h0, d, 2)

    for tiles in zip(*[list(group_tiles(gi, d)) for gi, (_, d) in enumerate(SWA_GROUPS)]):
        for tile in tiles:
            tile()
        if gate_tiles:
            gate_tile()
    while gate_tiles:
        gate_tile()
    flush()


def _swa_in_proj(x2, w_all, rope_tables, bsz, seq):
    tm = ROW_TILE
    n_span = seq // SWA_SPAN
    tile = lambda b, n, t: (b * n_span + n) * TILES_PER_SPAN + t
    row = lambda w: pl.BlockSpec((tm, w), lambda b, n, t: (tile(b, n, t), 0))
    dils = [d for _, d in SWA_GROUPS]
    unit_out = [(d, SWA_KV_W) for d in dils] * 2 + [(d, SWA_OUT) for d in dils]
    outs = pl.pallas_call(
        _swa_in_kernel,
        grid=(bsz, n_span, TILES_PER_SPAN),
        in_specs=[row(D_MODEL), _full(w_all.shape)] + [_full(a.shape) for a in rope_tables],
        out_specs=[_tile_unit_spec(d, w) for d, w in unit_out] + [row(SWA_OUT)],
        out_shape=[jax.ShapeDtypeStruct(_unit_shape(bsz, seq, d, w), BF16) for d, w in unit_out]
        + [jax.ShapeDtypeStruct((bsz * seq, SWA_OUT), BF16)],
        scratch_shapes=[pltpu.VMEM((SWA_OUT // LANES, tm, LANES), F32), pltpu.VMEM((4, tm, SWA_HEAD_DIM), F32)],
        compiler_params=_params("parallel", "parallel", "parallel"),
        name="swa_in_proj",
    )(x2, w_all, *rope_tables)
    return outs[0:3], outs[3:6], outs[6:9], outs[9]


def _swa_kernel(q_ref, kc_ref, kp_ref, vc_ref, vp_ref, b0_ref, b_ref, o_ref, st_ref):
    blk = SWA_BLOCK
    e = SWA_HEAD_DIM
    n_j, d = q_ref.shape[2], q_ref.shape[3]
    lane_slot = lax.broadcasted_iota(jnp.int32, (blk, LANES), 1) // SWA_STAT_LANES
    ones = jnp.ones((2 * blk, LANES), BF16)

    for j in range(n_j):
        for r in range(d):
            bias = b0_ref[0] if j == 0 else b_ref[...]
            st_tile = jnp.zeros((blk, LANES), F32)
            for kvh in range(SWA_KV_HEADS):
                q4 = q_ref[0, 0, j, r, :, kvh * SWA_REP * e:(kvh + 1) * SWA_REP * e]
                qs = jnp.concatenate([q4[:, h * e:(h + 1) * e] for h in range(SWA_REP)], axis=0)
                kcols = slice(kvh * e, (kvh + 1) * e)
                k_prev = kp_ref[0, 0, 0, r, :, kcols] if j == 0 else kc_ref[0, 0, j - 1, r, :, kcols]
                v_prev = vp_ref[0, 0, 0, r, :, kcols] if j == 0 else vc_ref[0, 0, j - 1, r, :, kcols]
                kk = jnp.concatenate([k_prev, kc_ref[0, 0, j, r, :, kcols]], axis=0)
                vv = jnp.concatenate([v_prev, vc_ref[0, 0, j, r, :, kcols]], axis=0)
                s = lax.dot_general(qs, kk, _NT, preferred_element_type=F32)
                s = (s.reshape(SWA_REP, blk, 2 * blk) + bias).reshape(SWA_REP * blk, 2 * blk)
                m = jnp.max(s, axis=-1, keepdims=True)
                p = jnp.exp(s - m).astype(BF16)
                o = jnp.dot(p, jnp.concatenate([vv, ones], axis=1), preferred_element_type=F32)
                for h in range(SWA_REP):
                    head = kvh * SWA_REP + h
                    o_ref[0, 0, j, r, :, head * e:(head + 1) * e] = o[h * blk:(h + 1) * blk, 0:e].astype(BF16)
                    st_tile = jnp.where(lane_slot == 2 * head, m[h * blk:(h + 1) * blk], st_tile)
                    st_tile = jnp.where(lane_slot == 2 * head + 1, o[h * blk:(h + 1) * blk, e:], st_tile)
            st_ref[0, 0, j, r] = st_tile


def _band_masks(n_span):
    blk = SWA_BLOCK
    qi = lax.broadcasted_iota(jnp.int32, (blk, 2 * blk), 0)
    kj = lax.broadcasted_iota(jnp.int32, (blk, 2 * blk), 1)
    rel = blk + qi - kj
    band = (rel >= 0) & (rel <= blk)
    plain = jnp.where(band, 0.0, -jnp.inf).astype(F32)
    first = jnp.where(band & (kj >= blk), 0.0, -jnp.inf).astype(F32)
    return jnp.concatenate([first[None], jnp.broadcast_to(plain, (n_span - 1, blk, 2 * blk))], axis=0), plain


def _swa_attention(q, k, v):
    bsz, n_span, n_j, d, blk, _ = q.shape
    span = lambda w, nbuf: pl.BlockSpec((1, 1, n_j, d, blk, w), lambda b, n: (b, n, 0, 0, 0, 0),
                                        pipeline_mode=pl.Buffered(nbuf))
    prev = pl.BlockSpec((1, 1, 1, d, blk, SWA_KV_W), lambda b, n: (b, jnp.maximum(n - 1, 0), n_j - 1, 0, 0, 0))
    b0_spec = pl.BlockSpec((1, blk, 2 * blk), lambda b, n: (n, 0, 0))
    b_spec = pl.BlockSpec((blk, 2 * blk), lambda b, n: (0, 0))
    stat_shape = q.shape[:-1] + (LANES,)
    b0, b_plain = _band_masks(n_span)

    def spans(*hbm_refs):
        pltpu.emit_pipeline(
            _swa_kernel,
            grid=(bsz, n_span),
            in_specs=[span(SWA_OUT, 3), span(SWA_KV_W, 3), prev, span(SWA_KV_W, 3), prev, b0_spec, b_spec],
            out_specs=[span(SWA_OUT, 2), span(LANES, 2)],
        )(*hbm_refs)

    any_spec = pl.BlockSpec(memory_space=pl.ANY)
    return pl.pallas_call(
        spans,
        in_specs=[any_spec] * 7,
        out_specs=[any_spec] * 2,
        out_shape=[jax.ShapeDtypeStruct(q.shape, BF16), jax.ShapeDtypeStruct(stat_shape, F32)],
        compiler_params=pltpu.CompilerParams(vmem_limit_bytes=VMEM_LIMIT),
        name=f"swa_attention_d{d}",
    )(q, k, k, v, v, b0, b_plain)


def _merge_out_kernel(*refs):
    n_g = SWA_N_GROUPS
    o_refs, st_refs = refs[0:n_g], refs[n_g:2 * n_g]
    g_ref, x_ref, w_ref, lg_ref, lb_ref, out_ref, stat_s, o_s = refs[2 * n_g:]
    tm = ROW_TILE

    def to_tokens(ref, d, dst):
        jb, rows = _tile_unit_rows(d)
        n_slabs = ref.shape[-1] // LANES
        lanes = lambda c: slice(c * LANES, (c + 1) * LANES)
        if d == 1:
            tok = ref[0, 0].reshape(tm, ref.shape[-1])
            return [tok[:, lanes(c)] for c in range(n_slabs)]
        for jj in range(jb):
            for r in range(d):
                for c in range(n_slabs):
                    dst[c, _token_rows(d, jj, r, rows), :] = ref[0, 0, jj, r, :, lanes(c)].astype(F32)
        return [dst[c] for c in range(n_slabs)]

    dils = [d for _, d in SWA_GROUPS]
    sts = [to_tokens(st_refs[gi], d, stat_s.at[gi:gi + 1])[0] for gi, d in enumerate(dils)]
    mx = functools.reduce(jnp.maximum, sts)
    es = [jnp.exp(st - mx) for st in sts]
    ls = [pltpu.roll(st, LANES - SWA_STAT_LANES, 1) for st in sts]
    tot = functools.reduce(jnp.add, [eg * lg for eg, lg in zip(es, ls)])
    inv_tot = 1.0 / tot
    m_slot = (lax.broadcasted_iota(jnp.int32, (tm, LANES), 1) // SWA_STAT_LANES) % 2 == 0
    src = lax.broadcasted_iota(jnp.int32, (LANES, SWA_OUT), 0)
    dst = lax.broadcasted_iota(jnp.int32, (LANES, SWA_OUT), 1)
    expand = jnp.where(src == (dst // SWA_HEAD_DIM) * 2 * SWA_STAT_LANES, 1.0, 0.0).astype(BF16)
    acc = None
    for gi, d in enumerate(dils):
        w = jnp.dot(jnp.where(m_slot, es[gi] * inv_tot, 0.0).astype(BF16), expand, preferred_element_type=F32)
        term = w.astype(BF16) * jnp.concatenate(to_tokens(o_refs[gi], d, o_s), axis=1).astype(BF16)
        acc = term if acc is None else acc + term
    a = acc * _silu(g_ref[...])
    y = jnp.dot(a, w_ref[...], preferred_element_type=F32)
    out_ref[...] = _layer_norm(DEEPNORM_ALPHA * x_ref[...] + y, lg_ref[1:2, :], lb_ref[1:2, :])


def _merge_out_ln(os, sts, g, x2, w_out, ln_g, ln_b, bsz, seq):
    tm = ROW_TILE
    n_span = seq // SWA_SPAN
    row = lambda w: pl.BlockSpec((tm, w), lambda b, n, t: ((b * n_span + n) * TILES_PER_SPAN + t, 0))
    dils = [d for _, d in SWA_GROUPS]
    return pl.pallas_call(
        _merge_out_kernel,
        grid=(bsz, n_span, TILES_PER_SPAN),
        in_specs=[_tile_unit_spec(d, SWA_OUT) for d in dils] + [_tile_unit_spec(d, LANES) for d in dils]
        + [row(SWA_OUT), row(D_MODEL), _full(w_out.shape), _full(ln_g.shape), _full(ln_b.shape)],
        out_specs=row(D_MODEL),
        out_shape=jax.ShapeDtypeStruct((bsz * seq, D_MODEL), F32),
        scratch_shapes=[pltpu.VMEM((SWA_N_GROUPS, tm, LANES), F32),
                        pltpu.VMEM((SWA_OUT // LANES, tm, LANES), F32)],
        compiler_params=_params("parallel", "parallel", "parallel"),
        name="merge_out_ln",
    )(*os, *sts, g, x2, w_out, ln_g, ln_b)


def _rope_tables(seq):
    half = SWA_HEAD_DIM // 2
    inv = ROPE_THETA ** (-(jnp.arange(half, dtype=F32) * 2.0) / SWA_HEAD_DIM)
    inv = jnp.concatenate([inv, inv])[None, :]
    base = (jnp.arange(seq // ROW_TILE) * ROW_TILE).astype(F32)[:, None] * inv
    off = jnp.arange(ROW_TILE).astype(F32)[:, None] * inv
    return jnp.cos(base), jnp.sin(base), jnp.cos(off), jnp.sin(off)


GLA_N_MAIN = 2 * GLA_DK_TOTAL + 2 * GLA_DV_TOTAL
CAST_ROWS = 128


def _cast_weights_kernel(gw_in_t, gw_a2, gw_out, w_main, w_alr, w_a2, w_out0):
    r = CAST_ROWS
    t = gw_in_t[0]
    for c in range(0, GLA_N_MAIN, r):
        w_main[:, c:c + r] = t[c:c + r, :].T.astype(BF16)
    tail = t[GLA_N_MAIN + GLA_GATE_RANK - r:, :].T
    lane = lax.broadcasted_iota(jnp.int32, (r, r), 1)
    w_alr[...] = jnp.where(lane >= r - GLA_GATE_RANK, tail, 0.0).astype(BF16)
    w_a2[...] = jnp.zeros_like(w_a2)
    w_a2[LANES - GLA_GATE_RANK:, :] = gw_a2[0].astype(BF16)
    w_out0[...] = gw_out[0].astype(BF16)


def _cast_weights(gla_w_in, gla_w_a2, gla_w_out):
    r = CAST_ROWS
    assert r == LANES
    rows3 = lambda n: pl.BlockSpec((1, r, n), lambda i: (0, i, 0))
    rows2 = lambda n: pl.BlockSpec((r, n), lambda i: (i, 0))
    gla_w_in_t = jnp.swapaxes(gla_w_in, 1, 2)
    cols_t = pl.BlockSpec((1, gla_w_in_t.shape[1], r), lambda i: (0, 0, i))
    out_widths = [GLA_N_MAIN, LANES, None, D_MODEL]
    out_shapes = [(D_MODEL, w) if w else (LANES, GLA_DK_TOTAL) for w in out_widths]
    return pl.pallas_call(
        _cast_weights_kernel,
        grid=(D_MODEL // r,),
        in_specs=[cols_t, _full(gla_w_a2.shape), rows3(D_MODEL)],
        out_specs=[rows2(w) if w else pl.BlockSpec((LANES, GLA_DK_TOTAL), lambda i: (0, 0)) for w in out_widths],
        out_shape=[jax.ShapeDtypeStruct(s, BF16) for s in out_shapes],
        compiler_params=_params("arbitrary"),
        name="cast_weights",
    )(gla_w_in_t, gla_w_a2, gla_w_out)


def kernel(x, gla_w_in, gla_w_a2, gla_b_a2, gla_norm_g, gla_w_out, w_kv, swa_w_in, swa_w_out, ln_g, ln_b):
    bsz, seq, _ = x.shape
    assert seq % SWA_SPAN == 0 and seq % GLA_BLOCK == 0
    assert gla_w_in.shape[0] == 1 and swa_w_in.shape[0] == 1 and ln_g.shape[0] == DEPTH
    m = bsz * seq

    w_main, w_alr, w_a2, w_out0 = _cast_weights(gla_w_in, gla_w_a2, gla_w_out)
    x2, w_all, w_out1 = _gla_layer(x, w_main, w_alr, w_a2, gla_b_a2, gla_norm_g, w_out0, ln_g, ln_b,
                                   w_kv, swa_w_in, swa_w_out)
    x2 = x2.reshape(m, D_MODEL)

    ks, vs, qs, g = _swa_in_proj(x2, w_all, _rope_tables(seq), bsz, seq)
    os, sts = zip(*[_swa_attention(qs[gi], ks[gi], vs[gi]) for gi in range(SWA_N_GROUPS)])
    out = _merge_out_ln(os, sts, g, x2, w_out1, ln_g, ln_b, bsz, seq)
    return out.reshape(bsz, seq, D_MODEL)
```
